```python
import jax, jax.numpy as jnp
from jax import lax
import numpy as np

D_MODEL = 1024
BATCH = 8
SEQ = 2048
DEPTH = 1
DEC_BATCH = 128
DEC_SEQ = 1
PAST_LEN = 16384
PAGE_SIZE = 128

N_META = 16
EPS = 1e-6
SSD_D_INNER = 2 * D_MODEL
SSD_HEAD_DIM = 64
SSD_HEADS = SSD_D_INNER // SSD_HEAD_DIM
SSD_GROUPS = 4
SSD_D_STATE = 128
SSD_CONV = 4
SSD_CHUNK = 128
SSD_CONV_DIM = SSD_D_INNER + 2 * SSD_GROUPS * SSD_D_STATE
GLA_HEADS = 4
GLA_KEY_DIM = D_MODEL // 2
GLA_VAL_DIM = D_MODEL
GLA_DK = GLA_KEY_DIM // GLA_HEADS
GLA_DV = GLA_VAL_DIM // GLA_HEADS
GLA_GATE_RANK = 16
GLA_GATE_NORMALIZER = 16.0
GLA_CHUNK = 64
PEER_HEADS = 8
PEER_N_KEYS = 128
PEER_N_EXPERTS = PEER_N_KEYS * PEER_N_KEYS
PEER_DQ = 256
PEER_TOPK = 16
PEER_BLOCK = 256
IN_SIZES = (SSD_D_INNER, SSD_CONV_DIM, SSD_HEADS, GLA_KEY_DIM, GLA_KEY_DIM, GLA_VAL_DIM, GLA_VAL_DIM, GLA_GATE_RANK, 2 * D_MODEL)
IN_DIM = sum(IN_SIZES)

kernel_name = 'hybrid_ssd_gla_peer_step'


def rms(x):
    x32 = x.astype(jnp.float32)
    return x32 * lax.rsqrt(jnp.mean(x32 * x32, axis=-1, keepdims=True) + EPS)


def rmsnorm(x, w):
    return rms(x).astype(x.dtype) * w


def pad_chunks(t, front, chunk):
    back = (-(front + t.shape[1])) % chunk
    widths = [(0, 0)] * t.ndim
    widths[1] = (front, back)
    return jnp.pad(t, widths)


def ssd_chunked(xh, dt, a, bm, cm, s0, front):
    b, l = xh.shape[:2]
    q = SSD_CHUNK
    hpg = SSD_HEADS // SSD_GROUPS
    xh, dt, bm, cm = [pad_chunks(t, front, q) for t in (xh, dt, bm, cm)]
    nc = xh.shape[1] // q
    x = xh.reshape(b, nc, q, SSD_GROUPS, hpg, SSD_HEAD_DIM)
    dt = dt.reshape(b, nc, q, SSD_GROUPS, hpg)
    bm = bm.reshape(b, nc, q, SSD_GROUPS, SSD_D_STATE)
    cm = cm.reshape(b, nc, q, SSD_GROUPS, SSD_D_STATE)
    acum = jnp.cumsum(dt * a.reshape(SSD_GROUPS, hpg), axis=2)
    causal = jnp.tril(jnp.ones((q, q), dtype=bool))[:, :, None, None]
    seg = jnp.exp(jnp.where(causal, acum[:, :, :, None] - acum[:, :, None], -jnp.inf))
    xdt = x * dt[..., None]
    cb = jnp.einsum('bctgn,bcsgn->bctsg', cm, bm)
    y_diag = jnp.einsum('bctsg,bctsgh,bcsghp->bctghp', cb, seg, xdt)
    decay_to_end = jnp.exp(acum[:, :, -1:] - acum)
    chunk_states = jnp.einsum('bcsgn,bcsgh,bcsghp->bcghpn', bm, decay_to_end, xdt)
    chunk_decay = jnp.exp(acum[:, :, -1])

    def step(s, inp):
        cs, cd = inp
        return cd[..., None, None] * s + cs, s

    s_fin, s_in = lax.scan(step, s0.reshape(b, SSD_GROUPS, hpg, SSD_HEAD_DIM, SSD_D_STATE),
                           (jnp.moveaxis(chunk_states, 1, 0), jnp.moveaxis(chunk_decay, 1, 0)))
    s_in = jnp.moveaxis(s_in, 0, 1)
    y_off = jnp.einsum('bctgn,bcghpn,bctgh->bctghp', cm, s_in, jnp.exp(acum))
    y = (y_diag + y_off).reshape(b, nc * q, SSD_HEADS, SSD_HEAD_DIM)[:, front:front + l]
    return y, s_fin.reshape(b, SSD_HEADS, SSD_HEAD_DIM, SSD_D_STATE)


def gla_chunked(qh, kh, vh, gk, s0, front):
    b, l = qh.shape[:2]
    q = GLA_CHUNK
    qh, kh, vh, gk = [pad_chunks(t, front, q) for t in (qh, kh, vh, gk)]
    nc = qh.shape[1] // q
    qh = qh.reshape(b, nc, q, GLA_HEADS, GLA_DK)
    kh = kh.reshape(b, nc, q, GLA_HEADS, GLA_DK)
    vh = vh.reshape(b, nc, q, GLA_HEADS, GLA_DV)
    bcum = jnp.cumsum(gk.reshape(b, nc, q, GLA_HEADS, GLA_DK), axis=2)
    qe = qh * jnp.exp(bcum) * (GLA_DK ** -0.5)
    ke = kh * jnp.exp(-bcum)
    kend = kh * jnp.exp(bcum[:, :, -1:] - bcum)
    causal = jnp.tril(jnp.ones((q, q), dtype=bool))
    att = jnp.where(causal, jnp.einsum('bcthk,bcshk->bchts', qe, ke), 0.0)
    o_intra = jnp.einsum('bchts,bcshv->bcthv', att, vh)
    chunk_states = jnp.einsum('bcshk,bcshv->bchkv', kend, vh)
    chunk_decay = jnp.exp(bcum[:, :, -1])

    def step(s, inp):
        cs, cd = inp
        return cd[..., None] * s + cs, s

    s_fin, s_in = lax.scan(step, s0, (jnp.moveaxis(chunk_states, 1, 0), jnp.moveaxis(chunk_decay, 1, 0)))
    s_in = jnp.moveaxis(s_in, 0, 1)
    o = o_intra + jnp.einsum('bcthk,bchkv->bcthv', qe, s_in)
    return o.reshape(b, nc * q, GLA_HEADS, GLA_DV)[:, front:front + l], s_fin


def peer(xn, w_q, sub_keys, expert_u, expert_v):
    t_all = xn.shape[0]
    pad = (-t_all) % PEER_BLOCK
    blocks = jnp.pad(xn, ((0, pad), (0, 0))).reshape(-1, PEER_BLOCK, D_MODEL)

    def block(xb):
        qh = (xb @ w_q).reshape(PEER_BLOCK, PEER_HEADS, 2, PEER_DQ // 2)
        scores = jnp.einsum('thcd,chkd->thck', qh, sub_keys).astype(jnp.float32)
        sv, si = lax.top_k(scores, PEER_TOPK)
        cand = (sv[:, :, 0, :, None] + sv[:, :, 1, None, :]).reshape(PEER_BLOCK, PEER_HEADS, -1)
        cidx = (si[:, :, 0, :, None] * PEER_N_KEYS + si[:, :, 1, None, :]).reshape(PEER_BLOCK, PEER_HEADS, -1)
        best, pos = lax.top_k(cand, PEER_TOPK)
        idx = jnp.take_along_axis(cidx, pos, axis=-1)
        gate = jax.nn.softmax(best, axis=-1).astype(xb.dtype)
        hid = jax.nn.gelu(jnp.einsum('thkd,td->thk', expert_u[idx], xb), approximate=False)
        return jnp.einsum('thk,thkd->td', gate * hid, expert_v[idx])

    return lax.map(block, blocks).reshape(-1, D_MODEL)[:t_all]


def hybrid_layer(x, ssd_s, conv_s, gla_s, start, norm1_w, w_in, conv_w, conv_b, dt_bias, a_log, d_skip,
                 ssd_norm_w, w_ssd_out, gla_gate_up, gla_gate_b, gla_norm_w, w_gla_out, w_out,
                 norm2_w, peer_w_q, peer_sub_keys, peer_u, peer_v):
    b, l, _ = x.shape
    xn = rmsnorm(x, norm1_w)
    splits = np.cumsum(IN_SIZES)[:-1].tolist()
    z, xbc, dt_raw, q, k, v, g, g_lr, merge = jnp.split(xn @ w_in, splits, axis=-1)
    hist = jnp.concatenate([conv_s.astype(xbc.dtype), xbc], axis=1)
    conv = conv_b
    for j in range(SSD_CONV):
        conv = conv + conv_w[j] * hist[:, j:j + l]
    xbc_c = jax.nn.silu(conv)
    xs, bs, cs = jnp.split(xbc_c, [SSD_D_INNER, SSD_D_INNER + SSD_GROUPS * SSD_D_STATE], axis=-1)
    dt = jax.nn.softplus((dt_raw + dt_bias).astype(jnp.float32))
    a = -jnp.exp(a_log.astype(jnp.float32))
    xh = xs.reshape(b, l, SSD_HEADS, SSD_HEAD_DIM).astype(jnp.float32)
    y, ssd_new = ssd_chunked(xh, dt, a,
                             bs.reshape(b, l, SSD_GROUPS, SSD_D_STATE).astype(jnp.float32),
                             cs.reshape(b, l, SSD_GROUPS, SSD_D_STATE).astype(jnp.float32),
                             ssd_s.astype(jnp.float32), start % SSD_CHUNK)
    y = (y + d_skip[:, None] * xh).reshape(b, l, SSD_D_INNER) * jax.nn.silu(z.astype(jnp.float32))
    y = rms(y.reshape(b, l, SSD_GROUPS, -1)).reshape(b, l, SSD_D_INNER) * ssd_norm_w
    y_ssd = y.astype(x.dtype) @ w_ssd_out
    gk = jax.nn.log_sigmoid((g_lr @ gla_gate_up + gla_gate_b).astype(jnp.float32)) / GLA_GATE_NORMALIZER
    o, gla_new = gla_chunked(q.reshape(b, l, GLA_HEADS, GLA_DK).astype(jnp.float32),
                             k.reshape(b, l, GLA_HEADS, GLA_DK).astype(jnp.float32),
                             v.reshape(b, l, GLA_HEADS, GLA_DV).astype(jnp.float32),
                             gk.reshape(b, l, GLA_HEADS, GLA_DK), gla_s.astype(jnp.float32), start % GLA_CHUNK)
    o = rms(o) * gla_norm_w * jax.nn.silu(g.reshape(b, l, GLA_HEADS, GLA_DV).astype(jnp.float32))
    y_gla = o.reshape(b, l, GLA_VAL_DIM).astype(x.dtype) @ w_gla_out
    gate_ssd, gate_gla = jnp.split(jax.nn.sigmoid(merge), 2, axis=-1)
    x = x + (gate_ssd * y_ssd + gate_gla * y_gla) @ w_out
    hn = rmsnorm(x, norm2_w).reshape(b * l, D_MODEL)
    x = x + peer(hn, peer_w_q, peer_sub_keys, peer_u, peer_v).reshape(b, l, D_MODEL)
    new_conv = hist[:, -(SSD_CONV - 1):]
    return x, ssd_new.astype(ssd_s.dtype), new_conv.astype(conv_s.dtype), gla_new.astype(gla_s.dtype)


def setup_inputs(seed: int = 0) -> dict:
    key = jax.random.key(seed)
    ks = jax.random.split(key, 32)

    def nrm(k, shape, scale):
        return jax.random.normal(k, shape, jnp.float32) * scale

    dt0 = jnp.exp(jax.random.uniform(ks[8], (DEPTH, SSD_HEADS), jnp.float32, np.log(1e-3), np.log(1e-1)))
    return {
        'x_prompt': nrm(ks[0], (BATCH, SEQ, D_MODEL), 1.0),
        'x_sample': nrm(ks[1], (DEC_BATCH, DEC_SEQ, D_MODEL), 1.0),
        'state_ssd': nrm(ks[2], (DEPTH, DEC_BATCH, SSD_HEADS, SSD_HEAD_DIM, SSD_D_STATE), 0.1),
        'state_conv': nrm(ks[3], (DEPTH, DEC_BATCH, SSD_CONV - 1, SSD_CONV_DIM), 1.0),
        'state_gla': nrm(ks[4], (DEPTH, DEC_BATCH, GLA_HEADS, GLA_DK, GLA_DV), 1.0),
        'meta_tokens': nrm(ks[5], (N_META, D_MODEL), 1.0),
        'norm1_w': 1.0 + nrm(ks[6], (DEPTH, D_MODEL), 0.02),
        'w_in': nrm(ks[7], (DEPTH, D_MODEL, IN_DIM), D_MODEL ** -0.5),
        'conv_w': nrm(ks[9], (DEPTH, SSD_CONV, SSD_CONV_DIM), SSD_CONV ** -0.5),
        'conv_b': nrm(ks[10], (DEPTH, SSD_CONV_DIM), 0.02),
        'dt_bias': dt0 + jnp.log(-jnp.expm1(-dt0)),
        'a_log': jnp.log(jax.random.uniform(ks[11], (DEPTH, SSD_HEADS), jnp.float32, 1.0, 16.0)),
        'd_skip': 1.0 + nrm(ks[12], (DEPTH, SSD_HEADS), 0.02),
        'ssd_norm_w': 1.0 + nrm(ks[13], (DEPTH, SSD_D_INNER), 0.02),
        'w_ssd_out': nrm(ks[14], (DEPTH, SSD_D_INNER, D_MODEL), SSD_D_INNER ** -0.5),
        'gla_gate_up': nrm(ks[15], (DEPTH, GLA_GATE_RANK, GLA_KEY_DIM), GLA_GATE_RANK ** -0.5),
        'gla_gate_b': nrm(ks[16], (DEPTH, GLA_KEY_DIM), 0.02),
        'gla_norm_w': 1.0 + nrm(ks[17], (DEPTH, GLA_DV), 0.02),
        'w_gla_out': nrm(ks[18], (DEPTH, GLA_VAL_DIM, D_MODEL), GLA_VAL_DIM ** -0.5),
        'w_out': nrm(ks[19], (DEPTH, D_MODEL, D_MODEL), D_MODEL ** -0.5),
        'norm2_w': 1.0 + nrm(ks[20], (DEPTH, D_MODEL), 0.02),
        'peer_w_q': nrm(ks[21], (DEPTH, D_MODEL, PEER_HEADS * PEER_DQ), D_MODEL ** -0.5),
        'peer_sub_keys': nrm(ks[22], (DEPTH, 2, PEER_HEADS, PEER_N_KEYS, PEER_DQ // 2), (PEER_DQ // 2) ** -0.5),
        'peer_u': nrm(ks[23], (DEPTH, PEER_N_EXPERTS, D_MODEL), D_MODEL ** -0.5),
        'peer_v': nrm(ks[24], (DEPTH, PEER_N_EXPERTS, D_MODEL), PEER_HEADS ** -0.5),
        'final_norm_w': 1.0 + nrm(ks[25], (D_MODEL,), 0.02),
    }


def reference(x_prompt, x_sample, state_ssd, state_conv, state_gla, meta_tokens, norm1_w, w_in, conv_w,
              conv_b, dt_bias, a_log, d_skip, ssd_norm_w, w_ssd_out, gla_gate_up, gla_gate_b, gla_norm_w,
              w_gla_out, w_out, norm2_w, peer_w_q, peer_sub_keys, peer_u, peer_v, final_norm_w):
    layer_w = (norm1_w, w_in, conv_w, conv_b, dt_bias, a_log, d_skip, ssd_norm_w, w_ssd_out, gla_gate_up,
               gla_gate_b, gla_norm_w, w_gla_out, w_out, norm2_w, peer_w_q, peer_sub_keys, peer_u, peer_v)

    def run(x, s_ssd, s_conv, s_gla, start):
        new_ssd, new_conv, new_gla = [], [], []
        for i in range(DEPTH):
            x, s1, s2, s3 = hybrid_layer(x, s_ssd[i], s_conv[i], s_gla[i], start, *[w[i] for w in layer_w])
            new_ssd.append(s1)
            new_conv.append(s2)
            new_gla.append(s3)
        return rmsnorm(x, final_norm_w), jnp.stack(new_ssd), jnp.stack(new_conv), jnp.stack(new_gla)

    b = x_prompt.shape[0]
    dt = x_prompt.dtype
    meta = jnp.broadcast_to(meta_tokens.astype(dt)[None], (b, N_META, D_MODEL))
    xp = jnp.concatenate([meta, x_prompt], axis=1)
    z_ssd = jnp.zeros((DEPTH, b, SSD_HEADS, SSD_HEAD_DIM, SSD_D_STATE), dt)
    z_conv = jnp.zeros((DEPTH, b, SSD_CONV - 1, SSD_CONV_DIM), dt)
    z_gla = jnp.zeros((DEPTH, b, GLA_HEADS, GLA_DK, GLA_DV), dt)
    yp, p_ssd, p_conv, p_gla = run(xp, z_ssd, z_conv, z_gla, -N_META)
    ys, s_ssd, s_conv, s_gla = run(x_sample, state_ssd, state_conv, state_gla, PAST_LEN)
    return (yp[:, N_META:], ys, p_ssd, p_conv, p_gla, s_ssd, s_conv, s_gla)
```

```python
import functools

import jax
import jax.numpy as jnp
from jax import lax
from jax.experimental import pallas as pl
from jax.experimental.pallas import tpu as pltpu

F32 = jnp.float32
BF16 = jnp.bfloat16

LANES = 128
SUBLANES = 8
VMEM_LIMIT_BYTES = 56 * 1024 * 1024

D_MODEL = 1024
N_META = 16
EPS = 1e-6
SSD_D_INNER = 2 * D_MODEL
SSD_HEAD_DIM = 64
SSD_HEADS = SSD_D_INNER // SSD_HEAD_DIM
SSD_GROUPS = 4
SSD_HPG = SSD_HEADS // SSD_GROUPS
SSD_D_STATE = 128
SSD_CONV = 4
SSD_CHUNK = 128
SSD_BC = 2 * SSD_GROUPS * SSD_D_STATE
SSD_CONV_DIM = SSD_D_INNER + SSD_BC
SSD_GROUP_W = SSD_HPG * SSD_HEAD_DIM
GLA_HEADS = 4
GLA_KEY_DIM = D_MODEL // 2
GLA_VAL_DIM = D_MODEL
GLA_DK = GLA_KEY_DIM // GLA_HEADS
GLA_DV = GLA_VAL_DIM // GLA_HEADS
GLA_GATE_RANK = 16
GLA_GATE_NORMALIZER = 16.0
GLA_CHUNK = 64
PEER_HEADS = 8
PEER_N_KEYS = 128
PEER_N_EXPERTS = PEER_N_KEYS * PEER_N_KEYS
PEER_DQ = 256
PEER_TOPK = 16
PEER_SLABS_PER_STEP = 4
PEER_STEP_EXPERTS = PEER_SLABS_PER_STEP * PEER_N_KEYS
PEER_PITCH = PEER_N_KEYS + SUBLANES
PEER_TOKEN_BLOCK = 256

_IN_SIZES = (SSD_D_INNER, SSD_CONV_DIM, SSD_HEADS, GLA_KEY_DIM, GLA_KEY_DIM, GLA_VAL_DIM,
             GLA_VAL_DIM, GLA_GATE_RANK, 2 * D_MODEL)
COL_Z = 0
COL_XS = COL_Z + SSD_D_INNER
COL_BC = COL_XS + SSD_D_INNER
COL_Q = COL_BC + SSD_BC
COL_K = COL_Q + GLA_KEY_DIM
COL_V = COL_K + GLA_KEY_DIM
COL_G = COL_V + GLA_VAL_DIM
COL_SMALL = COL_G + GLA_VAL_DIM
IN_COLS = COL_SMALL + LANES
SMALL_GLR = SSD_HEADS
PROMPT_ZERO_ROWS = SSD_CHUNK - N_META
PROMPT_SKIP_ROWS = PROMPT_ZERO_ROWS + N_META


def _cparams(*sem):
    return pltpu.CompilerParams(dimension_semantics=sem, vmem_limit_bytes=VMEM_LIMIT_BYTES)


def _const_spec(shape):
    nd = len(shape)
    return pl.BlockSpec(shape, lambda *_: (0,) * nd, pipeline_mode=pl.Buffered(1))


def _row_tile(t, cap):
    tm = cap
    while t % tm:
        tm //= 2
    return tm


def _split3(x):
    hi = x.astype(BF16)
    r1 = x - hi.astype(F32)
    mid = r1.astype(BF16)
    lo = (r1 - mid.astype(F32)).astype(BF16)
    return hi, mid, lo


def _dot(a, b):
    return jnp.dot(a, b, preferred_element_type=F32)


def _dot_nt(a, b):
    return lax.dot_general(a, b, (((1,), (1,)), ((), ())), preferred_element_type=F32)


def _sel_right(sel01, x):
    hi, mid, lo = _split3(x)
    return _dot(hi, sel01) + _dot(mid, sel01) + _dot(lo, sel01)


def _sel_left(sel01, x):
    hi, mid, lo = _split3(x)
    return _dot(sel01, hi) + _dot(sel01, mid) + _dot(sel01, lo)


def _rms_scale(x):
    return x * lax.rsqrt(jnp.mean(x * x, axis=-1, keepdims=True) + EPS)


def _tril(n):
    r = lax.broadcasted_iota(jnp.int32, (n, n), 0)
    c = lax.broadcasted_iota(jnp.int32, (n, n), 1)
    return r >= c


def _norm_matmul_kernel(x_ref, nw_ref, w_ref, o_ref, xn_ref):
    @pl.when(pl.program_id(1) == 0)
    def _():
        xn_ref[...] = (_rms_scale(x_ref[...]) * nw_ref[...]).astype(BF16)

    o_ref[...] = _dot(xn_ref[...], w_ref[...])


def norm_matmul(x, norm_w, w, tn):
    t, d = x.shape
    n = w.shape[1]
    tm = _row_tile(t, 1024)
    assert n % tn == 0
    return pl.pallas_call(
        _norm_matmul_kernel,
        grid=(t // tm, n // tn),
        in_specs=[
            pl.BlockSpec((tm, d), lambda i, j: (i, 0)),
            pl.BlockSpec((1, d), lambda i, j: (0, 0)),
            pl.BlockSpec((d, tn), lambda i, j: (0, j)),
        ],
        out_specs=pl.BlockSpec((tm, tn), lambda i, j: (i, j)),
        out_shape=jax.ShapeDtypeStruct((t, n), F32),
        scratch_shapes=[pltpu.VMEM((tm, d), BF16)],
        compiler_params=_cparams("parallel", "arbitrary"),
        name="norm_in_proj",
    )(x, norm_w, w)


def _ssd_dt(small, dt_bias_ref, valid):
    lane = lax.broadcasted_iota(jnp.int32, small.shape, 1)
    dt = jax.nn.softplus(small + dt_bias_ref[...])
    keep = lane < SSD_HEADS
    if valid is not None:
        keep = jnp.logical_and(keep, valid)
    return jnp.where(keep, dt, 0.0)


def _ssd_gated_norm(y, z, norm_w_ref):
    y = y * jax.nn.silu(z)
    parts = []
    for g in range(SSD_GROUPS):
        parts.append(_rms_scale(y[:, g * SSD_GROUP_W:(g + 1) * SSD_GROUP_W]))
    return jnp.concatenate(parts, axis=1) * norm_w_ref[...]


def _ssd_prompt_kernel(z_ref, xs_ref, bc_ref, small_ref, conv_w_ref, conv_b_ref, dt_bias_ref,
                       a_log_ref, d_exp_ref, norm_w_ref, expand_ref,
                       y_ref, state_ref, convtail_ref,
                       hist_ref, st_ref, yacc_ref, *, zero_rows):
    c = pl.program_id(1)
    nc = pl.num_programs(1)
    q = SSD_CHUNK

    @pl.when(c == 0)
    def _():
        hist_ref[0:SUBLANES, :] = jnp.zeros((SUBLANES, SSD_CONV_DIM), F32)
        st_ref[...] = jnp.zeros(st_ref.shape, F32)

    hist_ref[SUBLANES:SUBLANES + q, 0:SSD_D_INNER] = xs_ref[...]
    hist_ref[SUBLANES:SUBLANES + q, SSD_D_INNER:SSD_CONV_DIM] = bc_ref[...]
    conv = conv_b_ref[...]
    for j in range(SSD_CONV):
        start = SUBLANES - (SSD_CONV - 1) + j
        conv = conv + conv_w_ref[j:j + 1, :] * hist_ref[start:start + q, :]
    xbc = jax.nn.silu(conv)
    tail = hist_ref[q:q + SUBLANES, :]
    hist_ref[0:SUBLANES, :] = tail
    convtail_ref[0] = tail

    xs = xbc[:, 0:SSD_D_INNER]
    bm = xbc[:, SSD_D_INNER:SSD_D_INNER + SSD_GROUPS * SSD_D_STATE].astype(BF16)
    cm = xbc[:, SSD_D_INNER + SSD_GROUPS * SSD_D_STATE:].astype(BF16)

    row = lax.broadcasted_iota(jnp.int32, (q, LANES), 0) + c * q
    dt = _ssd_dt(small_ref[...], dt_bias_ref, row >= zero_rows)
    da = dt * (-jnp.exp(a_log_ref[...]))
    tril = _tril(q)
    acum = _sel_left(tril.astype(BF16), da)
    acum_t = acum.T
    expand = expand_ref[...]
    dt_exp = _sel_right(expand, dt)
    acum_exp = _sel_right(expand, acum)
    acum_last = acum_exp[q - 1:q, :]
    xdt = xs * dt_exp
    xdt_bf = xdt.astype(BF16)
    x_end = (xdt * jnp.exp(acum_last - acum_exp)).astype(BF16)

    lane_lo = lax.broadcasted_iota(jnp.int32, (q, LANES), 1) < SSD_HEAD_DIM
    for g in range(SSD_GROUPS):
        bg = bm[:, g * SSD_D_STATE:(g + 1) * SSD_D_STATE]
        cg = cm[:, g * SSD_D_STATE:(g + 1) * SSD_D_STATE]
        cb = _dot_nt(cg, bg)
        for hp in range(SSD_HPG // 2):
            lo = (g * SSD_HPG + 2 * hp) * SSD_HEAD_DIM
            pair = []
            for k in range(2):
                h = g * SSD_HPG + 2 * hp + k
                diff = acum[:, h:h + 1] - acum_t[h:h + 1, :]
                seg = jnp.exp(jnp.where(tril, diff, -jnp.inf))
                pair.append(_dot((cb * seg).astype(BF16), xdt_bf[:, lo:lo + LANES]))
            yacc_ref[:, lo:lo + LANES] = jnp.where(lane_lo, pair[0], pair[1])

        gs = slice(g * SSD_GROUP_W, (g + 1) * SSD_GROUP_W)
        st_g = st_ref[:, gs]
        y_off = _dot(cg, st_g.astype(BF16)) * jnp.exp(acum_exp[:, gs])
        yacc_ref[:, gs] = yacc_ref[:, gs] + y_off
        st_ref[:, gs] = st_g * jnp.exp(acum_last[:, gs]) + _dot(bg.T, x_end[:, gs])

    y = yacc_ref[...] + d_exp_ref[...] * xs
    y_ref[0] = _ssd_gated_norm(y, z_ref[...], norm_w_ref).astype(BF16)

    @pl.when(c == nc - 1)
    def _():
        state_ref[0] = st_ref[...].T


def ssd_prompt(proj, batch, n_chunks, zero_rows, skip_rows, w):
    q = SSD_CHUNK
    assert skip_rows % q == 0
    first_real = skip_rows // q
    n_out = n_chunks - first_real

    def rows(b, c):
        return b * n_chunks + c

    return pl.pallas_call(
        functools.partial(_ssd_prompt_kernel, zero_rows=zero_rows),
        grid=(batch, n_chunks),
        in_specs=[
            pl.BlockSpec((q, SSD_D_INNER), lambda b, c: (rows(b, c), COL_Z // SSD_D_INNER)),
            pl.BlockSpec((q, SSD_D_INNER), lambda b, c: (rows(b, c), COL_XS // SSD_D_INNER)),
            pl.BlockSpec((q, SSD_BC), lambda b, c: (rows(b, c), COL_BC // SSD_BC)),
            pl.BlockSpec((q, LANES), lambda b, c: (rows(b, c), COL_SMALL // LANES)),
            _const_spec((SSD_CONV, SSD_CONV_DIM)),
            _const_spec((1, SSD_CONV_DIM)),
            _const_spec((1, LANES)),
            _const_spec((1, LANES)),
            _const_spec((1, SSD_D_INNER)),
            _const_spec((1, SSD_D_INNER)),
            _const_spec((LANES, SSD_D_INNER)),
        ],
        out_specs=[
            pl.BlockSpec((1, q, SSD_D_INNER), lambda b, c: (b, jnp.maximum(c - first_real, 0), 0)),
            pl.BlockSpec((1, SSD_D_INNER, SSD_D_STATE), lambda b, c: (b, 0, 0)),
            pl.BlockSpec((1, SUBLANES, SSD_CONV_DIM), lambda b, c: (b, 0, 0)),
        ],
        out_shape=[
            jax.ShapeDtypeStruct((batch, n_out * q, SSD_D_INNER), BF16),
            jax.ShapeDtypeStruct((batch, SSD_D_INNER, SSD_D_STATE), F32),
            jax.ShapeDtypeStruct((batch, SUBLANES, SSD_CONV_DIM), F32),
        ],
        scratch_shapes=[
            pltpu.VMEM((SUBLANES + q, SSD_CONV_DIM), F32),
            pltpu.VMEM((SSD_D_STATE, SSD_D_INNER), F32),
            pltpu.VMEM((q, SSD_D_INNER), F32),
        ],
        compiler_params=_cparams("parallel", "arbitrary"),
        name="ssd_prompt",
    )(proj, proj, proj, proj, w["conv_w"], w["conv_b"], w["dt_bias"], w["a_log"], w["d_exp"],
      w["ssd_norm_w"], w["expand"])


def _ssd_step_pre_kernel(xs_ref, bc_ref, small_ref, c0_ref, c1_ref, c2_ref, conv_w_ref, conv_b_ref,
                         dt_bias_ref, a_log_ref, expand_ref,
                         xact_ref, xdt_ref, decay_ref, bm_ref, cm_ref, convnew_ref):
    x = jnp.concatenate([xs_ref[...], bc_ref[...]], axis=1)
    hist = (c0_ref[...], c1_ref[...], c2_ref[...], x)
    conv = conv_b_ref[...]
    for j in range(SSD_CONV):
        conv = conv + conv_w_ref[j:j + 1, :] * hist[j]
    xbc = jax.nn.silu(conv)
    xs = xbc[:, 0:SSD_D_INNER]
    dt = _ssd_dt(small_ref[...], dt_bias_ref, None)
    da = dt * (-jnp.exp(a_log_ref[...]))
    expand = expand_ref[...]
    xact_ref[...] = xs
    xdt_ref[...] = xs * _sel_right(expand, dt)
    decay_ref[...] = jnp.exp(_sel_right(expand, da))
    bm_ref[...] = xbc[:, SSD_D_INNER:SSD_D_INNER + SSD_GROUPS * SSD_D_STATE]
    cm_ref[...] = xbc[:, SSD_D_INNER + SSD_GROUPS * SSD_D_STATE:]
    convnew_ref[:, 0:SSD_CONV_DIM] = hist[1]
    convnew_ref[:, SSD_CONV_DIM:2 * SSD_CONV_DIM] = hist[2]
    convnew_ref[:, 2 * SSD_CONV_DIM:] = x


def ssd_step_pre(proj, conv_state, w):
    t = proj.shape[0]
    tm = _row_tile(t, 128)
    gw = SSD_GROUPS * SSD_D_STATE
    row = lambda width, col: pl.BlockSpec((tm, width), lambda i: (i, col))
    return pl.pallas_call(
        _ssd_step_pre_kernel,
        grid=(t // tm,),
        in_specs=[
            row(SSD_D_INNER, COL_XS // SSD_D_INNER),
            row(SSD_BC, COL_BC // SSD_BC),
            row(LANES, COL_SMALL // LANES),
            row(SSD_CONV_DIM, 0), row(SSD_CONV_DIM, 1), row(SSD_CONV_DIM, 2),
            _const_spec((SSD_CONV, SSD_CONV_DIM)),
            _const_spec((1, SSD_CONV_DIM)),
            _const_spec((1, LANES)),
            _const_spec((1, LANES)),
            _const_spec((LANES, SSD_D_INNER)),
        ],
        out_specs=[row(SSD_D_INNER, 0), row(SSD_D_INNER, 0), row(SSD_D_INNER, 0), row(gw, 0),
                   row(gw, 0), row((SSD_CONV - 1) * SSD_CONV_DIM, 0)],
        out_shape=[
            jax.ShapeDtypeStruct((t, SSD_D_INNER), F32),
            jax.ShapeDtypeStruct((t, SSD_D_INNER), F32),
            jax.ShapeDtypeStruct((t, SSD_D_INNER), F32),
            jax.ShapeDtypeStruct((t, gw), F32),
            jax.ShapeDtypeStruct((t, gw), F32),
            jax.ShapeDtypeStruct((t, (SSD_CONV - 1) * SSD_CONV_DIM), F32),
        ],
        compiler_params=_cparams("parallel"),
        name="ssd_step_pre",
    )(proj, proj, proj, conv_state, conv_state, conv_state, w["conv_w"], w["conv_b"], w["dt_bias"],
      w["a_log"], w["expand"])


def _ssd_step_state_kernel(s_ref, xdt_ref, decay_ref, bm_ref, cm_ref, snew_ref, y_ref):
    def per_group(row_ref):
        parts = [jnp.broadcast_to(row_ref[0][:, g * SSD_D_STATE:(g + 1) * SSD_D_STATE],
                                  (SSD_GROUP_W, SSD_D_STATE)) for g in range(SSD_GROUPS)]
        return jnp.concatenate(parts, axis=0)

    s_new = s_ref[0] * decay_ref[0] + xdt_ref[0] * per_group(bm_ref)
    snew_ref[0] = s_new
    y_ref[0] = jnp.sum(s_new * per_group(cm_ref), axis=1, keepdims=True)


def ssd_step_state(state, xdt, decay, bm, cm):
    t = state.shape[0]
    gw = SSD_GROUPS * SSD_D_STATE
    col = pl.BlockSpec((1, SSD_D_INNER, 1), lambda i: (i, 0, 0))
    return pl.pallas_call(
        _ssd_step_state_kernel,
        grid=(t,),
        in_specs=[
            pl.BlockSpec((1, SSD_D_INNER, SSD_D_STATE), lambda i: (i, 0, 0)),
            col, col,
            pl.BlockSpec((1, 1, gw), lambda i: (i, 0, 0)),
            pl.BlockSpec((1, 1, gw), lambda i: (i, 0, 0)),
        ],
        out_specs=[pl.BlockSpec((1, SSD_D_INNER, SSD_D_STATE), lambda i: (i, 0, 0)), col],
        out_shape=[
            jax.ShapeDtypeStruct((t, SSD_D_INNER, SSD_D_STATE), F32),
            jax.ShapeDtypeStruct((t, SSD_D_INNER, 1), F32),
        ],
        compiler_params=_cparams("parallel"),
        name="ssd_step_state",
    )(state, xdt, decay, bm, cm)


def _ssd_step_post_kernel(y_ref, xact_ref, z_ref, d_exp_ref, norm_w_ref, o_ref):
    y = y_ref[...] + d_exp_ref[...] * xact_ref[...]
    o_ref[...] = _ssd_gated_norm(y, z_ref[...], norm_w_ref).astype(BF16)


def ssd_step_post(y, xact, proj, w):
    t = y.shape[0]
    tm = _row_tile(t, 128)
    row = lambda col: pl.BlockSpec((tm, SSD_D_INNER), lambda i: (i, col))
    return pl.pallas_call(
        _ssd_step_post_kernel,
        grid=(t // tm,),
        in_specs=[row(0), row(0), row(COL_Z // SSD_D_INNER), _const_spec((1, SSD_D_INNER)),
                  _const_spec((1, SSD_D_INNER))],
        out_specs=row(0),
        out_shape=jax.ShapeDtypeStruct((t, SSD_D_INNER), BF16),
        compiler_params=_cparams("parallel"),
        name="ssd_step_post",
    )(y, xact, proj, w["d_exp"], w["ssd_norm_w"])


def _gla_gate(small, gate_up_ref, gate_b_ref):
    pre = _dot(small.astype(BF16), gate_up_ref[...]) + gate_b_ref[...]
    return jax.nn.log_sigmoid(pre) / GLA_GATE_NORMALIZER


def _gla_out_norm(o, g, norm_w_ref):
    parts = []
    for h in range(GLA_HEADS):
        parts.append(_rms_scale(o[:, h * GLA_DV:(h + 1) * GLA_DV]))
    return jnp.concatenate(parts, axis=1) * norm_w_ref[...] * jax.nn.silu(g)


def _gla_prompt_kernel(q_ref, k_ref, v_ref, g_ref, small_ref, gate_up_ref, gate_b_ref, norm_w_ref,
                       o_ref, state_ref, s_ref, *, zero_rows):
    c = pl.program_id(1)
    nc = pl.num_programs(1)
    q = GLA_CHUNK

    @pl.when(c == 0)
    def _():
        s_ref[...] = jnp.zeros(s_ref.shape, F32)

    valid = (lax.broadcasted_iota(jnp.int32, (q, GLA_KEY_DIM), 0) + c * q) >= zero_rows
    gk = jnp.where(valid, _gla_gate(small_ref[...], gate_up_ref, gate_b_ref), 0.0)
    kk = jnp.where(valid, k_ref[...], 0.0)
    tril = _tril(q)
    bcum = _sel_left(tril.astype(BF16), gk)
    bcum_last = bcum[q - 1:q, :]
    qe = q_ref[...] * jnp.exp(bcum) * (GLA_DK ** -0.5)
    ke = kk * jnp.exp(-bcum)
    kend = kk * jnp.exp(bcum_last - bcum)
    decay = jnp.exp(bcum_last)
    vv = v_ref[...].astype(BF16)
    outs = []
    for h in range(GLA_HEADS):
        ks = slice(h * GLA_DK, (h + 1) * GLA_DK)
        vs = slice(h * GLA_DV, (h + 1) * GLA_DV)
        qe_h = qe[:, ks].astype(BF16)
        att = jnp.where(tril, _dot_nt(qe_h, ke[:, ks].astype(BF16)), 0.0)
        s_h = s_ref[ks, :]
        outs.append(_dot(att.astype(BF16), vv[:, vs]) + _dot(qe_h, s_h.astype(BF16)))
        decay_col = jnp.broadcast_to(decay[:, ks], (GLA_DK, GLA_DK)).T
        kend_pad = jnp.concatenate([kend[:, ks], jnp.zeros((GLA_DK - q, GLA_DK), F32)], axis=0)
        kend_t = kend_pad.T[:, 0:q].astype(BF16)
        s_ref[ks, :] = (jnp.concatenate([decay_col] * (GLA_DV // GLA_DK), axis=1) * s_h
                        + _dot(kend_t, vv[:, vs]))
    o = jnp.concatenate(outs, axis=1)
    o_ref[0] = _gla_out_norm(o, g_ref[...], norm_w_ref).astype(BF16)

    @pl.when(c == nc - 1)
    def _():
        state_ref[0] = s_ref[...]


def gla_prompt(proj, batch, n_chunks, zero_rows, skip_rows, w):
    q = GLA_CHUNK
    assert skip_rows % q == 0
    first_real = skip_rows // q
    n_out = n_chunks - first_real

    def rows(b, c):
        return b * n_chunks + c

    return pl.pallas_call(
        functools.partial(_gla_prompt_kernel, zero_rows=zero_rows),
        grid=(batch, n_chunks),
        in_specs=[
            pl.BlockSpec((q, GLA_KEY_DIM), lambda b, c: (rows(b, c), COL_Q // GLA_KEY_DIM)),
            pl.BlockSpec((q, GLA_KEY_DIM), lambda b, c: (rows(b, c), COL_K // GLA_KEY_DIM)),
            pl.BlockSpec((q, GLA_VAL_DIM), lambda b, c: (rows(b, c), COL_V // GLA_VAL_DIM)),
            pl.BlockSpec((q, GLA_VAL_DIM), lambda b, c: (rows(b, c), COL_G // GLA_VAL_DIM)),
            pl.BlockSpec((q, LANES), lambda b, c: (rows(b, c), COL_SMALL // LANES)),
            _const_spec((LANES, GLA_KEY_DIM)),
            _const_spec((1, GLA_KEY_DIM)),
            _const_spec((1, GLA_VAL_DIM)),
        ],
        out_specs=[
            pl.BlockSpec((1, q, GLA_VAL_DIM), lambda b, c: (b, jnp.maximum(c - first_real, 0), 0)),
            pl.BlockSpec((1, GLA_KEY_DIM, GLA_DV), lambda b, c: (b, 0, 0)),
        ],
        out_shape=[
            jax.ShapeDtypeStruct((batch, n_out * q, GLA_VAL_DIM), BF16),
            jax.ShapeDtypeStruct((batch, GLA_KEY_DIM, GLA_DV), F32),
        ],
        scratch_shapes=[pltpu.VMEM((GLA_KEY_DIM, GLA_DV), F32)],
        compiler_params=_cparams("parallel", "arbitrary"),
        name="gla_prompt",
    )(proj, proj, proj, proj, proj, w["gate_up"], w["gate_b"], w["gla_norm_w"])


def _gla_step_pre_kernel(q_ref, k_ref, small_ref, gate_up_ref, gate_b_ref, qs_ref, kc_ref, decay_ref):
    qs_ref[...] = q_ref[...] * (GLA_DK ** -0.5)
    kc_ref[...] = k_ref[...]
    decay_ref[...] = jnp.exp(_gla_gate(small_ref[...], gate_up_ref, gate_b_ref))


def gla_step_pre(proj, w):
    t = proj.shape[0]
    tm = _row_tile(t, 128)
    row = lambda width, col: pl.BlockSpec((tm, width), lambda i: (i, col))
    out = jax.ShapeDtypeStruct((t, GLA_KEY_DIM), F32)
    return pl.pallas_call(
        _gla_step_pre_kernel,
        grid=(t // tm,),
        in_specs=[row(GLA_KEY_DIM, COL_Q // GLA_KEY_DIM), row(GLA_KEY_DIM, COL_K // GLA_KEY_DIM),
                  row(LANES, COL_SMALL // LANES), _const_spec((LANES, GLA_KEY_DIM)),
                  _const_spec((1, GLA_KEY_DIM))],
        out_specs=[row(GLA_KEY_DIM, 0)] * 3,
        out_shape=[out, out, out],
        compiler_params=_cparams("parallel"),
        name="gla_step_pre",
    )(proj, proj, proj, w["gate_up"], w["gate_b"])


def _gla_step_state_kernel(s_ref, q_ref, k_ref, decay_ref, v_ref, g_ref, norm_w_ref, snew_ref, o_ref):
    v = v_ref[0]
    vfull = jnp.concatenate(
        [jnp.broadcast_to(v[:, h * GLA_DV:(h + 1) * GLA_DV], (GLA_DK, GLA_DV)) for h in range(GLA_HEADS)],
        axis=0)
    s_new = s_ref[0] * decay_ref[0] + k_ref[0] * vfull
    snew_ref[0] = s_new
    qs = q_ref[0] * s_new
    o = jnp.concatenate(
        [jnp.sum(qs[h * GLA_DK:(h + 1) * GLA_DK, :], axis=0, keepdims=True) for h in range(GLA_HEADS)],
        axis=1)
    o_ref[0] = _gla_out_norm(o, g_ref[0], norm_w_ref)


def gla_step_state(state, qs, kc, decay, proj, w):
    t = state.shape[0]
    col = pl.BlockSpec((1, GLA_KEY_DIM, 1), lambda i: (i, 0, 0))
    vrow = pl.BlockSpec((1, 1, GLA_VAL_DIM), lambda i: (i, 0, 0))
    return pl.pallas_call(
        _gla_step_state_kernel,
        grid=(t,),
        in_specs=[pl.BlockSpec((1, GLA_KEY_DIM, GLA_DV), lambda i: (i, 0, 0)), col, col, col,
                  pl.BlockSpec((1, 1, GLA_VAL_DIM), lambda i: (i, 0, COL_V // GLA_VAL_DIM)),
                  pl.BlockSpec((1, 1, GLA_VAL_DIM), lambda i: (i, 0, COL_G // GLA_VAL_DIM)),
                  _const_spec((1, GLA_VAL_DIM))],
        out_specs=[pl.BlockSpec((1, GLA_KEY_DIM, GLA_DV), lambda i: (i, 0, 0)), vrow],
        out_shape=[
            jax.ShapeDtypeStruct((t, GLA_KEY_DIM, GLA_DV), F32),
            jax.ShapeDtypeStruct((t, 1, GLA_VAL_DIM), F32),
        ],
        compiler_params=_cparams("parallel"),
        name="gla_step_state",
    )(state, qs, kc, decay, proj, proj, w["gla_norm_w"])


def _merge_kernel(x_ref, ys_ref, og_ref, nw_ref, wm_ref, wso_ref, wgo_ref, wout_ref, o_ref):
    x = x_ref[...]
    xn = (_rms_scale(x) * nw_ref[...]).astype(BF16)
    gates = jax.nn.sigmoid(_dot(xn, wm_ref[...]))
    y_ssd = _dot(ys_ref[...].astype(BF16), wso_ref[...])
    y_gla = _dot(og_ref[...].astype(BF16), wgo_ref[...])
    mix = gates[:, 0:D_MODEL] * y_ssd + gates[:, D_MODEL:] * y_gla
    o_ref[...] = x + _dot(mix.astype(BF16), wout_ref[...])


def merge(x, y_ssd, o_gla, w):
    t = x.shape[0]
    tm = _row_tile(t, 512)
    row = lambda width: pl.BlockSpec((tm, width), lambda i: (i, 0))
    return pl.pallas_call(
        _merge_kernel,
        grid=(t // tm,),
        in_specs=[row(D_MODEL), row(SSD_D_INNER), row(GLA_VAL_DIM), _const_spec((1, D_MODEL)),
                  _const_spec((D_MODEL, 2 * D_MODEL)), _const_spec((SSD_D_INNER, D_MODEL)),
                  _const_spec((GLA_VAL_DIM, D_MODEL)), _const_spec((D_MODEL, D_MODEL))],
        out_specs=row(D_MODEL),
        out_shape=jax.ShapeDtypeStruct((t, D_MODEL), F32),
        compiler_params=_cparams("parallel"),
        name="merge",
    )(x, y_ssd, o_gla, w["norm1_w"], w["w_merge"], w["w_ssd_out"], w["w_gla_out"], w["w_out"])


def _top16_columns(x, with_index):
    n = x.shape[0]
    iota = lax.broadcasted_iota(jnp.int32, x.shape, 0).astype(F32)
    vals, idxs = [], []
    for _ in range(PEER_TOPK):
        m = jnp.max(x, axis=0, keepdims=True)
        hit = x == m
        first = jnp.min(jnp.where(hit, iota, float(n)), axis=0, keepdims=True)
        vals.append(m)
        idxs.append(first)
        x = jnp.where(iota == first, -jnp.inf, x)
    return vals, (idxs if with_index else None)


def _peer_kernel(x_ref, nw_ref, fnw_ref, wqt_ref, keys_ref, u_ref, v_ref, o_ref,
                 hn_ref, rw_ref, s2_ref, sv_ref, si_ref, stat_ref, ids_ref, acc_ref):
    s = pl.program_id(1)
    ns = pl.num_programs(1)
    tb = x_ref.shape[0]
    nk = PEER_N_KEYS

    @pl.when(s == 0)
    def _select():
        hn = _rms_scale(x_ref[...]) * nw_ref[...]
        hn_ref[...] = hn.astype(BF16)
        hn_t = hn.T.astype(BF16)
        stat_ref[...] = jnp.zeros(stat_ref.shape, F32)
        for h in range(PEER_HEADS):
            sv = []
            for c in range(2):
                r0 = (h * 2 + c) * (PEER_DQ // 2)
                q_t = _dot(wqt_ref[r0:r0 + PEER_DQ // 2, :], hn_t)
                sc_t = _dot(keys_ref[h * 2 + c], q_t.astype(BF16))
                vals, idxs = _top16_columns(sc_t, True)
                sv.append(vals)
                for k in range(PEER_TOPK):
                    sv_ref[c, h * PEER_TOPK + k:h * PEER_TOPK + k + 1, :] = vals[k]
                if c == 0:
                    for k in range(PEER_TOPK):
                        si_ref[h * PEER_TOPK + k:h * PEER_TOPK + k + 1, :] = idxs[k]
                else:
                    s2_ref[h] = sc_t.T
            sv2_16 = sv_ref[1, h * PEER_TOPK:(h + 1) * PEER_TOPK, :]
            sv2_8 = sv2_16[0:SUBLANES, :]
            cand = [sv[0][0] + sv2_16]
            for k1 in range(1, SUBLANES):
                cand.append(sv[0][k1] + sv2_8)
            cand.append(sv_ref[0, h * PEER_TOPK + SUBLANES:(h + 1) * PEER_TOPK, :] + sv[1][0])
            cand = jnp.concatenate(cand, axis=0)
            best, _ = _top16_columns(cand, True)
            tau = best[PEER_TOPK - 1]
            top = best[0]
            z = jnp.sum(jnp.where(cand >= tau, jnp.exp(cand - top), 0.0), axis=0, keepdims=True)
            stat_ref[h:h + 1, :] = tau
            stat_ref[PEER_HEADS + h:PEER_HEADS + h + 1, :] = top
            stat_ref[2 * PEER_HEADS + h:2 * PEER_HEADS + h + 1, :] = 1.0 / z

        sv1 = sv_ref[0].T
        stat = stat_ref[...].T
        ids_ref[...] = si_ref[...].T
        for j in range(PEER_HEADS * PEER_TOPK):
            h = j // PEER_TOPK
            u = s2_ref[h] + sv1[:, j:j + 1]
            keep = u >= stat[:, h:h + 1]
            gate = jnp.exp(u - stat[:, PEER_HEADS + h:PEER_HEADS + h + 1])
            row = jnp.where(keep, gate * stat[:, 2 * PEER_HEADS + h:2 * PEER_HEADS + h + 1], 0.0)
            rw_ref[pl.ds(j, tb, stride=PEER_PITCH), :] = row

        key1 = lax.broadcasted_iota(jnp.int32, (nk, nk), 0).astype(F32)

        def scatter(t, carry):
            base = pl.multiple_of(t * PEER_PITCH, SUBLANES)
            rows = rw_ref[pl.ds(base, nk), :].astype(BF16)
            onehot = jnp.where(key1 == ids_ref[pl.ds(t, 1), :], 1.0, 0.0).astype(BF16)
            rw_ref[pl.ds(base, nk), :] = _dot(onehot, rows)
            return carry

        lax.fori_loop(0, tb, scatter, 0)
        acc_ref[...] = jnp.zeros(acc_ref.shape, F32)

    hmat = _dot_nt(hn_ref[...], u_ref[...])
    act = 0.5 * hmat * (1.0 + lax.erf(hmat * (2.0 ** -0.5)))
    wts = jnp.concatenate(
        [rw_ref[pl.ds(s * PEER_SLABS_PER_STEP + i, tb, stride=PEER_PITCH), :]
         for i in range(PEER_SLABS_PER_STEP)], axis=1)
    acc_ref[...] += _dot((act * wts).astype(BF16), v_ref[...])

    @pl.when(s == ns - 1)
    def _():
        o_ref[...] = _rms_scale(x_ref[...] + acc_ref[...]) * fnw_ref[...]


def peer_final(x, w):
    t = x.shape[0]
    tb = _row_tile(t, PEER_TOKEN_BLOCK)
    assert tb % LANES == 0
    ns = PEER_N_EXPERTS // PEER_STEP_EXPERTS
    return pl.pallas_call(
        _peer_kernel,
        grid=(t // tb, ns),
        in_specs=[
            pl.BlockSpec((tb, D_MODEL), lambda i, s: (i, 0)),
            _const_spec((1, D_MODEL)),
            _const_spec((1, D_MODEL)),
            _const_spec((PEER_HEADS * PEER_DQ, D_MODEL)),
            _const_spec((2 * PEER_HEADS, PEER_N_KEYS, PEER_DQ // 2)),
            pl.BlockSpec((PEER_STEP_EXPERTS, D_MODEL), lambda i, s: (s, 0)),
            pl.BlockSpec((PEER_STEP_EXPERTS, D_MODEL), lambda i, s: (s, 0)),
        ],
        out_specs=pl.BlockSpec((tb, D_MODEL), lambda i, s: (i, 0)),
        out_shape=jax.ShapeDtypeStruct((t, D_MODEL), F32),
        scratch_shapes=[
            pltpu.VMEM((tb, D_MODEL), BF16),
            pltpu.VMEM((tb * PEER_PITCH, PEER_N_KEYS), F32),
            pltpu.VMEM((PEER_HEADS, tb, PEER_N_KEYS), F32),
            pltpu.VMEM((2, PEER_HEADS * PEER_TOPK, tb), F32),
            pltpu.VMEM((PEER_HEADS * PEER_TOPK, tb), F32),
            pltpu.VMEM((LANES, tb), F32),
            pltpu.VMEM((tb, PEER_HEADS * PEER_TOPK), F32),
            pltpu.VMEM((tb, D_MODEL), F32),
        ],
        compiler_params=_cparams("parallel", "arbitrary"),
        name="peer_final",
    )(x, w["norm2_w"], w["final_norm_w"], w["peer_wq_t"], w["peer_keys"], w["peer_u"], w["peer_v"])


def _prepare_weights(norm1_w, w_in, conv_w, conv_b, dt_bias, a_log, d_skip, ssd_norm_w, w_ssd_out,
                     gla_gate_up, gla_gate_b, gla_norm_w, w_gla_out, w_out, norm2_w, peer_w_q,
                     peer_sub_keys, peer_u, peer_v, final_norm_w):
    offs = [0]
    for sz in _IN_SIZES:
        offs.append(offs[-1] + sz)
    z0, xbc0, dt0, q0, _, _, _, glr0, mg0, end = offs
    small = jnp.concatenate(
        [w_in[:, dt0:q0], w_in[:, glr0:mg0],
         jnp.zeros((D_MODEL, LANES - SSD_HEADS - GLA_GATE_RANK), w_in.dtype)], axis=1)
    w_proj = jnp.concatenate([w_in[:, z0:dt0], w_in[:, q0:glr0], small], axis=1).astype(BF16)
    assert w_proj.shape[1] == IN_COLS

    def lane_pad(v):
        return jnp.pad(v, (0, LANES - v.shape[0])).reshape(1, LANES)

    head_of_col = jnp.arange(SSD_D_INNER) // SSD_HEAD_DIM
    expand = (jnp.arange(LANES)[:, None] == head_of_col[None, :]).astype(BF16)
    gate_up = jnp.zeros((LANES, GLA_KEY_DIM), F32).at[SMALL_GLR:SMALL_GLR + GLA_GATE_RANK].set(gla_gate_up)
    keys = jnp.transpose(peer_sub_keys, (1, 0, 2, 3)).reshape(2 * PEER_HEADS, PEER_N_KEYS, PEER_DQ // 2)
    return {
        "norm1_w": norm1_w.reshape(1, D_MODEL),
        "w_proj": w_proj,
        "w_merge": w_in[:, mg0:end].astype(BF16),
        "conv_w": conv_w,
        "conv_b": conv_b.reshape(1, SSD_CONV_DIM),
        "dt_bias": lane_pad(dt_bias),
        "a_log": lane_pad(a_log),
        "d_exp": jnp.repeat(d_skip, SSD_HEAD_DIM).reshape(1, SSD_D_INNER),
        "ssd_norm_w": ssd_norm_w.reshape(1, SSD_D_INNER),
        "expand": expand,
        "w_ssd_out": w_ssd_out.astype(BF16),
        "gate_up": gate_up.astype(BF16),
        "gate_b": gla_gate_b.reshape(1, GLA_KEY_DIM),
        "gla_norm_w": jnp.tile(gla_norm_w, GLA_HEADS).reshape(1, GLA_VAL_DIM),
        "w_gla_out": w_gla_out.astype(BF16),
        "w_out": w_out.astype(BF16),
        "norm2_w": norm2_w.reshape(1, D_MODEL),
        "peer_wq_t": peer_w_q.T.astype(BF16),
        "peer_keys": keys.astype(BF16),
        "peer_u": peer_u.astype(BF16),
        "peer_v": peer_v.astype(BF16),
        "final_norm_w": final_norm_w.reshape(1, D_MODEL),
    }


def _prompt_path(x_prompt, meta_tokens, w):
    b, seq, _ = x_prompt.shape
    assert seq % SSD_CHUNK == 0
    meta = jnp.broadcast_to(meta_tokens.astype(x_prompt.dtype)[None], (b, N_META, D_MODEL))
    xp = jnp.concatenate([jnp.zeros((b, PROMPT_ZERO_ROWS, D_MODEL), x_prompt.dtype), meta, x_prompt], axis=1)
    rows = xp.shape[1]
    proj = norm_matmul(xp.reshape(b * rows, D_MODEL), w["norm1_w"], w["w_proj"], IN_COLS // 5)
    y_ssd, st_ssd, conv_tail = ssd_prompt(proj, b, rows // SSD_CHUNK, PROMPT_ZERO_ROWS, PROMPT_SKIP_ROWS, w)
    o_gla, st_gla = gla_prompt(proj, b, rows // GLA_CHUNK, PROMPT_ZERO_ROWS, PROMPT_SKIP_ROWS, w)
    x1 = merge(x_prompt.reshape(b * seq, D_MODEL), y_ssd.reshape(b * seq, SSD_D_INNER),
               o_gla.reshape(b * seq, GLA_VAL_DIM), w)
    y = peer_final(x1, w).reshape(b, seq, D_MODEL)
    return (y,
            st_ssd.reshape(1, b, SSD_HEADS, SSD_HEAD_DIM, SSD_D_STATE),
            conv_tail[:, SUBLANES - (SSD_CONV - 1):, :][None],
            st_gla.reshape(1, b, GLA_HEADS, GLA_DK, GLA_DV))


def _sample_path(x_sample, state_ssd, state_conv, state_gla, w):
    b = x_sample.shape[0]
    x = x_sample.reshape(b, D_MODEL)
    proj = norm_matmul(x, w["norm1_w"], w["w_proj"], IN_COLS // 5)
    xact, xdt, decay, bm, cm, conv_new = ssd_step_pre(
        proj, state_conv.reshape(b, (SSD_CONV - 1) * SSD_CONV_DIM), w)
    gw = SSD_GROUPS * SSD_D_STATE
    st_ssd, y_col = ssd_step_state(
        state_ssd.reshape(b, SSD_D_INNER, SSD_D_STATE), xdt.reshape(b, SSD_D_INNER, 1),
        decay.reshape(b, SSD_D_INNER, 1), bm.reshape(b, 1, gw), cm.reshape(b, 1, gw))
    y_ssd = ssd_step_post(y_col.reshape(b, SSD_D_INNER), xact, proj, w)
    qs, kc, gdecay = gla_step_pre(proj, w)
    st_gla, o_gla = gla_step_state(
        state_gla.reshape(b, GLA_KEY_DIM, GLA_DV), qs.reshape(b, GLA_KEY_DIM, 1),
        kc.reshape(b, GLA_KEY_DIM, 1), gdecay.reshape(b, GLA_KEY_DIM, 1),
        proj.reshape(b, 1, IN_COLS), w)
    x1 = merge(x, y_ssd, o_gla.reshape(b, GLA_VAL_DIM), w)
    y = peer_final(x1, w).reshape(b, 1, D_MODEL)
    return (y,
            st_ssd.reshape(1, b, SSD_HEADS, SSD_HEAD_DIM, SSD_D_STATE),
            conv_new.reshape(1, b, SSD_CONV - 1, SSD_CONV_DIM),
            st_gla.reshape(1, b, GLA_HEADS, GLA_DK, GLA_DV))


def kernel(x_prompt, x_sample, state_ssd, state_conv, state_gla, meta_tokens, norm1_w, w_in, conv_w, conv_b,
           dt_bias, a_log, d_skip, ssd_norm_w, w_ssd_out, gla_gate_up, gla_gate_b, gla_norm_w, w_gla_out,
           w_out, norm2_w, peer_w_q, peer_sub_keys, peer_u, peer_v, final_norm_w):
    layer = (norm1_w, w_in, conv_w, conv_b, dt_bias, a_log, d_skip, ssd_norm_w, w_ssd_out, gla_gate_up,
             gla_gate_b, gla_norm_w, w_gla_out, w_out, norm2_w, peer_w_q, peer_sub_keys, peer_u, peer_v)
    assert all(p.shape[0] == 1 for p in layer), "single-layer step"
    w = _prepare_weights(*[p[0] for p in layer], final_norm_w)
    yp, p_ssd, p_conv, p_gla = _prompt_path(x_prompt, meta_tokens, w)
    ys, s_ssd, s_conv, s_gla = _sample_path(x_sample, state_ssd[0], state_conv[0], state_gla[0], w)
    return (yp, ys, p_ssd, p_conv, p_gla, s_ssd, s_conv, s_gla)
```

```python
import functools

import jax
import jax.numpy as jnp
from jax import lax
from jax.experimental import pallas as pl
from jax.experimental.pallas import tpu as pltpu

F32 = jnp.float32
BF16 = jnp.bfloat16

LANES = 128
SUBLANES = 8
VMEM_LIMIT_BYTES = 56 * 1024 * 1024

D_MODEL = 1024
N_META = 16
EPS = 1e-6
SSD_D_INNER = 2 * D_MODEL
SSD_HEAD_DIM = 64
SSD_HEADS = SSD_D_INNER // SSD_HEAD_DIM
SSD_GROUPS = 4
SSD_HPG = SSD_HEADS // SSD_GROUPS
SSD_D_STATE = 128
SSD_CONV = 4
SSD_CHUNK = 128
SSD_BC = 2 * SSD_GROUPS * SSD_D_STATE
SSD_CONV_DIM = SSD_D_INNER + SSD_BC
SSD_GROUP_W = SSD_HPG * SSD_HEAD_DIM
GLA_HEADS = 4
GLA_KEY_DIM = D_MODEL // 2
GLA_VAL_DIM = D_MODEL
GLA_DK = GLA_KEY_DIM // GLA_HEADS
GLA_DV = GLA_VAL_DIM // GLA_HEADS
GLA_GATE_RANK = 16
GLA_GATE_NORMALIZER = 16.0
GLA_CHUNK = 64
PEER_HEADS = 8
PEER_N_KEYS = 128
PEER_N_EXPERTS = PEER_N_KEYS * PEER_N_KEYS
PEER_DQ = 256
PEER_TOPK = 16
PEER_SLABS_PER_STEP = 8
PEER_STEP_EXPERTS = PEER_SLABS_PER_STEP * PEER_N_KEYS
PEER_SUB_SLABS = 4
PEER_SUB_EXPERTS = PEER_SUB_SLABS * PEER_N_KEYS
PEER_SCATTER_UNROLL = 8
PEER_PITCH = PEER_N_KEYS + 4
PEER_TOKEN_BLOCK = 256

_IN_SIZES = (SSD_D_INNER, SSD_CONV_DIM, SSD_HEADS, GLA_KEY_DIM, GLA_KEY_DIM, GLA_VAL_DIM,
             GLA_VAL_DIM, GLA_GATE_RANK, 2 * D_MODEL)
COL_Z = 0
COL_XS = COL_Z + SSD_D_INNER
COL_BC = COL_XS + SSD_D_INNER
COL_Q = COL_BC + SSD_BC
COL_K = COL_Q + GLA_KEY_DIM
COL_V = COL_K + GLA_KEY_DIM
COL_G = COL_V + GLA_VAL_DIM
COL_SMALL = COL_G + GLA_VAL_DIM
IN_COLS = COL_SMALL + LANES
SMALL_GLR = SSD_HEADS
PROMPT_ZERO_ROWS = SSD_CHUNK - N_META
PROMPT_SKIP_ROWS = PROMPT_ZERO_ROWS + N_META


def _cparams(*sem):
    return pltpu.CompilerParams(dimension_semantics=sem, vmem_limit_bytes=VMEM_LIMIT_BYTES)


def _const_spec(shape):
    nd = len(shape)
    return pl.BlockSpec(shape, lambda *_: (0,) * nd, pipeline_mode=pl.Buffered(1))


def _row_tile(t, cap):
    tm = cap
    while t % tm:
        tm //= 2
    return tm


def _split3(x):
    hi = x.astype(BF16)
    r1 = x - hi.astype(F32)
    mid = r1.astype(BF16)
    lo = (r1 - mid.astype(F32)).astype(BF16)
    return hi, mid, lo


def _dot(a, b):
    return jnp.dot(a, b, preferred_element_type=F32)


def _dot_nt(a, b):
    return lax.dot_general(a, b, (((1,), (1,)), ((), ())), preferred_element_type=F32)


def _sel_right(sel01, x):
    hi, mid, lo = _split3(x)
    return _dot(hi, sel01) + _dot(mid, sel01) + _dot(lo, sel01)


def _sel_left(sel01, x):
    hi, mid, lo = _split3(x)
    return _dot(sel01, hi) + _dot(sel01, mid) + _dot(sel01, lo)


def _rms_scale(x):
    return x * lax.rsqrt(jnp.mean(x * x, axis=-1, keepdims=True) + EPS)


def _tril(n):
    r = lax.broadcasted_iota(jnp.int32, (n, n), 0)
    c = lax.broadcasted_iota(jnp.int32, (n, n), 1)
    return r >= c


def _norm_matmul_kernel(x_ref, nw_ref, w_ref, o_ref, xn_ref):
    @pl.when(pl.program_id(1) == 0)
    def _():
        xn_ref[...] = (_rms_scale(x_ref[...]) * nw_ref[...]).astype(BF16)

    o_ref[...] = _dot(xn_ref[...], w_ref[...])


def norm_matmul(x, norm_w, w, tn):
    t, d = x.shape
    n = w.shape[1]
    tm = _row_tile(t, 1024)
    assert n % tn == 0
    return pl.pallas_call(
        _norm_matmul_kernel,
        grid=(t // tm, n // tn),
        in_specs=[
            pl.BlockSpec((tm, d), lambda i, j: (i, 0)),
            pl.BlockSpec((1, d), lambda i, j: (0, 0)),
            pl.BlockSpec((d, tn), lambda i, j: (0, j)),
        ],
        out_specs=pl.BlockSpec((tm, tn), lambda i, j: (i, j)),
        out_shape=jax.ShapeDtypeStruct((t, n), F32),
        scratch_shapes=[pltpu.VMEM((tm, d), BF16)],
        compiler_params=_cparams("parallel", "arbitrary"),
        name="norm_in_proj",
    )(x, norm_w, w)


def _ssd_dt(small, dt_bias_ref, valid):
    lane = lax.broadcasted_iota(jnp.int32, small.shape, 1)
    dt = jax.nn.softplus(small + dt_bias_ref[...])
    keep = lane < SSD_HEADS
    if valid is not None:
        keep = jnp.logical_and(keep, valid)
    return jnp.where(keep, dt, 0.0)


def _ssd_gated_norm(y, z, norm_w_ref):
    y = y * jax.nn.silu(z)
    parts = []
    for g in range(SSD_GROUPS):
        parts.append(_rms_scale(y[:, g * SSD_GROUP_W:(g + 1) * SSD_GROUP_W]))
    return jnp.concatenate(parts, axis=1) * norm_w_ref[...]


def _ssd_prompt_kernel(z_ref, xs_ref, bc_ref, small_ref, conv_w_ref, conv_b_ref, dt_bias_ref,
                       a_log_ref, d_exp_ref, norm_w_ref, expand_ref,
                       y_ref, state_ref, convtail_ref,
                       hist_ref, st_ref, yacc_ref, *, zero_rows):
    c = pl.program_id(1)
    nc = pl.num_programs(1)
    q = SSD_CHUNK

    @pl.when(c == 0)
    def _():
        hist_ref[0:SUBLANES, :] = jnp.zeros((SUBLANES, SSD_CONV_DIM), F32)
        st_ref[...] = jnp.zeros(st_ref.shape, F32)

    hist_ref[SUBLANES:SUBLANES + q, 0:SSD_D_INNER] = xs_ref[...]
    hist_ref[SUBLANES:SUBLANES + q, SSD_D_INNER:SSD_CONV_DIM] = bc_ref[...]
    conv = conv_b_ref[...]
    for j in range(SSD_CONV):
        start = SUBLANES - (SSD_CONV - 1) + j
        conv = conv + conv_w_ref[j:j + 1, :] * hist_ref[start:start + q, :]
    xbc = jax.nn.silu(conv)
    tail = hist_ref[q:q + SUBLANES, :]
    hist_ref[0:SUBLANES, :] = tail
    convtail_ref[0] = tail

    xs = xbc[:, 0:SSD_D_INNER]
    bm = xbc[:, SSD_D_INNER:SSD_D_INNER + SSD_GROUPS * SSD_D_STATE].astype(BF16)
    cm = xbc[:, SSD_D_INNER + SSD_GROUPS * SSD_D_STATE:].astype(BF16)

    row = lax.broadcasted_iota(jnp.int32, (q, LANES), 0) + c * q
    dt = _ssd_dt(small_ref[...], dt_bias_ref, row >= zero_rows)
    da = dt * (-jnp.exp(a_log_ref[...]))
    tril = _tril(q)
    acum = _sel_left(tril.astype(BF16), da)
    acum_t = acum.T
    expand = expand_ref[...]
    dt_exp = _sel_right(expand, dt)
    acum_exp = _sel_right(expand, acum)
    acum_last = acum_exp[q - 1:q, :]
    xdt = xs * dt_exp
    xdt_bf = xdt.astype(BF16)
    x_end = (xdt * jnp.exp(acum_last - acum_exp)).astype(BF16)

    lane_lo = lax.broadcasted_iota(jnp.int32, (q, LANES), 1) < SSD_HEAD_DIM
    for g in range(SSD_GROUPS):
        bg = bm[:, g * SSD_D_STATE:(g + 1) * SSD_D_STATE]
        cg = cm[:, g * SSD_D_STATE:(g + 1) * SSD_D_STATE]
        cb = _dot_nt(cg, bg)
        for hp in range(SSD_HPG // 2):
            lo = (g * SSD_HPG + 2 * hp) * SSD_HEAD_DIM
            pair = []
            for k in range(2):
                h = g * SSD_HPG + 2 * hp + k
                diff = acum[:, h:h + 1] - acum_t[h:h + 1, :]
                seg = jnp.exp(jnp.where(tril, diff, -jnp.inf))
                pair.append(_dot((cb * seg).astype(BF16), xdt_bf[:, lo:lo + LANES]))
            yacc_ref[:, lo:lo + LANES] = jnp.where(lane_lo, pair[0], pair[1])

        gs = slice(g * SSD_GROUP_W, (g + 1) * SSD_GROUP_W)
        st_g = st_ref[:, gs]
        y_off = _dot(cg, st_g.astype(BF16)) * jnp.exp(acum_exp[:, gs])
        yacc_ref[:, gs] = yacc_ref[:, gs] + y_off
        st_ref[:, gs] = st_g * jnp.exp(acum_last[:, gs]) + _dot(bg.T, x_end[:, gs])

    y = yacc_ref[...] + d_exp_ref[...] * xs
    y_ref[0] = _ssd_gated_norm(y, z_ref[...], norm_w_ref).astype(BF16)

    @pl.when(c == nc - 1)
    def _():
        state_ref[0] = st_ref[...].T


def ssd_prompt(proj, batch, n_chunks, zero_rows, skip_rows, w):
    q = SSD_CHUNK
    assert skip_rows % q == 0
    first_real = skip_rows // q
    n_out = n_chunks - first_real

    def rows(b, c):
        return b * n_chunks + c

    return pl.pallas_call(
        functools.partial(_ssd_prompt_kernel, zero_rows=zero_rows),
        grid=(batch, n_chunks),
        in_specs=[
            pl.BlockSpec((q, SSD_D_INNER), lambda b, c: (rows(b, c), COL_Z // SSD_D_INNER)),
            pl.BlockSpec((q, SSD_D_INNER), lambda b, c: (rows(b, c), COL_XS // SSD_D_INNER)),
            pl.BlockSpec((q, SSD_BC), lambda b, c: (rows(b, c), COL_BC // SSD_BC)),
            pl.BlockSpec((q, LANES), lambda b, c: (rows(b, c), COL_SMALL // LANES)),
            _const_spec((SSD_CONV, SSD_CONV_DIM)),
            _const_spec((1, SSD_CONV_DIM)),
            _const_spec((1, LANES)),
            _const_spec((1, LANES)),
            _const_spec((1, SSD_D_INNER)),
            _const_spec((1, SSD_D_INNER)),
            _const_spec((LANES, SSD_D_INNER)),
        ],
        out_specs=[
            pl.BlockSpec((1, q, SSD_D_INNER), lambda b, c: (b, jnp.maximum(c - first_real, 0), 0)),
            pl.BlockSpec((1, SSD_D_INNER, SSD_D_STATE), lambda b, c: (b, 0, 0)),
            pl.BlockSpec((1, SUBLANES, SSD_CONV_DIM), lambda b, c: (b, 0, 0)),
        ],
        out_shape=[
            jax.ShapeDtypeStruct((batch, n_out * q, SSD_D_INNER), BF16),
            jax.ShapeDtypeStruct((batch, SSD_D_INNER, SSD_D_STATE), F32),
            jax.ShapeDtypeStruct((batch, SUBLANES, SSD_CONV_DIM), F32),
        ],
        scratch_shapes=[
            pltpu.VMEM((SUBLANES + q, SSD_CONV_DIM), F32),
            pltpu.VMEM((SSD_D_STATE, SSD_D_INNER), F32),
            pltpu.VMEM((q, SSD_D_INNER), F32),
        ],
        compiler_params=_cparams("parallel", "arbitrary"),
        name="ssd_prompt",
    )(proj, proj, proj, proj, w["conv_w"], w["conv_b"], w["dt_bias"], w["a_log"], w["d_exp"],
      w["ssd_norm_w"], w["expand"])


def _ssd_step_pre_kernel(xs_ref, bc_ref, small_ref, c0_ref, c1_ref, c2_ref, conv_w_ref, conv_b_ref,
                         dt_bias_ref, a_log_ref, expand_ref,
                         xact_ref, xdt_ref, decay_ref, bm_ref, cm_ref, convnew_ref):
    x = jnp.concatenate([xs_ref[...], bc_ref[...]], axis=1)
    hist = (c0_ref[...], c1_ref[...], c2_ref[...], x)
    conv = conv_b_ref[...]
    for j in range(SSD_CONV):
        conv = conv + conv_w_ref[j:j + 1, :] * hist[j]
    xbc = jax.nn.silu(conv)
    xs = xbc[:, 0:SSD_D_INNER]
    dt = _ssd_dt(small_ref[...], dt_bias_ref, None)
    da = dt * (-jnp.exp(a_log_ref[...]))
    expand = expand_ref[...]
    xact_ref[...] = xs
    xdt_ref[...] = xs * _sel_right(expand, dt)
    decay_ref[...] = jnp.exp(_sel_right(expand, da))
    bm_ref[...] = xbc[:, SSD_D_INNER:SSD_D_INNER + SSD_GROUPS * SSD_D_STATE]
    cm_ref[...] = xbc[:, SSD_D_INNER + SSD_GROUPS * SSD_D_STATE:]
    convnew_ref[:, 0:SSD_CONV_DIM] = hist[1]
    convnew_ref[:, SSD_CONV_DIM:2 * SSD_CONV_DIM] = hist[2]
    convnew_ref[:, 2 * SSD_CONV_DIM:] = x


def ssd_step_pre(proj, conv_state, w):
    t = proj.shape[0]
    tm = _row_tile(t, 128)
    gw = SSD_GROUPS * SSD_D_STATE
    row = lambda width, col: pl.BlockSpec((tm, width), lambda i: (i, col))
    return pl.pallas_call(
        _ssd_step_pre_kernel,
        grid=(t // tm,),
        in_specs=[
            row(SSD_D_INNER, COL_XS // SSD_D_INNER),
            row(SSD_BC, COL_BC // SSD_BC),
            row(LANES, COL_SMALL // LANES),
            row(SSD_CONV_DIM, 0), row(SSD_CONV_DIM, 1), row(SSD_CONV_DIM, 2),
            _const_spec((SSD_CONV, SSD_CONV_DIM)),
            _const_spec((1, SSD_CONV_DIM)),
            _const_spec((1, LANES)),
            _const_spec((1, LANES)),
            _const_spec((LANES, SSD_D_INNER)),
        ],
        out_specs=[row(SSD_D_INNER, 0), row(SSD_D_INNER, 0), row(SSD_D_INNER, 0), row(gw, 0),
                   row(gw, 0), row((SSD_CONV - 1) * SSD_CONV_DIM, 0)],
        out_shape=[
            jax.ShapeDtypeStruct((t, SSD_D_INNER), F32),
            jax.ShapeDtypeStruct((t, SSD_D_INNER), F32),
            jax.ShapeDtypeStruct((t, SSD_D_INNER), F32),
            jax.ShapeDtypeStruct((t, gw), F32),
            jax.ShapeDtypeStruct((t, gw), F32),
            jax.ShapeDtypeStruct((t, (SSD_CONV - 1) * SSD_CONV_DIM), F32),
        ],
        compiler_params=_cparams("parallel"),
        name="ssd_step_pre",
    )(proj, proj, proj, conv_state, conv_state, conv_state, w["conv_w"], w["conv_b"], w["dt_bias"],
      w["a_log"], w["expand"])


def _ssd_step_state_kernel(s_ref, xdt_ref, decay_ref, bm_ref, cm_ref, snew_ref, y_ref):
    def per_group(row_ref):
        parts = [jnp.broadcast_to(row_ref[0][:, g * SSD_D_STATE:(g + 1) * SSD_D_STATE],
                                  (SSD_GROUP_W, SSD_D_STATE)) for g in range(SSD_GROUPS)]
        return jnp.concatenate(parts, axis=0)

    s_new = s_ref[0] * decay_ref[0] + xdt_ref[0] * per_group(bm_ref)
    snew_ref[0] = s_new
    y_ref[0] = jnp.sum(s_new * per_group(cm_ref), axis=1, keepdims=True)


def ssd_step_state(state, xdt, decay, bm, cm):
    t = state.shape[0]
    gw = SSD_GROUPS * SSD_D_STATE
    col = pl.BlockSpec((1, SSD_D_INNER, 1), lambda i: (i, 0, 0))
    return pl.pallas_call(
        _ssd_step_state_kernel,
        grid=(t,),
        in_specs=[
            pl.BlockSpec((1, SSD_D_INNER, SSD_D_STATE), lambda i: (i, 0, 0)),
            col, col,
            pl.BlockSpec((1, 1, gw), lambda i: (i, 0, 0)),
            pl.BlockSpec((1, 1, gw), lambda i: (i, 0, 0)),
        ],
        out_specs=[pl.BlockSpec((1, SSD_D_INNER, SSD_D_STATE), lambda i: (i, 0, 0)), col],
        out_shape=[
            jax.ShapeDtypeStruct((t, SSD_D_INNER, SSD_D_STATE), F32),
            jax.ShapeDtypeStruct((t, SSD_D_INNER, 1), F32),
        ],
        compiler_params=_cparams("parallel"),
        name="ssd_step_state",
    )(state, xdt, decay, bm, cm)


def _ssd_step_post_kernel(y_ref, xact_ref, z_ref, d_exp_ref, norm_w_ref, o_ref):
    y = y_ref[...] + d_exp_ref[...] * xact_ref[...]
    o_ref[...] = _ssd_gated_norm(y, z_ref[...], norm_w_ref).astype(BF16)


def ssd_step_post(y, xact, proj, w):
    t = y.shape[0]
    tm = _row_tile(t, 128)
    row = lambda col: pl.BlockSpec((tm, SSD_D_INNER), lambda i: (i, col))
    return pl.pallas_call(
        _ssd_step_post_kernel,
        grid=(t // tm,),
        in_specs=[row(0), row(0), row(COL_Z // SSD_D_INNER), _const_spec((1, SSD_D_INNER)),
                  _const_spec((1, SSD_D_INNER))],
        out_specs=row(0),
        out_shape=jax.ShapeDtypeStruct((t, SSD_D_INNER), BF16),
        compiler_params=_cparams("parallel"),
        name="ssd_step_post",
    )(y, xact, proj, w["d_exp"], w["ssd_norm_w"])


def _gla_gate(small, gate_up_ref, gate_b_ref):
    pre = _dot(small.astype(BF16), gate_up_ref[...]) + gate_b_ref[...]
    return jax.nn.log_sigmoid(pre) / GLA_GATE_NORMALIZER


def _gla_out_norm(o, g, norm_w_ref):
    parts = []
    for h in range(GLA_HEADS):
        parts.append(_rms_scale(o[:, h * GLA_DV:(h + 1) * GLA_DV]))
    return jnp.concatenate(parts, axis=1) * norm_w_ref[...] * jax.nn.silu(g)


def _gla_prompt_kernel(q_ref, k_ref, v_ref, g_ref, small_ref, gate_up_ref, gate_b_ref, norm_w_ref,
                       o_ref, state_ref, s_ref, *, zero_rows):
    c = pl.program_id(1)
    nc = pl.num_programs(1)
    q = GLA_CHUNK

    @pl.when(c == 0)
    def _():
        s_ref[...] = jnp.zeros(s_ref.shape, F32)

    valid = (lax.broadcasted_iota(jnp.int32, (q, GLA_KEY_DIM), 0) + c * q) >= zero_rows
    gk = jnp.where(valid, _gla_gate(small_ref[...], gate_up_ref, gate_b_ref), 0.0)
    kk = jnp.where(valid, k_ref[...], 0.0)
    tril = _tril(q)
    bcum = _sel_left(tril.astype(BF16), gk)
    bcum_last = bcum[q - 1:q, :]
    qe = q_ref[...] * jnp.exp(bcum) * (GLA_DK ** -0.5)
    ke = kk * jnp.exp(-bcum)
    kend = kk * jnp.exp(bcum_last - bcum)
    decay = jnp.exp(bcum_last)
    vv = v_ref[...].astype(BF16)
    outs = []
    for h in range(GLA_HEADS):
        ks = slice(h * GLA_DK, (h + 1) * GLA_DK)
        vs = slice(h * GLA_DV, (h + 1) * GLA_DV)
        qe_h = qe[:, ks].astype(BF16)
        att = jnp.where(tril, _dot_nt(qe_h, ke[:, ks].astype(BF16)), 0.0)
        s_h = s_ref[ks, :]
        outs.append(_dot(att.astype(BF16), vv[:, vs]) + _dot(qe_h, s_h.astype(BF16)))
        decay_col = jnp.broadcast_to(decay[:, ks], (GLA_DK, GLA_DK)).T
        kend_pad = jnp.concatenate([kend[:, ks], jnp.zeros((GLA_DK - q, GLA_DK), F32)], axis=0)
        kend_t = kend_pad.T[:, 0:q].astype(BF16)
        s_ref[ks, :] = (jnp.concatenate([decay_col] * (GLA_DV // GLA_DK), axis=1) * s_h
                        + _dot(kend_t, vv[:, vs]))
    o = jnp.concatenate(outs, axis=1)
    o_ref[0] = _gla_out_norm(o, g_ref[...], norm_w_ref).astype(BF16)

    @pl.when(c == nc - 1)
    def _():
        state_ref[0] = s_ref[...]


def gla_prompt(proj, batch, n_chunks, zero_rows, skip_rows, w):
    q = GLA_CHUNK
    assert skip_rows % q == 0
    first_real = skip_rows // q
    n_out = n_chunks - first_real

    def rows(b, c):
        return b * n_chunks + c

    return pl.pallas_call(
        functools.partial(_gla_prompt_kernel, zero_rows=zero_rows),
        grid=(batch, n_chunks),
        in_specs=[
            pl.BlockSpec((q, GLA_KEY_DIM), lambda b, c: (rows(b, c), COL_Q // GLA_KEY_DIM)),
            pl.BlockSpec((q, GLA_KEY_DIM), lambda b, c: (rows(b, c), COL_K // GLA_KEY_DIM)),
            pl.BlockSpec((q, GLA_VAL_DIM), lambda b, c: (rows(b, c), COL_V // GLA_VAL_DIM)),
            pl.BlockSpec((q, GLA_VAL_DIM), lambda b, c: (rows(b, c), COL_G // GLA_VAL_DIM)),
            pl.BlockSpec((q, LANES), lambda b, c: (rows(b, c), COL_SMALL // LANES)),
            _const_spec((LANES, GLA_KEY_DIM)),
            _const_spec((1, GLA_KEY_DIM)),
            _const_spec((1, GLA_VAL_DIM)),
        ],
        out_specs=[
            pl.BlockSpec((1, q, GLA_VAL_DIM), lambda b, c: (b, jnp.maximum(c - first_real, 0), 0)),
            pl.BlockSpec((1, GLA_KEY_DIM, GLA_DV), lambda b, c: (b, 0, 0)),
        ],
        out_shape=[
            jax.ShapeDtypeStruct((batch, n_out * q, GLA_VAL_DIM), BF16),
            jax.ShapeDtypeStruct((batch, GLA_KEY_DIM, GLA_DV), F32),
        ],
        scratch_shapes=[pltpu.VMEM((GLA_KEY_DIM, GLA_DV), F32)],
        compiler_params=_cparams("parallel", "arbitrary"),
        name="gla_prompt",
    )(proj, proj, proj, proj, proj, w["gate_up"], w["gate_b"], w["gla_norm_w"])


def _gla_step_pre_kernel(q_ref, k_ref, small_ref, gate_up_ref, gate_b_ref, qs_ref, kc_ref, decay_ref):
    qs_ref[...] = q_ref[...] * (GLA_DK ** -0.5)
    kc_ref[...] = k_ref[...]
    decay_ref[...] = jnp.exp(_gla_gate(small_ref[...], gate_up_ref, gate_b_ref))


def gla_step_pre(proj, w):
    t = proj.shape[0]
    tm = _row_tile(t, 128)
    row = lambda width, col: pl.BlockSpec((tm, width), lambda i: (i, col))
    out = jax.ShapeDtypeStruct((t, GLA_KEY_DIM), F32)
    return pl.pallas_call(
        _gla_step_pre_kernel,
        grid=(t // tm,),
        in_specs=[row(GLA_KEY_DIM, COL_Q // GLA_KEY_DIM), row(GLA_KEY_DIM, COL_K // GLA_KEY_DIM),
                  row(LANES, COL_SMALL // LANES), _const_spec((LANES, GLA_KEY_DIM)),
                  _const_spec((1, GLA_KEY_DIM))],
        out_specs=[row(GLA_KEY_DIM, 0)] * 3,
        out_shape=[out, out, out],
        compiler_params=_cparams("parallel"),
        name="gla_step_pre",
    )(proj, proj, proj, w["gate_up"], w["gate_b"])


def _gla_step_state_kernel(s_ref, q_ref, k_ref, decay_ref, v_ref, g_ref, norm_w_ref, snew_ref, o_ref):
    v = v_ref[0]
    vfull = jnp.concatenate(
        [jnp.broadcast_to(v[:, h * GLA_DV:(h + 1) * GLA_DV], (GLA_DK, GLA_DV)) for h in range(GLA_HEADS)],
        axis=0)
    s_new = s_ref[0] * decay_ref[0] + k_ref[0] * vfull
    snew_ref[0] = s_new
    qs = q_ref[0] * s_new
    o = jnp.concatenate(
        [jnp.sum(qs[h * GLA_DK:(h + 1) * GLA_DK, :], axis=0, keepdims=True) for h in range(GLA_HEADS)],
        axis=1)
    o_ref[0] = _gla_out_norm(o, g_ref[0], norm_w_ref)


def gla_step_state(state, qs, kc, decay, proj, w):
    t = state.shape[0]
    col = pl.BlockSpec((1, GLA_KEY_DIM, 1), lambda i: (i, 0, 0))
    vrow = pl.BlockSpec((1, 1, GLA_VAL_DIM), lambda i: (i, 0, 0))
    return pl.pallas_call(
        _gla_step_state_kernel,
        grid=(t,),
        in_specs=[pl.BlockSpec((1, GLA_KEY_DIM, GLA_DV), lambda i: (i, 0, 0)), col, col, col,
                  pl.BlockSpec((1, 1, GLA_VAL_DIM), lambda i: (i, 0, COL_V // GLA_VAL_DIM)),
                  pl.BlockSpec((1, 1, GLA_VAL_DIM), lambda i: (i, 0, COL_G // GLA_VAL_DIM)),
                  _const_spec((1, GLA_VAL_DIM))],
        out_specs=[pl.BlockSpec((1, GLA_KEY_DIM, GLA_DV), lambda i: (i, 0, 0)), vrow],
        out_shape=[
            jax.ShapeDtypeStruct((t, GLA_KEY_DIM, GLA_DV), F32),
            jax.ShapeDtypeStruct((t, 1, GLA_VAL_DIM), F32),
        ],
        compiler_params=_cparams("parallel"),
        name="gla_step_state",
    )(state, qs, kc, decay, proj, proj, w["gla_norm_w"])


def _merge_kernel(x_ref, ys_ref, og_ref, nw_ref, wm_ref, wso_ref, wgo_ref, wout_ref, o_ref):
    x = x_ref[...]
    xn = (_rms_scale(x) * nw_ref[...]).astype(BF16)
    gates = jax.nn.sigmoid(_dot(xn, wm_ref[...]))
    y_ssd = _dot(ys_ref[...].astype(BF16), wso_ref[...])
    y_gla = _dot(og_ref[...].astype(BF16), wgo_ref[...])
    mix = gates[:, 0:D_MODEL] * y_ssd + gates[:, D_MODEL:] * y_gla
    o_ref[...] = x + _dot(mix.astype(BF16), wout_ref[...])


def merge(x, y_ssd, o_gla, w):
    t = x.shape[0]
    tm = _row_tile(t, 512)
    row = lambda width: pl.BlockSpec((tm, width), lambda i: (i, 0))
    return pl.pallas_call(
        _merge_kernel,
        grid=(t // tm,),
        in_specs=[row(D_MODEL), row(SSD_D_INNER), row(GLA_VAL_DIM), _const_spec((1, D_MODEL)),
                  _const_spec((D_MODEL, 2 * D_MODEL)), _const_spec((SSD_D_INNER, D_MODEL)),
                  _const_spec((GLA_VAL_DIM, D_MODEL)), _const_spec((D_MODEL, D_MODEL))],
        out_specs=row(D_MODEL),
        out_shape=jax.ShapeDtypeStruct((t, D_MODEL), F32),
        compiler_params=_cparams("parallel"),
        name="merge",
    )(x, y_ssd, o_gla, w["norm1_w"], w["w_merge"], w["w_ssd_out"], w["w_gla_out"], w["w_out"])


def _top16_columns(x, with_index):
    n = x.shape[0]
    iota = lax.broadcasted_iota(jnp.int32, x.shape, 0).astype(F32)
    vals, idxs = [], []
    for _ in range(PEER_TOPK):
        m = jnp.max(x, axis=0, keepdims=True)
        hit = x == m
        first = jnp.min(jnp.where(hit, iota, float(n)), axis=0, keepdims=True)
        vals.append(m)
        idxs.append(first)
        x = jnp.where(iota == first, -jnp.inf, x)
    return vals, (idxs if with_index else None)


def _peer_kernel(x_ref, nw_ref, fnw_ref, wqt_ref, keys_ref, u_ref, v_ref, o_ref,
                 hn_ref, rw_ref, s2_ref, sv_ref, si_ref, stat_ref, g1_ref, ids_ref, g1tm_ref, acc_ref):
    s = pl.program_id(1)
    ns = pl.num_programs(1)
    tb = x_ref.shape[0]
    nk = PEER_N_KEYS

    @pl.when(s == 0)
    def _select():
        acc_ref[...] = jnp.zeros(acc_ref.shape, F32)
        hn = _rms_scale(x_ref[...]) * nw_ref[...]
        hn_ref[...] = hn.astype(BF16)
        hn_t = hn.T.astype(BF16)
        stat_ref[...] = jnp.zeros(stat_ref.shape, F32)
        for h in range(PEER_HEADS):
            sv = []
            for c in range(2):
                r0 = (h * 2 + c) * (PEER_DQ // 2)
                q_t = _dot(wqt_ref[r0:r0 + PEER_DQ // 2, :], hn_t)
                sc_t = _dot(keys_ref[h * 2 + c], q_t.astype(BF16))
                vals, idxs = _top16_columns(sc_t, True)
                sv.append(vals)
                for k in range(PEER_TOPK):
                    sv_ref[c, h * PEER_TOPK + k:h * PEER_TOPK + k + 1, :] = vals[k]
                if c == 0:
                    for k in range(PEER_TOPK):
                        si_ref[h * PEER_TOPK + k:h * PEER_TOPK + k + 1, :] = idxs[k]
                else:
                    s2_ref[h] = sc_t.T
            sv2_16 = sv_ref[1, h * PEER_TOPK:(h + 1) * PEER_TOPK, :]
            sv2_8 = sv2_16[0:SUBLANES, :]
            cand = [sv[0][0] + sv2_16]
            for k1 in range(1, SUBLANES):
                cand.append(sv[0][k1] + sv2_8)
            cand.append(sv_ref[0, h * PEER_TOPK + SUBLANES:(h + 1) * PEER_TOPK, :] + sv[1][0])
            cand = jnp.concatenate(cand, axis=0)
            best, _ = _top16_columns(cand, True)
            tau = best[PEER_TOPK - 1]
            top = best[0]
            z = jnp.sum(jnp.where(cand >= tau, jnp.exp(cand - top), 0.0), axis=0, keepdims=True)
            stat_ref[h:h + 1, :] = tau
            stat_ref[PEER_HEADS + h:PEER_HEADS + h + 1, :] = sv[1][0]
            sv1_16 = sv_ref[0, h * PEER_TOPK:(h + 1) * PEER_TOPK, :]
            g1_ref[h * PEER_TOPK:(h + 1) * PEER_TOPK, :] = jnp.exp(sv1_16 - sv[0][0]) / z

        sv1 = sv_ref[0].T
        stat = stat_ref[...].T
        ids_ref[...] = si_ref[...].T
        g1tm_ref[...] = g1_ref[...].T
        for h in range(PEER_HEADS):
            s2 = s2_ref[h]
            tau_b = jnp.broadcast_to(stat[:, h:h + 1], s2.shape)
            e2 = jnp.exp(s2 - stat[:, PEER_HEADS + h:PEER_HEADS + h + 1])
            for k in range(PEER_TOPK):
                j = h * PEER_TOPK + k
                keep = (s2 + sv1[:, j:j + 1]) >= tau_b
                rw_ref[pl.ds(j, tb, stride=PEER_PITCH), :] = jnp.where(keep, e2, 0.0)

        key1 = lax.broadcasted_iota(jnp.int32, (nk, nk), 0).astype(F32)

        def scatter(t, carry):
            base = t * PEER_PITCH
            rows = rw_ref[pl.ds(base, nk), :].astype(BF16)
            place = jnp.where(key1 == ids_ref[pl.ds(t, 1), :], g1tm_ref[pl.ds(t, 1), :], 0.0)
            rw_ref[pl.ds(base, nk), :] = _dot(place.astype(BF16), rows)
            return carry

        lax.fori_loop(0, tb, scatter, 0, unroll=PEER_SCATTER_UNROLL)

    hn = hn_ref[...]

    def hidden(k):
        return _dot_nt(hn, u_ref[k * PEER_SUB_EXPERTS:(k + 1) * PEER_SUB_EXPERTS, :])

    def weighted(k, hmat):
        act = 0.5 * hmat * (1.0 + lax.erf(hmat * (2.0 ** -0.5)))
        slab0 = s * PEER_SLABS_PER_STEP + k * PEER_SUB_SLABS
        wts = jnp.concatenate(
            [rw_ref[pl.ds(slab0 + i, tb, stride=PEER_PITCH), :] for i in range(PEER_SUB_SLABS)], axis=1)
        return (act * wts).astype(BF16)

    n_sub = PEER_STEP_EXPERTS // PEER_SUB_EXPERTS
    hmat = hidden(0)
    total = None
    for k in range(n_sub):
        nxt = hidden(k + 1) if k + 1 < n_sub else None
        part = _dot(weighted(k, hmat), v_ref[k * PEER_SUB_EXPERTS:(k + 1) * PEER_SUB_EXPERTS, :])
        total = part if total is None else total + part
        hmat = nxt
    acc_ref[...] += total

    @pl.when(s == ns - 1)
    def _():
        o_ref[...] = _rms_scale(x_ref[...] + acc_ref[...]) * fnw_ref[...]


def peer_final(x, w):
    t = x.shape[0]
    tb = _row_tile(t, PEER_TOKEN_BLOCK)
    assert tb % LANES == 0
    ns = PEER_N_EXPERTS // PEER_STEP_EXPERTS
    return pl.pallas_call(
        _peer_kernel,
        grid=(t // tb, ns),
        in_specs=[
            pl.BlockSpec((tb, D_MODEL), lambda i, s: (i, 0)),
            _const_spec((1, D_MODEL)),
            _const_spec((1, D_MODEL)),
            _const_spec((PEER_HEADS * PEER_DQ, D_MODEL)),
            _const_spec((2 * PEER_HEADS, PEER_N_KEYS, PEER_DQ // 2)),
            pl.BlockSpec((PEER_STEP_EXPERTS, D_MODEL), lambda i, s: (s, 0)),
            pl.BlockSpec((PEER_STEP_EXPERTS, D_MODEL), lambda i, s: (s, 0)),
        ],
        out_specs=pl.BlockSpec((tb, D_MODEL), lambda i, s: (i, 0)),
        out_shape=jax.ShapeDtypeStruct((t, D_MODEL), F32),
        scratch_shapes=[
            pltpu.VMEM((tb, D_MODEL), BF16),
            pltpu.VMEM((tb * PEER_PITCH, PEER_N_KEYS), F32),
            pltpu.VMEM((PEER_HEADS, tb, PEER_N_KEYS), F32),
            pltpu.VMEM((2, PEER_HEADS * PEER_TOPK, tb), F32),
            pltpu.VMEM((PEER_HEADS * PEER_TOPK, tb), F32),
            pltpu.VMEM((LANES, tb), F32),
            pltpu.VMEM((PEER_HEADS * PEER_TOPK, tb), F32),
            pltpu.VMEM((tb, PEER_HEADS * PEER_TOPK), F32),
            pltpu.VMEM((tb, PEER_HEADS * PEER_TOPK), F32),
            pltpu.VMEM((tb, D_MODEL), F32),
        ],
        compiler_params=_cparams("parallel", "arbitrary"),
        name="peer_final",
    )(x, w["norm2_w"], w["final_norm_w"], w["peer_wq_t"], w["peer_keys"], w["peer_u"], w["peer_v"])


def _prepare_weights(norm1_w, w_in, conv_w, conv_b, dt_bias, a_log, d_skip, ssd_norm_w, w_ssd_out,
                     gla_gate_up, gla_gate_b, gla_norm_w, w_gla_out, w_out, norm2_w, peer_w_q,
                     peer_sub_keys, peer_u, peer_v, final_norm_w):
    offs = [0]
    for sz in _IN_SIZES:
        offs.append(offs[-1] + sz)
    z0, xbc0, dt0, q0, _, _, _, glr0, mg0, end = offs
    small = jnp.concatenate(
        [w_in[:, dt0:q0], w_in[:, glr0:mg0],
         jnp.zeros((D_MODEL, LANES - SSD_HEADS - GLA_GATE_RANK), w_in.dtype)], axis=1)
    w_proj = jnp.concatenate([w_in[:, z0:dt0], w_in[:, q0:glr0], small], axis=1).astype(BF16)
    assert w_proj.shape[1] == IN_COLS

    def lane_pad(v):
        return jnp.pad(v, (0, LANES - v.shape[0])).reshape(1, LANES)

    head_of_col = jnp.arange(SSD_D_INNER) // SSD_HEAD_DIM
    expand = (jnp.arange(LANES)[:, None] == head_of_col[None, :]).astype(BF16)
    gate_up = jnp.zeros((LANES, GLA_KEY_DIM), F32).at[SMALL_GLR:SMALL_GLR + GLA_GATE_RANK].set(gla_gate_up)
    keys = jnp.transpose(peer_sub_keys, (1, 0, 2, 3)).reshape(2 * PEER_HEADS, PEER_N_KEYS, PEER_DQ // 2)
    return {
        "norm1_w": norm1_w.reshape(1, D_MODEL),
        "w_proj": w_proj,
        "w_merge": w_in[:, mg0:end].astype(BF16),
        "conv_w": conv_w,
        "conv_b": conv_b.reshape(1, SSD_CONV_DIM),
        "dt_bias": lane_pad(dt_bias),
        "a_log": lane_pad(a_log),
        "d_exp": jnp.repeat(d_skip, SSD_HEAD_DIM).reshape(1, SSD_D_INNER),
        "ssd_norm_w": ssd_norm_w.reshape(1, SSD_D_INNER),
        "expand": expand,
        "w_ssd_out": w_ssd_out.astype(BF16),
        "gate_up": gate_up.astype(BF16),
        "gate_b": gla_gate_b.reshape(1, GLA_KEY_DIM),
        "gla_norm_w": jnp.tile(gla_norm_w, GLA_HEADS).reshape(1, GLA_VAL_DIM),
        "w_gla_out": w_gla_out.astype(BF16),
        "w_out": w_out.astype(BF16),
        "norm2_w": norm2_w.reshape(1, D_MODEL),
        "peer_wq_t": peer_w_q.T.astype(BF16),
        "peer_keys": keys.astype(BF16),
        "peer_u": peer_u.astype(BF16),
        "peer_v": peer_v.astype(BF16),
        "final_norm_w": final_norm_w.reshape(1, D_MODEL),
    }


def _prompt_path(x_prompt, meta_tokens, w):
    b, seq, _ = x_prompt.shape
    assert seq % SSD_CHUNK == 0
    meta = jnp.broadcast_to(meta_tokens.astype(x_prompt.dtype)[None], (b, N_META, D_MODEL))
    xp = jnp.concatenate([jnp.zeros((b, PROMPT_ZERO_ROWS, D_MODEL), x_prompt.dtype), meta, x_prompt], axis=1)
    rows = xp.shape[1]
    proj = norm_matmul(xp.reshape(b * rows, D_MODEL), w["norm1_w"], w["w_proj"], IN_COLS // 5)
    y_ssd, st_ssd, conv_tail = ssd_prompt(proj, b, rows // SSD_CHUNK, PROMPT_ZERO_ROWS, PROMPT_SKIP_ROWS, w)
    o_gla, st_gla = gla_prompt(proj, b, rows // GLA_CHUNK, PROMPT_ZERO_ROWS, PROMPT_SKIP_ROWS, w)
    x1 = merge(x_prompt.reshape(b * seq, D_MODEL), y_ssd.reshape(b * seq, SSD_D_INNER),
               o_gla.reshape(b * seq, GLA_VAL_DIM), w)
    y = peer_final(x1, w).reshape(b, seq, D_MODEL)
    return (y,
            st_ssd.reshape(1, b, SSD_HEADS, SSD_HEAD_DIM, SSD_D_STATE),
            conv_tail[:, SUBLANES - (SSD_CONV - 1):, :][None],
            st_gla.reshape(1, b, GLA_HEADS, GLA_DK, GLA_DV))


def _sample_path(x_sample, state_ssd, state_conv, state_gla, w):
    b = x_sample.shape[0]
    x = x_sample.reshape(b, D_MODEL)
    proj = norm_matmul(x, w["norm1_w"], w["w_proj"], IN_COLS // 5)
    xact, xdt, decay, bm, cm, conv_new = ssd_step_pre(
        proj, state_conv.reshape(b, (SSD_CONV - 1) * SSD_CONV_DIM), w)
    gw = SSD_GROUPS * SSD_D_STATE
    st_ssd, y_col = ssd_step_state(
        state_ssd.reshape(b, SSD_D_INNER, SSD_D_STATE), xdt.reshape(b, SSD_D_INNER, 1),
        decay.reshape(b, SSD_D_INNER, 1), bm.reshape(b, 1, gw), cm.reshape(b, 1, gw))
    y_ssd = ssd_step_post(y_col.reshape(b, SSD_D_INNER), xact, proj, w)
    qs, kc, gdecay = gla_step_pre(proj, w)
    st_gla, o_gla = gla_step_state(
        state_gla.reshape(b, GLA_KEY_DIM, GLA_DV), qs.reshape(b, GLA_KEY_DIM, 1),
        kc.reshape(b, GLA_KEY_DIM, 1), gdecay.reshape(b, GLA_KEY_DIM, 1),
        proj.reshape(b, 1, IN_COLS), w)
    x1 = merge(x, y_ssd, o_gla.reshape(b, GLA_VAL_DIM), w)
    y = peer_final(x1, w).reshape(b, 1, D_MODEL)
    return (y,
            st_ssd.reshape(1, b, SSD_HEADS, SSD_HEAD_DIM, SSD_D_STATE),
            conv_new.reshape(1, b, SSD_CONV - 1, SSD_CONV_DIM),
            st_gla.reshape(1, b, GLA_HEADS, GLA_DK, GLA_DV))


def kernel(x_prompt, x_sample, state_ssd, state_conv, state_gla, meta_tokens, norm1_w, w_in, conv_w, conv_b,
           dt_bias, a_log, d_skip, ssd_norm_w, w_ssd_out, gla_gate_up, gla_gate_b, gla_norm_w, w_gla_out,
           w_out, norm2_w, peer_w_q, peer_sub_keys, peer_u, peer_v, final_norm_w):
    layer = (norm1_w, w_in, conv_w, conv_b, dt_bias, a_log, d_skip, ssd_norm_w, w_ssd_out, gla_gate_up,
             gla_gate_b, gla_norm_w, w_gla_out, w_out, norm2_w, peer_w_q, peer_sub_keys, peer_u, peer_v)
    assert all(p.shape[0] == 1 for p in layer), "single-layer step"
    w = _prepare_weights(*[p[0] for p in layer], final_norm_w)
    yp, p_ssd, p_conv, p_gla = _prompt_path(x_prompt, meta_tokens, w)
    ys, s_ssd, s_conv, s_gla = _sample_path(x_sample, state_ssd[0], state_conv[0], state_gla[0], w)
    return (yp, ys, p_ssd, p_conv, p_gla, s_ssd, s_conv, s_gla)
```

```python
import functools

import jax
import jax.numpy as jnp
from jax import lax
from jax.experimental import pallas as pl
from jax.experimental.pallas import tpu as pltpu

F32 = jnp.float32
BF16 = jnp.bfloat16

LANES = 128
SUBLANES = 8
VMEM_LIMIT_BYTES = 56 * 1024 * 1024

D_MODEL = 1024
N_META = 16
EPS = 1e-6
SSD_D_INNER = 2 * D_MODEL
SSD_HEAD_DIM = 64
SSD_HEADS = SSD_D_INNER // SSD_HEAD_DIM
SSD_GROUPS = 4
SSD_HPG = SSD_HEADS // SSD_GROUPS
SSD_D_STATE = 128
SSD_CONV = 4
SSD_CHUNK = 128
SSD_BC = 2 * SSD_GROUPS * SSD_D_STATE
SSD_CONV_DIM = SSD_D_INNER + SSD_BC
SSD_GROUP_W = SSD_HPG * SSD_HEAD_DIM
GLA_HEADS = 4
GLA_KEY_DIM = D_MODEL // 2
GLA_VAL_DIM = D_MODEL
GLA_DK = GLA_KEY_DIM // GLA_HEADS
GLA_DV = GLA_VAL_DIM // GLA_HEADS
GLA_GATE_RANK = 16
GLA_GATE_NORMALIZER = 16.0
GLA_CHUNK = 64
PEER_HEADS = 8
PEER_N_KEYS = 128
PEER_N_EXPERTS = PEER_N_KEYS * PEER_N_KEYS
PEER_DQ = 256
PEER_TOPK = 16
PEER_SLABS_PER_STEP = 8
PEER_STEP_EXPERTS = PEER_SLABS_PER_STEP * PEER_N_KEYS
PEER_SCATTER_UNROLL = 16
PEER_PITCH = PEER_N_KEYS + 4
PEER_TOKEN_BLOCK = 256

_IN_SIZES = (SSD_D_INNER, SSD_CONV_DIM, SSD_HEADS, GLA_KEY_DIM, GLA_KEY_DIM, GLA_VAL_DIM,
             GLA_VAL_DIM, GLA_GATE_RANK, 2 * D_MODEL)
COL_Z = 0
COL_XS = COL_Z + SSD_D_INNER
COL_BC = COL_XS + SSD_D_INNER
COL_Q = COL_BC + SSD_BC
COL_K = COL_Q + GLA_KEY_DIM
COL_V = COL_K + GLA_KEY_DIM
COL_G = COL_V + GLA_VAL_DIM
COL_SMALL = COL_G + GLA_VAL_DIM
IN_COLS = COL_SMALL + LANES
SMALL_GLR = SSD_HEADS
PROMPT_ZERO_ROWS = SSD_CHUNK - N_META
PROMPT_SKIP_ROWS = PROMPT_ZERO_ROWS + N_META


def _cparams(*sem):
    return pltpu.CompilerParams(dimension_semantics=sem, vmem_limit_bytes=VMEM_LIMIT_BYTES)


def _const_spec(shape):
    nd = len(shape)
    return pl.BlockSpec(shape, lambda *_: (0,) * nd, pipeline_mode=pl.Buffered(1))


def _row_tile(t, cap):
    tm = cap
    while t % tm:
        tm //= 2
    return tm


def _split3(x):
    hi = x.astype(BF16)
    r1 = x - hi.astype(F32)
    mid = r1.astype(BF16)
    lo = (r1 - mid.astype(F32)).astype(BF16)
    return hi, mid, lo


def _dot(a, b):
    return jnp.dot(a, b, preferred_element_type=F32)


def _dot_nt(a, b):
    return lax.dot_general(a, b, (((1,), (1,)), ((), ())), preferred_element_type=F32)


def _sel_right(sel01, x):
    hi, mid, lo = _split3(x)
    return _dot(hi, sel01) + _dot(mid, sel01) + _dot(lo, sel01)


def _sel_left(sel01, x):
    hi, mid, lo = _split3(x)
    return _dot(sel01, hi) + _dot(sel01, mid) + _dot(sel01, lo)


def _rms_scale(x):
    return x * lax.rsqrt(jnp.mean(x * x, axis=-1, keepdims=True) + EPS)


def _tril(n):
    r = lax.broadcasted_iota(jnp.int32, (n, n), 0)
    c = lax.broadcasted_iota(jnp.int32, (n, n), 1)
    return r >= c


def _norm_matmul_kernel(x_ref, nw_ref, w_ref, o_ref, xn_ref):
    @pl.when(pl.program_id(1) == 0)
    def _():
        xn_ref[...] = (_rms_scale(x_ref[...]) * nw_ref[...]).astype(BF16)

    o_ref[...] = _dot(xn_ref[...], w_ref[...])


def norm_matmul(x, norm_w, w, tn):
    t, d = x.shape
    n = w.shape[1]
    tm = _row_tile(t, 1024)
    assert n % tn == 0
    return pl.pallas_call(
        _norm_matmul_kernel,
        grid=(t // tm, n // tn),
        in_specs=[
            pl.BlockSpec((tm, d), lambda i, j: (i, 0)),
            pl.BlockSpec((1, d), lambda i, j: (0, 0)),
            pl.BlockSpec((d, tn), lambda i, j: (0, j)),
        ],
        out_specs=pl.BlockSpec((tm, tn), lambda i, j: (i, j)),
        out_shape=jax.ShapeDtypeStruct((t, n), F32),
        scratch_shapes=[pltpu.VMEM((tm, d), BF16)],
        compiler_params=_cparams("parallel", "arbitrary"),
        name="norm_in_proj",
    )(x, norm_w, w)


def _ssd_dt(small, dt_bias_ref, valid):
    lane = lax.broadcasted_iota(jnp.int32, small.shape, 1)
    dt = jax.nn.softplus(small + dt_bias_ref[...])
    keep = lane < SSD_HEADS
    if valid is not None:
        keep = jnp.logical_and(keep, valid)
    return jnp.where(keep, dt, 0.0)


def _ssd_gated_norm(y, z, norm_w_ref):
    y = y * jax.nn.silu(z)
    parts = []
    for g in range(SSD_GROUPS):
        parts.append(_rms_scale(y[:, g * SSD_GROUP_W:(g + 1) * SSD_GROUP_W]))
    return jnp.concatenate(parts, axis=1) * norm_w_ref[...]


def _ssd_prompt_kernel(z_ref, xs_ref, bc_ref, small_ref, conv_w_ref, conv_b_ref, dt_bias_ref,
                       a_log_ref, d_exp_ref, norm_w_ref, expand_ref,
                       y_ref, state_ref, convtail_ref,
                       hist_ref, st_ref, yacc_ref, *, zero_rows):
    c = pl.program_id(1)
    nc = pl.num_programs(1)
    q = SSD_CHUNK

    @pl.when(c == 0)
    def _():
        hist_ref[0:SUBLANES, :] = jnp.zeros((SUBLANES, SSD_CONV_DIM), F32)
        st_ref[...] = jnp.zeros(st_ref.shape, F32)

    hist_ref[SUBLANES:SUBLANES + q, 0:SSD_D_INNER] = xs_ref[...]
    hist_ref[SUBLANES:SUBLANES + q, SSD_D_INNER:SSD_CONV_DIM] = bc_ref[...]
    conv = conv_b_ref[...]
    for j in range(SSD_CONV):
        start = SUBLANES - (SSD_CONV - 1) + j
        conv = conv + conv_w_ref[j:j + 1, :] * hist_ref[start:start + q, :]
    xbc = jax.nn.silu(conv)
    tail = hist_ref[q:q + SUBLANES, :]
    hist_ref[0:SUBLANES, :] = tail
    convtail_ref[0] = tail

    xs = xbc[:, 0:SSD_D_INNER]
    bm = xbc[:, SSD_D_INNER:SSD_D_INNER + SSD_GROUPS * SSD_D_STATE].astype(BF16)
    cm = xbc[:, SSD_D_INNER + SSD_GROUPS * SSD_D_STATE:].astype(BF16)

    row = lax.broadcasted_iota(jnp.int32, (q, LANES), 0) + c * q
    dt = _ssd_dt(small_ref[...], dt_bias_ref, row >= zero_rows)
    da = dt * (-jnp.exp(a_log_ref[...]))
    tril = _tril(q)
    acum = _sel_left(tril.astype(BF16), da)
    acum_t = acum.T
    expand = expand_ref[...]
    dt_exp = _sel_right(expand, dt)
    acum_exp = _sel_right(expand, acum)
    acum_last = acum_exp[q - 1:q, :]
    xdt = xs * dt_exp
    xdt_bf = xdt.astype(BF16)
    x_end = (xdt * jnp.exp(acum_last - acum_exp)).astype(BF16)

    lane_lo = lax.broadcasted_iota(jnp.int32, (q, LANES), 1) < SSD_HEAD_DIM
    for g in range(SSD_GROUPS):
        bg = bm[:, g * SSD_D_STATE:(g + 1) * SSD_D_STATE]
        cg = cm[:, g * SSD_D_STATE:(g + 1) * SSD_D_STATE]
        cb = _dot_nt(cg, bg)
        for hp in range(SSD_HPG // 2):
            lo = (g * SSD_HPG + 2 * hp) * SSD_HEAD_DIM
            pair = []
            for k in range(2):
                h = g * SSD_HPG + 2 * hp + k
                diff = acum[:, h:h + 1] - acum_t[h:h + 1, :]
                seg = jnp.exp(jnp.where(tril, diff, -jnp.inf))
                pair.append(_dot((cb * seg).astype(BF16), xdt_bf[:, lo:lo + LANES]))
            yacc_ref[:, lo:lo + LANES] = jnp.where(lane_lo, pair[0], pair[1])

        gs = slice(g * SSD_GROUP_W, (g + 1) * SSD_GROUP_W)
        st_g = st_ref[:, gs]
        y_off = _dot(cg, st_g.astype(BF16)) * jnp.exp(acum_exp[:, gs])
        yacc_ref[:, gs] = yacc_ref[:, gs] + y_off
        st_ref[:, gs] = st_g * jnp.exp(acum_last[:, gs]) + _dot(bg.T, x_end[:, gs])

    y = yacc_ref[...] + d_exp_ref[...] * xs
    y_ref[0] = _ssd_gated_norm(y, z_ref[...], norm_w_ref).astype(BF16)

    @pl.when(c == nc - 1)
    def _():
        state_ref[0] = st_ref[...].T


def ssd_prompt(proj, batch, n_chunks, zero_rows, skip_rows, w):
    q = SSD_CHUNK
    assert skip_rows % q == 0
    first_real = skip_rows // q
    n_out = n_chunks - first_real

    def rows(b, c):
        return b * n_chunks + c

    return pl.pallas_call(
        functools.partial(_ssd_prompt_kernel, zero_rows=zero_rows),
        grid=(batch, n_chunks),
        in_specs=[
            pl.BlockSpec((q, SSD_D_INNER), lambda b, c: (rows(b, c), COL_Z // SSD_D_INNER)),
            pl.BlockSpec((q, SSD_D_INNER), lambda b, c: (rows(b, c), COL_XS // SSD_D_INNER)),
            pl.BlockSpec((q, SSD_BC), lambda b, c: (rows(b, c), COL_BC // SSD_BC)),
            pl.BlockSpec((q, LANES), lambda b, c: (rows(b, c), COL_SMALL // LANES)),
            _const_spec((SSD_CONV, SSD_CONV_DIM)),
            _const_spec((1, SSD_CONV_DIM)),
            _const_spec((1, LANES)),
            _const_spec((1, LANES)),
            _const_spec((1, SSD_D_INNER)),
            _const_spec((1, SSD_D_INNER)),
            _const_spec((LANES, SSD_D_INNER)),
        ],
        out_specs=[
            pl.BlockSpec((1, q, SSD_D_INNER), lambda b, c: (b, jnp.maximum(c - first_real, 0), 0)),
            pl.BlockSpec((1, SSD_D_INNER, SSD_D_STATE), lambda b, c: (b, 0, 0)),
            pl.BlockSpec((1, SUBLANES, SSD_CONV_DIM), lambda b, c: (b, 0, 0)),
        ],
        out_shape=[
            jax.ShapeDtypeStruct((batch, n_out * q, SSD_D_INNER), BF16),
            jax.ShapeDtypeStruct((batch, SSD_D_INNER, SSD_D_STATE), F32),
            jax.ShapeDtypeStruct((batch, SUBLANES, SSD_CONV_DIM), F32),
        ],
        scratch_shapes=[
            pltpu.VMEM((SUBLANES + q, SSD_CONV_DIM), F32),
            pltpu.VMEM((SSD_D_STATE, SSD_D_INNER), F32),
            pltpu.VMEM((q, SSD_D_INNER), F32),
        ],
        compiler_params=_cparams("parallel", "arbitrary"),
        name="ssd_prompt",
    )(proj, proj, proj, proj, w["conv_w"], w["conv_b"], w["dt_bias"], w["a_log"], w["d_exp"],
      w["ssd_norm_w"], w["expand"])


def _ssd_step_pre_kernel(xs_ref, bc_ref, small_ref, c0_ref, c1_ref, c2_ref, conv_w_ref, conv_b_ref,
                         dt_bias_ref, a_log_ref, expand_ref,
                         xact_ref, xdt_ref, decay_ref, bm_ref, cm_ref, convnew_ref):
    x = jnp.concatenate([xs_ref[...], bc_ref[...]], axis=1)
    hist = (c0_ref[...], c1_ref[...], c2_ref[...], x)
    conv = conv_b_ref[...]
    for j in range(SSD_CONV):
        conv = conv + conv_w_ref[j:j + 1, :] * hist[j]
    xbc = jax.nn.silu(conv)
    xs = xbc[:, 0:SSD_D_INNER]
    dt = _ssd_dt(small_ref[...], dt_bias_ref, None)
    da = dt * (-jnp.exp(a_log_ref[...]))
    expand = expand_ref[...]
    xact_ref[...] = xs
    xdt_ref[...] = xs * _sel_right(expand, dt)
    decay_ref[...] = jnp.exp(da)
    bm_ref[...] = xbc[:, SSD_D_INNER:SSD_D_INNER + SSD_GROUPS * SSD_D_STATE]
    cm_ref[...] = xbc[:, SSD_D_INNER + SSD_GROUPS * SSD_D_STATE:]
    convnew_ref[:, 0:SSD_CONV_DIM] = hist[1]
    convnew_ref[:, SSD_CONV_DIM:2 * SSD_CONV_DIM] = hist[2]
    convnew_ref[:, 2 * SSD_CONV_DIM:] = x


def ssd_step_pre(proj, conv_state, w):
    t = proj.shape[0]
    tm = _row_tile(t, 128)
    gw = SSD_GROUPS * SSD_D_STATE
    row = lambda width, col: pl.BlockSpec((tm, width), lambda i: (i, col))
    return pl.pallas_call(
        _ssd_step_pre_kernel,
        grid=(t // tm,),
        in_specs=[
            row(SSD_D_INNER, COL_XS // SSD_D_INNER),
            row(SSD_BC, COL_BC // SSD_BC),
            row(LANES, COL_SMALL // LANES),
            row(SSD_CONV_DIM, 0), row(SSD_CONV_DIM, 1), row(SSD_CONV_DIM, 2),
            _const_spec((SSD_CONV, SSD_CONV_DIM)),
            _const_spec((1, SSD_CONV_DIM)),
            _const_spec((1, LANES)),
            _const_spec((1, LANES)),
            _const_spec((LANES, SSD_D_INNER)),
        ],
        out_specs=[row(SSD_D_INNER, 0), row(SSD_D_INNER, 0), row(LANES, 0), row(gw, 0),
                   row(gw, 0), row((SSD_CONV - 1) * SSD_CONV_DIM, 0)],
        out_shape=[
            jax.ShapeDtypeStruct((t, SSD_D_INNER), F32),
            jax.ShapeDtypeStruct((t, SSD_D_INNER), F32),
            jax.ShapeDtypeStruct((t, LANES), F32),
            jax.ShapeDtypeStruct((t, gw), F32),
            jax.ShapeDtypeStruct((t, gw), F32),
            jax.ShapeDtypeStruct((t, (SSD_CONV - 1) * SSD_CONV_DIM), F32),
        ],
        compiler_params=_cparams("parallel"),
        name="ssd_step_pre",
    )(proj, proj, proj, conv_state, conv_state, conv_state, w["conv_w"], w["conv_b"], w["dt_bias"],
      w["a_log"], w["expand"])


def _dot_tn(a, b):
    return lax.dot_general(a, b, (((0,), (0,)), ((), ())), preferred_element_type=F32)


def _own_row(j):
    return lax.broadcasted_iota(jnp.int32, (SUBLANES, 1), 0) == j


def _ssd_step_state_kernel(s_ref, xdt_ref, dec_ref, bm_ref, cm_ref, snew_ref, y_ref):
    j = pl.program_id(1)
    own = _own_row(j)
    dec = dec_ref[pl.ds(j, 1), :]

    @pl.when(j == 0)
    def _():
        y_ref[...] = jnp.zeros(y_ref.shape, F32)

    def group_rows(x):
        return jnp.concatenate(
            [x[:, g * SSD_D_STATE:(g + 1) * SSD_D_STATE] for g in range(SSD_GROUPS)], axis=0)

    group_of_lane = lax.broadcasted_iota(jnp.int32, (SUBLANES, SSD_D_INNER), 1) // SSD_GROUP_W
    xdt = jnp.where(own, xdt_ref[...], 0.0)
    x_rows = jnp.concatenate(
        [jnp.where(group_of_lane == g, xdt, 0.0) for g in range(SSD_GROUPS)], axis=0).astype(BF16)
    b_rows = group_rows(bm_ref[...]).astype(BF16)
    c_rows = group_rows(jnp.where(own, cm_ref[...], 0.0)).astype(BF16)
    decayed = [s_ref[0, h * SSD_HEAD_DIM:(h + 1) * SSD_HEAD_DIM, :] * dec[:, h:h + 1]
               for h in range(SSD_HEADS)]
    s_new = jnp.concatenate(decayed, axis=0) + _dot_tn(x_rows, b_rows)
    snew_ref[0] = s_new
    y_rows = _dot_nt(c_rows, s_new.astype(BF16))
    y_ref[...] += jnp.concatenate(
        [y_rows[g * SUBLANES:(g + 1) * SUBLANES, g * SSD_GROUP_W:(g + 1) * SSD_GROUP_W]
         for g in range(SSD_GROUPS)], axis=1)


def ssd_step_state(state, xdt, dec, bm, cm):
    t = state.shape[0]
    assert t % SUBLANES == 0
    gw = SSD_GROUPS * SSD_D_STATE
    rows = lambda width: pl.BlockSpec((SUBLANES, width), lambda i, j: (i, 0))
    st = pl.BlockSpec((1, SSD_D_INNER, SSD_D_STATE), lambda i, j: (i * SUBLANES + j, 0, 0))
    return pl.pallas_call(
        _ssd_step_state_kernel,
        grid=(t // SUBLANES, SUBLANES),
        in_specs=[st, rows(SSD_D_INNER), rows(LANES), rows(gw), rows(gw)],
        out_specs=[st, rows(SSD_D_INNER)],
        out_shape=[
            jax.ShapeDtypeStruct((t, SSD_D_INNER, SSD_D_STATE), F32),
            jax.ShapeDtypeStruct((t, SSD_D_INNER), F32),
        ],
        compiler_params=_cparams("parallel", "arbitrary"),
        name="ssd_step_state",
    )(state, xdt, dec, bm, cm)


def _ssd_step_post_kernel(y_ref, xact_ref, z_ref, d_exp_ref, norm_w_ref, o_ref):
    y = y_ref[...] + d_exp_ref[...] * xact_ref[...]
    o_ref[...] = _ssd_gated_norm(y, z_ref[...], norm_w_ref).astype(BF16)


def ssd_step_post(y, xact, proj, w):
    t = y.shape[0]
    tm = _row_tile(t, 128)
    row = lambda col: pl.BlockSpec((tm, SSD_D_INNER), lambda i: (i, col))
    return pl.pallas_call(
        _ssd_step_post_kernel,
        grid=(t // tm,),
        in_specs=[row(0), row(0), row(COL_Z // SSD_D_INNER), _const_spec((1, SSD_D_INNER)),
                  _const_spec((1, SSD_D_INNER))],
        out_specs=row(0),
        out_shape=jax.ShapeDtypeStruct((t, SSD_D_INNER), BF16),
        compiler_params=_cparams("parallel"),
        name="ssd_step_post",
    )(y, xact, proj, w["d_exp"], w["ssd_norm_w"])


def _gla_gate(small, gate_up_ref, gate_b_ref):
    pre = _dot(small.astype(BF16), gate_up_ref[...]) + gate_b_ref[...]
    return jax.nn.log_sigmoid(pre) / GLA_GATE_NORMALIZER


def _gla_out_norm(o, g, norm_w_ref):
    parts = []
    for h in range(GLA_HEADS):
        parts.append(_rms_scale(o[:, h * GLA_DV:(h + 1) * GLA_DV]))
    return jnp.concatenate(parts, axis=1) * norm_w_ref[...] * jax.nn.silu(g)


def _gla_prompt_kernel(q_ref, k_ref, v_ref, g_ref, small_ref, gate_up_ref, gate_b_ref, norm_w_ref,
                       o_ref, state_ref, s_ref, *, zero_rows):
    c = pl.program_id(1)
    nc = pl.num_programs(1)
    q = GLA_CHUNK

    @pl.when(c == 0)
    def _():
        s_ref[...] = jnp.zeros(s_ref.shape, F32)

    valid = (lax.broadcasted_iota(jnp.int32, (q, GLA_KEY_DIM), 0) + c * q) >= zero_rows
    gk = jnp.where(valid, _gla_gate(small_ref[...], gate_up_ref, gate_b_ref), 0.0)
    kk = jnp.where(valid, k_ref[...], 0.0)
    tril = _tril(q)
    bcum = _sel_left(tril.astype(BF16), gk)
    bcum_last = bcum[q - 1:q, :]
    qe = q_ref[...] * jnp.exp(bcum) * (GLA_DK ** -0.5)
    ke = kk * jnp.exp(-bcum)
    kend = kk * jnp.exp(bcum_last - bcum)
    decay = jnp.exp(bcum_last)
    vv = v_ref[...].astype(BF16)
    outs = []
    for h in range(GLA_HEADS):
        ks = slice(h * GLA_DK, (h + 1) * GLA_DK)
        vs = slice(h * GLA_DV, (h + 1) * GLA_DV)
        qe_h = qe[:, ks].astype(BF16)
        att = jnp.where(tril, _dot_nt(qe_h, ke[:, ks].astype(BF16)), 0.0)
        s_h = s_ref[ks, :]
        outs.append(_dot(att.astype(BF16), vv[:, vs]) + _dot(qe_h, s_h.astype(BF16)))
        decay_col = jnp.broadcast_to(decay[:, ks], (GLA_DK, GLA_DK)).T
        kend_pad = jnp.concatenate([kend[:, ks], jnp.zeros((GLA_DK - q, GLA_DK), F32)], axis=0)
        kend_t = kend_pad.T[:, 0:q].astype(BF16)
        s_ref[ks, :] = (jnp.concatenate([decay_col] * (GLA_DV // GLA_DK), axis=1) * s_h
                        + _dot(kend_t, vv[:, vs]))
    o = jnp.concatenate(outs, axis=1)
    o_ref[0] = _gla_out_norm(o, g_ref[...], norm_w_ref).astype(BF16)

    @pl.when(c == nc - 1)
    def _():
        state_ref[0] = s_ref[...]


def gla_prompt(proj, batch, n_chunks, zero_rows, skip_rows, w):
    q = GLA_CHUNK
    assert skip_rows % q == 0
    first_real = skip_rows // q
    n_out = n_chunks - first_real

    def rows(b, c):
        return b * n_chunks + c

    return pl.pallas_call(
        functools.partial(_gla_prompt_kernel, zero_rows=zero_rows),
        grid=(batch, n_chunks),
        in_specs=[
            pl.BlockSpec((q, GLA_KEY_DIM), lambda b, c: (rows(b, c), COL_Q // GLA_KEY_DIM)),
            pl.BlockSpec((q, GLA_KEY_DIM), lambda b, c: (rows(b, c), COL_K // GLA_KEY_DIM)),
            pl.BlockSpec((q, GLA_VAL_DIM), lambda b, c: (rows(b, c), COL_V // GLA_VAL_DIM)),
            pl.BlockSpec((q, GLA_VAL_DIM), lambda b, c: (rows(b, c), COL_G // GLA_VAL_DIM)),
            pl.BlockSpec((q, LANES), lambda b, c: (rows(b, c), COL_SMALL // LANES)),
            _const_spec((LANES, GLA_KEY_DIM)),
            _const_spec((1, GLA_KEY_DIM)),
            _const_spec((1, GLA_VAL_DIM)),
        ],
        out_specs=[
            pl.BlockSpec((1, q, GLA_VAL_DIM), lambda b, c: (b, jnp.maximum(c - first_real, 0), 0)),
            pl.BlockSpec((1, GLA_KEY_DIM, GLA_DV), lambda b, c: (b, 0, 0)),
        ],
        out_shape=[
            jax.ShapeDtypeStruct((batch, n_out * q, GLA_VAL_DIM), BF16),
            jax.ShapeDtypeStruct((batch, GLA_KEY_DIM, GLA_DV), F32),
        ],
        scratch_shapes=[pltpu.VMEM((GLA_KEY_DIM, GLA_DV), F32)],
        compiler_params=_cparams("parallel", "arbitrary"),
        name="gla_prompt",
    )(proj, proj, proj, proj, proj, w["gate_up"], w["gate_b"], w["gla_norm_w"])


def _gla_step_kernel(s_ref, q_ref, k_ref, v_ref, g_ref, small_ref, gate_up_ref, gate_b_ref, norm_w_ref,
                     snew_ref, o_ref, dec_ref, acc_ref):
    j = pl.program_id(1)

    @pl.when(j == 0)
    def _():
        dec_ref[...] = jnp.exp(_gla_gate(small_ref[...], gate_up_ref, gate_b_ref))
        acc_ref[...] = jnp.zeros(acc_ref.shape, F32)

    own = _own_row(j)
    head_of_lane = lax.broadcasted_iota(jnp.int32, (SUBLANES, GLA_KEY_DIM), 1) // GLA_DK

    def per_head_rows(x):
        return jnp.concatenate([jnp.where(head_of_lane == h, x, 0.0) for h in range(GLA_HEADS)], axis=0)

    k_rows = per_head_rows(jnp.where(own, k_ref[...], 0.0)).astype(BF16)
    q_rows = per_head_rows(jnp.where(own, q_ref[...] * (GLA_DK ** -0.5), 0.0)).astype(BF16)
    vv = v_ref[...]
    v_rows = jnp.concatenate(
        [vv[:, h * GLA_DV:(h + 1) * GLA_DV] for h in range(GLA_HEADS)], axis=0).astype(BF16)
    pieces = [p.astype(F32) for p in _split3(jnp.where(own, dec_ref[...], 0.0))]
    d_rows = jnp.concatenate(pieces + [jnp.zeros_like(pieces[0])], axis=0).astype(BF16)
    decay = _dot_tn(d_rows, jnp.ones((d_rows.shape[0], GLA_DV), BF16))
    s_new = s_ref[0] * decay + _dot_tn(k_rows, v_rows)
    snew_ref[0] = s_new
    o_rows = _dot(q_rows, s_new.astype(BF16))
    acc_ref[...] += jnp.concatenate(
        [o_rows[h * SUBLANES:(h + 1) * SUBLANES, :] for h in range(GLA_HEADS)], axis=1)

    @pl.when(j == pl.num_programs(1) - 1)
    def _():
        o_ref[...] = _gla_out_norm(acc_ref[...], g_ref[...], norm_w_ref)


def gla_step(state, proj, w):
    t = state.shape[0]
    assert t % SUBLANES == 0
    rows = lambda width, col: pl.BlockSpec((SUBLANES, width), lambda i, j: (i, col))
    st = pl.BlockSpec((1, GLA_KEY_DIM, GLA_DV), lambda i, j: (i * SUBLANES + j, 0, 0))
    return pl.pallas_call(
        _gla_step_kernel,
        grid=(t // SUBLANES, SUBLANES),
        in_specs=[st, rows(GLA_KEY_DIM, COL_Q // GLA_KEY_DIM), rows(GLA_KEY_DIM, COL_K // GLA_KEY_DIM),
                  rows(GLA_VAL_DIM, COL_V // GLA_VAL_DIM), rows(GLA_VAL_DIM, COL_G // GLA_VAL_DIM),
                  rows(LANES, COL_SMALL // LANES), _const_spec((LANES, GLA_KEY_DIM)),
                  _const_spec((1, GLA_KEY_DIM)), _const_spec((1, GLA_VAL_DIM))],
        out_specs=[st, rows(GLA_VAL_DIM, 0)],
        out_shape=[
            jax.ShapeDtypeStruct((t, GLA_KEY_DIM, GLA_DV), F32),
            jax.ShapeDtypeStruct((t, GLA_VAL_DIM), F32),
        ],
        scratch_shapes=[pltpu.VMEM((SUBLANES, GLA_KEY_DIM), F32), pltpu.VMEM((SUBLANES, GLA_VAL_DIM), F32)],
        compiler_params=_cparams("parallel", "arbitrary"),
        name="gla_step",
    )(state, proj, proj, proj, proj, proj, w["gate_up"], w["gate_b"], w["gla_norm_w"])


def _merge_kernel(x_ref, ys_ref, og_ref, nw_ref, wm_ref, wso_ref, wgo_ref, wout_ref, o_ref):
    x = x_ref[...]
    xn = (_rms_scale(x) * nw_ref[...]).astype(BF16)
    gates = jax.nn.sigmoid(_dot(xn, wm_ref[...]))
    y_ssd = _dot(ys_ref[...].astype(BF16), wso_ref[...])
    y_gla = _dot(og_ref[...].astype(BF16), wgo_ref[...])
    mix = gates[:, 0:D_MODEL] * y_ssd + gates[:, D_MODEL:] * y_gla
    o_ref[...] = x + _dot(mix.astype(BF16), wout_ref[...])


def merge(x, y_ssd, o_gla, w):
    t = x.shape[0]
    tm = _row_tile(t, 512)
    row = lambda width: pl.BlockSpec((tm, width), lambda i: (i, 0))
    return pl.pallas_call(
        _merge_kernel,
        grid=(t // tm,),
        in_specs=[row(D_MODEL), row(SSD_D_INNER), row(GLA_VAL_DIM), _const_spec((1, D_MODEL)),
                  _const_spec((D_MODEL, 2 * D_MODEL)), _const_spec((SSD_D_INNER, D_MODEL)),
                  _const_spec((GLA_VAL_DIM, D_MODEL)), _const_spec((D_MODEL, D_MODEL))],
        out_specs=row(D_MODEL),
        out_shape=jax.ShapeDtypeStruct((t, D_MODEL), F32),
        compiler_params=_cparams("parallel"),
        name="merge",
    )(x, y_ssd, o_gla, w["norm1_w"], w["w_merge"], w["w_ssd_out"], w["w_gla_out"], w["w_out"])


def _merge_exchange_pairs(n):
    pairs = []
    p = n // 2
    while p > 0:
        q, r, d = n // 2, 0, p
        while d > 0:
            pairs += [(i, i + d) for i in range(n - d) if (i & p) == r]
            d, q, r = q - p, q // 2, p
        p //= 2
    return pairs


_SORT16 = _merge_exchange_pairs(PEER_TOPK)
_BITONIC16 = [(i, i + d) for d in (8, 4, 2, 1) for i in range(PEER_TOPK) if (i & d) == 0]


def _exchange(v, ids, i, j):
    if ids is None:
        v[i], v[j] = jnp.maximum(v[i], v[j]), jnp.minimum(v[i], v[j])
    else:
        ge = v[i] >= v[j]
        v[i], v[j] = jnp.where(ge, v[i], v[j]), jnp.where(ge, v[j], v[i])
        ids[i], ids[j] = jnp.where(ge, ids[i], ids[j]), jnp.where(ge, ids[j], ids[i])


def _merge_top16(v, ids, w, wids):
    n, m = len(v), len(w)
    for i in range(n - m, n):
        o = n - 1 - i
        if ids is None:
            v[i] = jnp.maximum(v[i], w[o])
        else:
            ge = v[i] >= w[o]
            v[i] = jnp.where(ge, v[i], w[o])
            ids[i] = jnp.where(ge, ids[i], wids[o])
    for i, j in _BITONIC16:
        _exchange(v, ids, i, j)


def _top16_rows(x, with_ids):
    n = x.shape[0] // SUBLANES
    assert n == PEER_TOPK
    v = [x[SUBLANES * r:SUBLANES * (r + 1), :] for r in range(n)]
    ids = None
    if with_ids:
        row = lax.broadcasted_iota(jnp.int32, v[0].shape, 0).astype(F32)
        ids = [row + float(SUBLANES * r) for r in range(n)]
    for i, j in _SORT16:
        _exchange(v, ids, i, j)
    shift = SUBLANES // 2
    while shift:
        w = [pltpu.roll(a, shift, 0) for a in v]
        wids = [pltpu.roll(a, shift, 0) for a in ids] if with_ids else None
        _merge_top16(v, ids, w, wids)
        shift //= 2
    return v, ids


def _peer_kernel(x_ref, nw_ref, fnw_ref, wqt_ref, keys_ref, u_ref, v_ref, o_ref,
                 hn_ref, rw_ref, s2_ref, svk_ref, sik_ref, stat_ref, g1k_ref, ids_ref, g1tm_ref,
                 hid_ref, acc_ref):
    s = pl.program_id(1)
    ns = pl.num_programs(1)
    tb = x_ref.shape[0]
    nk = PEER_N_KEYS
    nj = PEER_HEADS * PEER_TOPK

    @pl.when(s == 0)
    def _select():
        acc_ref[...] = jnp.zeros(acc_ref.shape, F32)
        hid_ref[...] = jnp.zeros(hid_ref.shape, F32)
        hn_ref[...] = (_rms_scale(x_ref[...]) * nw_ref[...]).astype(BF16)
        stat_ref[...] = jnp.zeros(stat_ref.shape, F32)
        hn = hn_ref[...]
        for h in range(PEER_HEADS):
            for c in range(2):
                r0 = (h * 2 + c) * (PEER_DQ // 2)
                q_t = _dot_nt(wqt_ref[r0:r0 + PEER_DQ // 2, :], hn)
                sc_t = _dot(keys_ref[h * 2 + c], q_t.astype(BF16))
                vals, ids = _top16_rows(sc_t, c == 0)
                for k in range(PEER_TOPK):
                    svk_ref[c, k, h:h + 1, :] = vals[k][0:1, :]
                    if c == 0:
                        sik_ref[k, h:h + 1, :] = ids[k][0:1, :]
                if c == 1:
                    s2_ref[h] = sc_t.T

        pairs = ([(0, k2) for k2 in range(PEER_TOPK)]
                 + [(k1, k2) for k1 in range(1, PEER_TOPK // 2) for k2 in range(PEER_TOPK // (k1 + 1))]
                 + [(k1, 0) for k1 in range(PEER_TOPK // 2, PEER_TOPK)])
        for g in range(tb // LANES):
            ls = slice(g * LANES, (g + 1) * LANES)
            a = [svk_ref[0, k, :, ls] for k in range(PEER_TOPK)]
            b = [svk_ref[1, k, :, ls] for k in range(PEER_TOPK)]
            best = [a[0] + b[k2] for k2 in range(PEER_TOPK)]
            for k1 in range(1, PEER_TOPK // 2):
                _merge_top16(best, None, [a[k1] + b[k2] for k2 in range(PEER_TOPK // (k1 + 1))], None)
            _merge_top16(best, None, [a[k1] + b[0] for k1 in range(PEER_TOPK // 2, PEER_TOPK)], None)
            tau, top = best[PEER_TOPK - 1], best[0]
            z = jnp.zeros_like(tau)
            for k1, k2 in pairs:
                cv = a[k1] + b[k2]
                z = z + jnp.where(cv >= tau, jnp.exp(cv - top), 0.0)
            stat_ref[0:PEER_HEADS, ls] = tau
            stat_ref[PEER_HEADS:2 * PEER_HEADS, ls] = b[0]
            inv_z = 1.0 / z
            for k in range(PEER_TOPK):
                g1k_ref[k, :, ls] = jnp.exp(a[k] - a[0]) * inv_z

        sv1 = svk_ref[0].reshape(nj, tb).T
        stat = stat_ref[...].T
        ids_ref[...] = sik_ref[...].reshape(nj, tb).T
        g1tm_ref[...] = g1k_ref[...].reshape(nj, tb).T
        for h in range(PEER_HEADS):
            s2 = s2_ref[h]
            tau_b = jnp.broadcast_to(stat[:, h:h + 1], s2.shape)
            e2 = jnp.exp(s2 - stat[:, PEER_HEADS + h:PEER_HEADS + h + 1])
            for k in range(PEER_TOPK):
                j = k * PEER_HEADS + h
                keep = (s2 + sv1[:, j:j + 1]) >= tau_b
                rw_ref[pl.ds(j, tb, stride=PEER_PITCH), :] = jnp.where(keep, e2, 0.0)

        key1 = lax.broadcasted_iota(jnp.int32, (nk, nk), 0).astype(F32)

        def scatter(t, carry):
            base = t * PEER_PITCH
            rows = rw_ref[pl.ds(base, nk), :].astype(BF16)
            place = jnp.where(key1 == ids_ref[pl.ds(t, 1), :], g1tm_ref[pl.ds(t, 1), :], 0.0)
            rw_ref[pl.ds(base, nk), :] = _dot(place.astype(BF16), rows)
            return carry

        lax.fori_loop(0, tb, scatter, 0, unroll=PEER_SCATTER_UNROLL)

    prev = hid_ref[...]
    act = 0.5 * prev * (1.0 + lax.erf(prev * (2.0 ** -0.5)))
    slab0 = jnp.maximum(s - 1, 0) * PEER_SLABS_PER_STEP
    wts = jnp.concatenate(
        [rw_ref[pl.ds(slab0 + i, tb, stride=PEER_PITCH), :] for i in range(PEER_SLABS_PER_STEP)], axis=1)
    gated = (act * wts).astype(BF16)
    hn = hn_ref[...]
    half = PEER_STEP_EXPERTS // 2
    for c in range(2):
        hid_ref[:, c * half:(c + 1) * half] = _dot_nt(hn, u_ref[c * half:(c + 1) * half, :])
    half = D_MODEL // 2
    for c in range(2):
        acc_ref[:, c * half:(c + 1) * half] += _dot(gated, v_ref[:, c * half:(c + 1) * half])

    @pl.when(s == ns - 1)
    def _():
        o_ref[...] = _rms_scale(x_ref[...] + acc_ref[...]) * fnw_ref[...]


def peer_final(x, w):
    t = x.shape[0]
    tb = _row_tile(t, PEER_TOKEN_BLOCK)
    assert tb % LANES == 0
    nblk = PEER_N_EXPERTS // PEER_STEP_EXPERTS
    return pl.pallas_call(
        _peer_kernel,
        grid=(t // tb, nblk + 1),
        in_specs=[
            pl.BlockSpec((tb, D_MODEL), lambda i, s: (i, 0)),
            _const_spec((1, D_MODEL)),
            _const_spec((1, D_MODEL)),
            _const_spec((PEER_HEADS * PEER_DQ, D_MODEL)),
            _const_spec((2 * PEER_HEADS, PEER_N_KEYS, PEER_DQ // 2)),
            pl.BlockSpec((PEER_STEP_EXPERTS, D_MODEL), lambda i, s: (jnp.minimum(s, nblk - 1), 0)),
            pl.BlockSpec((PEER_STEP_EXPERTS, D_MODEL), lambda i, s: (jnp.maximum(s - 1, 0), 0)),
        ],
        out_specs=pl.BlockSpec((tb, D_MODEL), lambda i, s: (i, 0)),
        out_shape=jax.ShapeDtypeStruct((t, D_MODEL), F32),
        scratch_shapes=[
            pltpu.VMEM((tb, D_MODEL), BF16),
            pltpu.VMEM((tb * PEER_PITCH, PEER_N_KEYS), F32),
            pltpu.VMEM((PEER_HEADS, tb, PEER_N_KEYS), F32),
            pltpu.VMEM((2, PEER_TOPK, PEER_HEADS, tb), F32),
            pltpu.VMEM((PEER_TOPK, PEER_HEADS, tb), F32),
            pltpu.VMEM((LANES, tb), F32),
            pltpu.VMEM((PEER_TOPK, PEER_HEADS, tb), F32),
            pltpu.VMEM((tb, PEER_HEADS * PEER_TOPK), F32),
            pltpu.VMEM((tb, PEER_HEADS * PEER_TOPK), F32),
            pltpu.VMEM((tb, PEER_STEP_EXPERTS), F32),
            pltpu.VMEM((tb, D_MODEL), F32),
        ],
        compiler_params=_cparams("parallel", "arbitrary"),
        name="peer_final",
    )(x, w["norm2_w"], w["final_norm_w"], w["peer_wq_t"], w["peer_keys"], w["peer_u"], w["peer_v"])


def _prepare_weights(norm1_w, w_in, conv_w, conv_b, dt_bias, a_log, d_skip, ssd_norm_w, w_ssd_out,
                     gla_gate_up, gla_gate_b, gla_norm_w, w_gla_out, w_out, norm2_w, peer_w_q,
                     peer_sub_keys, peer_u, peer_v, final_norm_w):
    offs = [0]
    for sz in _IN_SIZES:
        offs.append(offs[-1] + sz)
    z0, xbc0, dt0, q0, _, _, _, glr0, mg0, end = offs
    small = jnp.concatenate(
        [w_in[:, dt0:q0], w_in[:, glr0:mg0],
         jnp.zeros((D_MODEL, LANES - SSD_HEADS - GLA_GATE_RANK), w_in.dtype)], axis=1)
    w_proj = jnp.concatenate([w_in[:, z0:dt0], w_in[:, q0:glr0], small], axis=1).astype(BF16)
    assert w_proj.shape[1] == IN_COLS

    def lane_pad(v):
        return jnp.pad(v, (0, LANES - v.shape[0])).reshape(1, LANES)

    head_of_col = jnp.arange(SSD_D_INNER) // SSD_HEAD_DIM
    expand = (jnp.arange(LANES)[:, None] == head_of_col[None, :]).astype(BF16)
    gate_up = jnp.zeros((LANES, GLA_KEY_DIM), F32).at[SMALL_GLR:SMALL_GLR + GLA_GATE_RANK].set(gla_gate_up)
    keys = jnp.transpose(peer_sub_keys, (1, 0, 2, 3)).reshape(2 * PEER_HEADS, PEER_N_KEYS, PEER_DQ // 2)
    return {
        "norm1_w": norm1_w.reshape(1, D_MODEL),
        "w_proj": w_proj,
        "w_merge": w_in[:, mg0:end].astype(BF16),
        "conv_w": conv_w,
        "conv_b": conv_b.reshape(1, SSD_CONV_DIM),
        "dt_bias": lane_pad(dt_bias),
        "a_log": lane_pad(a_log),
        "d_exp": jnp.repeat(d_skip, SSD_HEAD_DIM).reshape(1, SSD_D_INNER),
        "ssd_norm_w": ssd_norm_w.reshape(1, SSD_D_INNER),
        "expand": expand,
        "w_ssd_out": w_ssd_out.astype(BF16),
        "gate_up": gate_up.astype(BF16),
        "gate_b": gla_gate_b.reshape(1, GLA_KEY_DIM),
        "gla_norm_w": jnp.tile(gla_norm_w, GLA_HEADS).reshape(1, GLA_VAL_DIM),
        "w_gla_out": w_gla_out.astype(BF16),
        "w_out": w_out.astype(BF16),
        "norm2_w": norm2_w.reshape(1, D_MODEL),
        "peer_wq_t": peer_w_q.T.astype(BF16),
        "peer_keys": keys.astype(BF16),
        "peer_u": peer_u.astype(BF16),
        "peer_v": peer_v.astype(BF16),
        "final_norm_w": final_norm_w.reshape(1, D_MODEL),
    }


def _prompt_path(x_prompt, meta_tokens, w):
    b, seq, _ = x_prompt.shape
    assert seq % SSD_CHUNK == 0
    meta = jnp.broadcast_to(meta_tokens.astype(x_prompt.dtype)[None], (b, N_META, D_MODEL))
    xp = jnp.concatenate([jnp.zeros((b, PROMPT_ZERO_ROWS, D_MODEL), x_prompt.dtype), meta, x_prompt], axis=1)
    rows = xp.shape[1]
    proj = norm_matmul(xp.reshape(b * rows, D_MODEL), w["norm1_w"], w["w_proj"], IN_COLS // 5)
    y_ssd, st_ssd, conv_tail = ssd_prompt(proj, b, rows // SSD_CHUNK, PROMPT_ZERO_ROWS, PROMPT_SKIP_ROWS, w)
    o_gla, st_gla = gla_prompt(proj, b, rows // GLA_CHUNK, PROMPT_ZERO_ROWS, PROMPT_SKIP_ROWS, w)
    x1 = merge(x_prompt.reshape(b * seq, D_MODEL), y_ssd.reshape(b * seq, SSD_D_INNER),
               o_gla.reshape(b * seq, GLA_VAL_DIM), w)
    y = peer_final(x1, w).reshape(b, seq, D_MODEL)
    return (y,
            st_ssd.reshape(1, b, SSD_HEADS, SSD_HEAD_DIM, SSD_D_STATE),
            conv_tail[:, SUBLANES - (SSD_CONV - 1):, :][None],
            st_gla.reshape(1, b, GLA_HEADS, GLA_DK, GLA_DV))


def _sample_path(x_sample, state_ssd, state_conv, state_gla, w):
    b = x_sample.shape[0]
    x = x_sample.reshape(b, D_MODEL)
    proj = norm_matmul(x, w["norm1_w"], w["w_proj"], IN_COLS // 5)
    xact, xdt, dec, bm, cm, conv_new = ssd_step_pre(
        proj, state_conv.reshape(b, (SSD_CONV - 1) * SSD_CONV_DIM), w)
    st_ssd, y = ssd_step_state(state_ssd.reshape(b, SSD_D_INNER, SSD_D_STATE), xdt, dec, bm, cm)
    y_ssd = ssd_step_post(y, xact, proj, w)
    st_gla, o_gla = gla_step(state_gla.reshape(b, GLA_KEY_DIM, GLA_DV), proj, w)
    x1 = merge(x, y_ssd, o_gla, w)
    y = peer_final(x1, w).reshape(b, 1, D_MODEL)
    return (y,
            st_ssd.reshape(1, b, SSD_HEADS, SSD_HEAD_DIM, SSD_D_STATE),
            conv_new.reshape(1, b, SSD_CONV - 1, SSD_CONV_DIM),
            st_gla.reshape(1, b, GLA_HEADS, GLA_DK, GLA_DV))


def kernel(x_prompt, x_sample, state_ssd, state_conv, state_gla, meta_tokens, norm1_w, w_in, conv_w, conv_b,
           dt_bias, a_log, d_skip, ssd_norm_w, w_ssd_out, gla_gate_up, gla_gate_b, gla_norm_w, w_gla_out,
           w_out, norm2_w, peer_w_q, peer_sub_keys, peer_u, peer_v, final_norm_w):
    layer = (norm1_w, w_in, conv_w, conv_b, dt_bias, a_log, d_skip, ssd_norm_w, w_ssd_out, gla_gate_up,
             gla_gate_b, gla_norm_w, w_gla_out, w_out, norm2_w, peer_w_q, peer_sub_keys, peer_u, peer_v)
    assert all(p.shape[0] == 1 for p in layer), "single-layer step"
    w = _prepare_weights(*[p[0] for p in layer], final_norm_w)
    yp, p_ssd, p_conv, p_gla = _prompt_path(x_prompt, meta_tokens, w)
    ys, s_ssd, s_conv, s_gla = _sample_path(x_sample, state_ssd[0], state_conv[0], state_gla[0], w)
    return (yp, ys, p_ssd, p_conv, p_gla, s_ssd, s_conv, s_gla)
```

```python
import functools

import jax
import jax.numpy as jnp
from jax import lax
from jax.experimental import pallas as pl
from jax.experimental.pallas import tpu as pltpu

F32 = jnp.float32
BF16 = jnp.bfloat16

LANES = 128
SUBLANES = 8
VMEM_LIMIT_BYTES = 56 * 1024 * 1024

D_MODEL = 1024
N_META = 16
EPS = 1e-6
SSD_D_INNER = 2 * D_MODEL
SSD_HEAD_DIM = 64
SSD_HEADS = SSD_D_INNER // SSD_HEAD_DIM
SSD_GROUPS = 4
SSD_HPG = SSD_HEADS // SSD_GROUPS
SSD_D_STATE = 128
SSD_CONV = 4
SSD_CHUNK = 128
SSD_BC = 2 * SSD_GROUPS * SSD_D_STATE
SSD_CONV_DIM = SSD_D_INNER + SSD_BC
SSD_GROUP_W = SSD_HPG * SSD_HEAD_DIM
GLA_HEADS = 4
GLA_KEY_DIM = D_MODEL // 2
GLA_VAL_DIM = D_MODEL
GLA_DK = GLA_KEY_DIM // GLA_HEADS
GLA_DV = GLA_VAL_DIM // GLA_HEADS
GLA_GATE_RANK = 16
GLA_GATE_NORMALIZER = 16.0
GLA_CHUNK = 64
PEER_HEADS = 8
PEER_N_KEYS = 128
PEER_N_EXPERTS = PEER_N_KEYS * PEER_N_KEYS
PEER_DQ = 256
PEER_TOPK = 16
PEER_SLABS_PER_STEP = 16
PEER_STEP_EXPERTS = PEER_SLABS_PER_STEP * PEER_N_KEYS
PEER_SCATTER_UNROLL = 16
PEER_PITCH = PEER_N_KEYS + 4
PEER_TOKEN_BLOCK = 256

_IN_SIZES = (SSD_D_INNER, SSD_CONV_DIM, SSD_HEADS, GLA_KEY_DIM, GLA_KEY_DIM, GLA_VAL_DIM,
             GLA_VAL_DIM, GLA_GATE_RANK, 2 * D_MODEL)
COL_Z = 0
COL_XS = COL_Z + SSD_D_INNER
COL_BC = COL_XS + SSD_D_INNER
COL_Q = COL_BC + SSD_BC
COL_K = COL_Q + GLA_KEY_DIM
COL_V = COL_K + GLA_KEY_DIM
COL_G = COL_V + GLA_VAL_DIM
COL_SMALL = COL_G + GLA_VAL_DIM
IN_COLS = COL_SMALL + LANES
SMALL_GLR = SSD_HEADS
PROMPT_ZERO_ROWS = SSD_CHUNK - N_META
PROMPT_SKIP_ROWS = PROMPT_ZERO_ROWS + N_META


def _cparams(*sem):
    return pltpu.CompilerParams(dimension_semantics=sem, vmem_limit_bytes=VMEM_LIMIT_BYTES)


def _const_spec(shape):
    nd = len(shape)
    return pl.BlockSpec(shape, lambda *_: (0,) * nd, pipeline_mode=pl.Buffered(1))


def _row_tile(t, cap):
    tm = cap
    while t % tm:
        tm //= 2
    return tm


def _split3(x):
    hi = x.astype(BF16)
    r1 = x - hi.astype(F32)
    mid = r1.astype(BF16)
    lo = (r1 - mid.astype(F32)).astype(BF16)
    return hi, mid, lo


def _dot(a, b):
    return jnp.dot(a, b, preferred_element_type=F32)


def _dot_nt(a, b):
    return lax.dot_general(a, b, (((1,), (1,)), ((), ())), preferred_element_type=F32)


def _sel_right(sel01, x):
    hi, mid, lo = _split3(x)
    return _dot(hi, sel01) + _dot(mid, sel01) + _dot(lo, sel01)


def _sel_left(sel01, x):
    hi, mid, lo = _split3(x)
    return _dot(sel01, hi) + _dot(sel01, mid) + _dot(sel01, lo)


def _rms_scale(x):
    return x * lax.rsqrt(jnp.mean(x * x, axis=-1, keepdims=True) + EPS)


def _tril(n):
    r = lax.broadcasted_iota(jnp.int32, (n, n), 0)
    c = lax.broadcasted_iota(jnp.int32, (n, n), 1)
    return r >= c


def _norm_matmul_kernel(x_ref, nw_ref, w_ref, o_ref, xn_ref):
    @pl.when(pl.program_id(1) == 0)
    def _():
        xn_ref[...] = (_rms_scale(x_ref[...]) * nw_ref[...]).astype(BF16)

    o_ref[...] = _dot(xn_ref[...], w_ref[...])


def norm_matmul(x, norm_w, w, tn):
    t, d = x.shape
    n = w.shape[1]
    tm = _row_tile(t, 1024)
    assert n % tn == 0
    return pl.pallas_call(
        _norm_matmul_kernel,
        grid=(t // tm, n // tn),
        in_specs=[
            pl.BlockSpec((tm, d), lambda i, j: (i, 0)),
            pl.BlockSpec((1, d), lambda i, j: (0, 0)),
            pl.BlockSpec((d, tn), lambda i, j: (0, j)),
        ],
        out_specs=pl.BlockSpec((tm, tn), lambda i, j: (i, j)),
        out_shape=jax.ShapeDtypeStruct((t, n), F32),
        scratch_shapes=[pltpu.VMEM((tm, d), BF16)],
        compiler_params=_cparams("parallel", "arbitrary"),
        name="norm_in_proj",
    )(x, norm_w, w)


def _ssd_dt(small, dt_bias_ref, valid):
    lane = lax.broadcasted_iota(jnp.int32, small.shape, 1)
    dt = jax.nn.softplus(small + dt_bias_ref[...])
    keep = lane < SSD_HEADS
    if valid is not None:
        keep = jnp.logical_and(keep, valid)
    return jnp.where(keep, dt, 0.0)


def _ssd_gated_norm(y, z, norm_w_ref):
    y = y * jax.nn.silu(z)
    parts = []
    for g in range(SSD_GROUPS):
        parts.append(_rms_scale(y[:, g * SSD_GROUP_W:(g + 1) * SSD_GROUP_W]))
    return jnp.concatenate(parts, axis=1) * norm_w_ref[...]


def _ssd_prompt_kernel(z_ref, xs_ref, bc_ref, small_ref, conv_w_ref, conv_b_ref, dt_bias_ref,
                       a_log_ref, d_exp_ref, norm_w_ref, expand_ref,
                       y_ref, state_ref, convtail_ref,
                       hist_ref, st_ref, yacc_ref, *, zero_rows):
    c = pl.program_id(1)
    nc = pl.num_programs(1)
    q = SSD_CHUNK

    @pl.when(c == 0)
    def _():
        hist_ref[0:SUBLANES, :] = jnp.zeros((SUBLANES, SSD_CONV_DIM), F32)
        st_ref[...] = jnp.zeros(st_ref.shape, F32)

    hist_ref[SUBLANES:SUBLANES + q, 0:SSD_D_INNER] = xs_ref[...]
    hist_ref[SUBLANES:SUBLANES + q, SSD_D_INNER:SSD_CONV_DIM] = bc_ref[...]
    conv = conv_b_ref[...]
    for j in range(SSD_CONV):
        start = SUBLANES - (SSD_CONV - 1) + j
        conv = conv + conv_w_ref[j:j + 1, :] * hist_ref[start:start + q, :]
    xbc = jax.nn.silu(conv)
    tail = hist_ref[q:q + SUBLANES, :]
    hist_ref[0:SUBLANES, :] = tail
    convtail_ref[0] = tail

    xs = xbc[:, 0:SSD_D_INNER]
    bm = xbc[:, SSD_D_INNER:SSD_D_INNER + SSD_GROUPS * SSD_D_STATE].astype(BF16)
    cm = xbc[:, SSD_D_INNER + SSD_GROUPS * SSD_D_STATE:].astype(BF16)

    row = lax.broadcasted_iota(jnp.int32, (q, LANES), 0) + c * q
    dt = _ssd_dt(small_ref[...], dt_bias_ref, row >= zero_rows)
    da = dt * (-jnp.exp(a_log_ref[...]))
    tril = _tril(q)
    acum = _sel_left(tril.astype(BF16), da)
    acum_t = acum.T
    expand = expand_ref[...]
    dt_exp = _sel_right(expand, dt)
    acum_exp = _sel_right(expand, acum)
    acum_last = acum_exp[q - 1:q, :]
    xdt = xs * dt_exp
    xdt_bf = xdt.astype(BF16)
    x_end = (xdt * jnp.exp(acum_last - acum_exp)).astype(BF16)

    lane_lo = lax.broadcasted_iota(jnp.int32, (q, LANES), 1) < SSD_HEAD_DIM
    for g in range(SSD_GROUPS):
        bg = bm[:, g * SSD_D_STATE:(g + 1) * SSD_D_STATE]
        cg = cm[:, g * SSD_D_STATE:(g + 1) * SSD_D_STATE]
        cb = _dot_nt(cg, bg)
        for hp in range(SSD_HPG // 2):
            lo = (g * SSD_HPG + 2 * hp) * SSD_HEAD_DIM
            pair = []
            for k in range(2):
                h = g * SSD_HPG + 2 * hp + k
                diff = acum[:, h:h + 1] - acum_t[h:h + 1, :]
                seg = jnp.exp(jnp.where(tril, diff, -jnp.inf))
                pair.append(_dot((cb * seg).astype(BF16), xdt_bf[:, lo:lo + LANES]))
            yacc_ref[:, lo:lo + LANES] = jnp.where(lane_lo, pair[0], pair[1])

        gs = slice(g * SSD_GROUP_W, (g + 1) * SSD_GROUP_W)
        st_g = st_ref[:, gs]
        y_off = _dot(cg, st_g.astype(BF16)) * jnp.exp(acum_exp[:, gs])
        yacc_ref[:, gs] = yacc_ref[:, gs] + y_off
        st_ref[:, gs] = st_g * jnp.exp(acum_last[:, gs]) + _dot(bg.T, x_end[:, gs])

    y = yacc_ref[...] + d_exp_ref[...] * xs
    y_ref[0] = _ssd_gated_norm(y, z_ref[...], norm_w_ref).astype(BF16)

    @pl.when(c == nc - 1)
    def _():
        state_ref[0] = st_ref[...].T


def ssd_prompt(proj, batch, n_chunks, zero_rows, skip_rows, w):
    q = SSD_CHUNK
    assert skip_rows % q == 0
    first_real = skip_rows // q
    n_out = n_chunks - first_real

    def rows(b, c):
        return b * n_chunks + c

    return pl.pallas_call(
        functools.partial(_ssd_prompt_kernel, zero_rows=zero_rows),
        grid=(batch, n_chunks),
        in_specs=[
            pl.BlockSpec((q, SSD_D_INNER), lambda b, c: (rows(b, c), COL_Z // SSD_D_INNER)),
            pl.BlockSpec((q, SSD_D_INNER), lambda b, c: (rows(b, c), COL_XS // SSD_D_INNER)),
            pl.BlockSpec((q, SSD_BC), lambda b, c: (rows(b, c), COL_BC // SSD_BC)),
            pl.BlockSpec((q, LANES), lambda b, c: (rows(b, c), COL_SMALL // LANES)),
            _const_spec((SSD_CONV, SSD_CONV_DIM)),
            _const_spec((1, SSD_CONV_DIM)),
            _const_spec((1, LANES)),
            _const_spec((1, LANES)),
            _const_spec((1, SSD_D_INNER)),
            _const_spec((1, SSD_D_INNER)),
            _const_spec((LANES, SSD_D_INNER)),
        ],
        out_specs=[
            pl.BlockSpec((1, q, SSD_D_INNER), lambda b, c: (b, jnp.maximum(c - first_real, 0), 0)),
            pl.BlockSpec((1, SSD_D_INNER, SSD_D_STATE), lambda b, c: (b, 0, 0)),
            pl.BlockSpec((1, SUBLANES, SSD_CONV_DIM), lambda b, c: (b, 0, 0)),
        ],
        out_shape=[
            jax.ShapeDtypeStruct((batch, n_out * q, SSD_D_INNER), BF16),
            jax.ShapeDtypeStruct((batch, SSD_D_INNER, SSD_D_STATE), F32),
            jax.ShapeDtypeStruct((batch, SUBLANES, SSD_CONV_DIM), F32),
        ],
        scratch_shapes=[
            pltpu.VMEM((SUBLANES + q, SSD_CONV_DIM), F32),
            pltpu.VMEM((SSD_D_STATE, SSD_D_INNER), F32),
            pltpu.VMEM((q, SSD_D_INNER), F32),
        ],
        compiler_params=_cparams("parallel", "arbitrary"),
        name="ssd_prompt",
    )(proj, proj, proj, proj, w["conv_w"], w["conv_b"], w["dt_bias"], w["a_log"], w["d_exp"],
      w["ssd_norm_w"], w["expand"])


def _ssd_step_pre_kernel(xs_ref, bc_ref, small_ref, c0_ref, c1_ref, c2_ref, conv_w_ref, conv_b_ref,
                         dt_bias_ref, a_log_ref, expand_ref,
                         xact_ref, xdt_ref, decay_ref, bm_ref, cm_ref, convnew_ref):
    x = jnp.concatenate([xs_ref[...], bc_ref[...]], axis=1)
    hist = (c0_ref[...], c1_ref[...], c2_ref[...], x)
    conv = conv_b_ref[...]
    for j in range(SSD_CONV):
        conv = conv + conv_w_ref[j:j + 1, :] * hist[j]
    xbc = jax.nn.silu(conv)
    xs = xbc[:, 0:SSD_D_INNER]
    dt = _ssd_dt(small_ref[...], dt_bias_ref, None)
    da = dt * (-jnp.exp(a_log_ref[...]))
    expand = expand_ref[...]
    xact_ref[...] = xs
    xdt_ref[...] = xs * _sel_right(expand, dt)
    decay_ref[...] = jnp.exp(da)
    bm_ref[...] = xbc[:, SSD_D_INNER:SSD_D_INNER + SSD_GROUPS * SSD_D_STATE]
    cm_ref[...] = xbc[:, SSD_D_INNER + SSD_GROUPS * SSD_D_STATE:]
    convnew_ref[:, 0:SSD_CONV_DIM] = hist[1]
    convnew_ref[:, SSD_CONV_DIM:2 * SSD_CONV_DIM] = hist[2]
    convnew_ref[:, 2 * SSD_CONV_DIM:] = x


def ssd_step_pre(proj, conv_state, w):
    t = proj.shape[0]
    tm = _row_tile(t, 128)
    gw = SSD_GROUPS * SSD_D_STATE
    row = lambda width, col: pl.BlockSpec((tm, width), lambda i: (i, col))
    return pl.pallas_call(
        _ssd_step_pre_kernel,
        grid=(t // tm,),
        in_specs=[
            row(SSD_D_INNER, COL_XS // SSD_D_INNER),
            row(SSD_BC, COL_BC // SSD_BC),
            row(LANES, COL_SMALL // LANES),
            row(SSD_CONV_DIM, 0), row(SSD_CONV_DIM, 1), row(SSD_CONV_DIM, 2),
            _const_spec((SSD_CONV, SSD_CONV_DIM)),
            _const_spec((1, SSD_CONV_DIM)),
            _const_spec((1, LANES)),
            _const_spec((1, LANES)),
            _const_spec((LANES, SSD_D_INNER)),
        ],
        out_specs=[row(SSD_D_INNER, 0), row(SSD_D_INNER, 0), row(LANES, 0), row(gw, 0),
                   row(gw, 0), row((SSD_CONV - 1) * SSD_CONV_DIM, 0)],
        out_shape=[
            jax.ShapeDtypeStruct((t, SSD_D_INNER), F32),
            jax.ShapeDtypeStruct((t, SSD_D_INNER), F32),
            jax.ShapeDtypeStruct((t, LANES), F32),
            jax.ShapeDtypeStruct((t, gw), F32),
            jax.ShapeDtypeStruct((t, gw), F32),
            jax.ShapeDtypeStruct((t, (SSD_CONV - 1) * SSD_CONV_DIM), F32),
        ],
        compiler_params=_cparams("parallel"),
        name="ssd_step_pre",
    )(proj, proj, proj, conv_state, conv_state, conv_state, w["conv_w"], w["conv_b"], w["dt_bias"],
      w["a_log"], w["expand"])


def _dot_tn(a, b):
    return lax.dot_general(a, b, (((0,), (0,)), ((), ())), preferred_element_type=F32)


def _own_row(j):
    return lax.broadcasted_iota(jnp.int32, (SUBLANES, 1), 0) == j


def _ssd_step_state_kernel(s_ref, xdt_ref, dec_ref, bm_ref, cm_ref, snew_ref, y_ref):
    j = pl.program_id(1)
    own = _own_row(j)
    dec = dec_ref[pl.ds(j, 1), :]

    @pl.when(j == 0)
    def _():
        y_ref[...] = jnp.zeros(y_ref.shape, F32)

    def group_rows(x):
        return jnp.concatenate(
            [x[:, g * SSD_D_STATE:(g + 1) * SSD_D_STATE] for g in range(SSD_GROUPS)], axis=0)

    group_of_lane = lax.broadcasted_iota(jnp.int32, (SUBLANES, SSD_D_INNER), 1) // SSD_GROUP_W
    xdt = jnp.where(own, xdt_ref[...], 0.0)
    x_rows = jnp.concatenate(
        [jnp.where(group_of_lane == g, xdt, 0.0) for g in range(SSD_GROUPS)], axis=0).astype(BF16)
    b_rows = group_rows(bm_ref[...]).astype(BF16)
    c_rows = group_rows(jnp.where(own, cm_ref[...], 0.0)).astype(BF16)
    decayed = [s_ref[0, h * SSD_HEAD_DIM:(h + 1) * SSD_HEAD_DIM, :] * dec[:, h:h + 1]
               for h in range(SSD_HEADS)]
    s_new = jnp.concatenate(decayed, axis=0) + _dot_tn(x_rows, b_rows)
    snew_ref[0] = s_new
    y_rows = _dot_nt(c_rows, s_new.astype(BF16))
    y_ref[...] += jnp.concatenate(
        [y_rows[g * SUBLANES:(g + 1) * SUBLANES, g * SSD_GROUP_W:(g + 1) * SSD_GROUP_W]
         for g in range(SSD_GROUPS)], axis=1)


def ssd_step_state(state, xdt, dec, bm, cm):
    t = state.shape[0]
    assert t % SUBLANES == 0
    gw = SSD_GROUPS * SSD_D_STATE
    rows = lambda width: pl.BlockSpec((SUBLANES, width), lambda i, j: (i, 0))
    st = pl.BlockSpec((1, SSD_D_INNER, SSD_D_STATE), lambda i, j: (i * SUBLANES + j, 0, 0))
    return pl.pallas_call(
        _ssd_step_state_kernel,
        grid=(t // SUBLANES, SUBLANES),
        in_specs=[st, rows(SSD_D_INNER), rows(LANES), rows(gw), rows(gw)],
        out_specs=[st, rows(SSD_D_INNER)],
        out_shape=[
            jax.ShapeDtypeStruct((t, SSD_D_INNER, SSD_D_STATE), F32),
            jax.ShapeDtypeStruct((t, SSD_D_INNER), F32),
        ],
        compiler_params=_cparams("parallel", "arbitrary"),
        name="ssd_step_state",
    )(state, xdt, dec, bm, cm)


def _ssd_step_post_kernel(y_ref, xact_ref, z_ref, d_exp_ref, norm_w_ref, o_ref):
    y = y_ref[...] + d_exp_ref[...] * xact_ref[...]
    o_ref[...] = _ssd_gated_norm(y, z_ref[...], norm_w_ref).astype(BF16)


def ssd_step_post(y, xact, proj, w):
    t = y.shape[0]
    tm = _row_tile(t, 128)
    row = lambda col: pl.BlockSpec((tm, SSD_D_INNER), lambda i: (i, col))
    return pl.pallas_call(
        _ssd_step_post_kernel,
        grid=(t // tm,),
        in_specs=[row(0), row(0), row(COL_Z // SSD_D_INNER), _const_spec((1, SSD_D_INNER)),
                  _const_spec((1, SSD_D_INNER))],
        out_specs=row(0),
        out_shape=jax.ShapeDtypeStruct((t, SSD_D_INNER), BF16),
        compiler_params=_cparams("parallel"),
        name="ssd_step_post",
    )(y, xact, proj, w["d_exp"], w["ssd_norm_w"])


def _gla_gate(small, gate_up_ref, gate_b_ref):
    pre = _dot(small.astype(BF16), gate_up_ref[...]) + gate_b_ref[...]
    return jax.nn.log_sigmoid(pre) / GLA_GATE_NORMALIZER


def _gla_out_norm(o, g, norm_w_ref):
    parts = []
    for h in range(GLA_HEADS):
        parts.append(_rms_scale(o[:, h * GLA_DV:(h + 1) * GLA_DV]))
    return jnp.concatenate(parts, axis=1) * norm_w_ref[...] * jax.nn.silu(g)


def _gla_prompt_kernel(q_ref, k_ref, v_ref, g_ref, small_ref, gate_up_ref, gate_b_ref, norm_w_ref,
                       o_ref, state_ref, s_ref, *, zero_rows):
    c = pl.program_id(1)
    nc = pl.num_programs(1)
    q = GLA_CHUNK

    @pl.when(c == 0)
    def _():
        s_ref[...] = jnp.zeros(s_ref.shape, F32)

    valid = (lax.broadcasted_iota(jnp.int32, (q, GLA_KEY_DIM), 0) + c * q) >= zero_rows
    gk = jnp.where(valid, _gla_gate(small_ref[...], gate_up_ref, gate_b_ref), 0.0)
    kk = jnp.where(valid, k_ref[...], 0.0)
    tril = _tril(q)
    bcum = _sel_left(tril.astype(BF16), gk)
    bcum_last = bcum[q - 1:q, :]
    qe = q_ref[...] * jnp.exp(bcum) * (GLA_DK ** -0.5)
    ke = kk * jnp.exp(-bcum)
    kend = kk * jnp.exp(bcum_last - bcum)
    decay = jnp.exp(bcum_last)
    vv = v_ref[...].astype(BF16)
    outs = []
    for h in range(GLA_HEADS):
        ks = slice(h * GLA_DK, (h + 1) * GLA_DK)
        vs = slice(h * GLA_DV, (h + 1) * GLA_DV)
        qe_h = qe[:, ks].astype(BF16)
        att = jnp.where(tril, _dot_nt(qe_h, ke[:, ks].astype(BF16)), 0.0)
        s_h = s_ref[ks, :]
        outs.append(_dot(att.astype(BF16), vv[:, vs]) + _dot(qe_h, s_h.astype(BF16)))
        decay_col = jnp.broadcast_to(decay[:, ks], (GLA_DK, GLA_DK)).T
        kend_pad = jnp.concatenate([kend[:, ks], jnp.zeros((GLA_DK - q, GLA_DK), F32)], axis=0)
        kend_t = kend_pad.T[:, 0:q].astype(BF16)
        s_ref[ks, :] = (jnp.concatenate([decay_col] * (GLA_DV // GLA_DK), axis=1) * s_h
                        + _dot(kend_t, vv[:, vs]))
    o = jnp.concatenate(outs, axis=1)
    o_ref[0] = _gla_out_norm(o, g_ref[...], norm_w_ref).astype(BF16)

    @pl.when(c == nc - 1)
    def _():
        state_ref[0] = s_ref[...]


def gla_prompt(proj, batch, n_chunks, zero_rows, skip_rows, w):
    q = GLA_CHUNK
    assert skip_rows % q == 0
    first_real = skip_rows // q
    n_out = n_chunks - first_real

    def rows(b, c):
        return b * n_chunks + c

    return pl.pallas_call(
        functools.partial(_gla_prompt_kernel, zero_rows=zero_rows),
        grid=(batch, n_chunks),
        in_specs=[
            pl.BlockSpec((q, GLA_KEY_DIM), lambda b, c: (rows(b, c), COL_Q // GLA_KEY_DIM)),
            pl.BlockSpec((q, GLA_KEY_DIM), lambda b, c: (rows(b, c), COL_K // GLA_KEY_DIM)),
            pl.BlockSpec((q, GLA_VAL_DIM), lambda b, c: (rows(b, c), COL_V // GLA_VAL_DIM)),
            pl.BlockSpec((q, GLA_VAL_DIM), lambda b, c: (rows(b, c), COL_G // GLA_VAL_DIM)),
            pl.BlockSpec((q, LANES), lambda b, c: (rows(b, c), COL_SMALL // LANES)),
            _const_spec((LANES, GLA_KEY_DIM)),
            _const_spec((1, GLA_KEY_DIM)),
            _const_spec((1, GLA_VAL_DIM)),
        ],
        out_specs=[
            pl.BlockSpec((1, q, GLA_VAL_DIM), lambda b, c: (b, jnp.maximum(c - first_real, 0), 0)),
            pl.BlockSpec((1, GLA_KEY_DIM, GLA_DV), lambda b, c: (b, 0, 0)),
        ],
        out_shape=[
            jax.ShapeDtypeStruct((batch, n_out * q, GLA_VAL_DIM), BF16),
            jax.ShapeDtypeStruct((batch, GLA_KEY_DIM, GLA_DV), F32),
        ],
        scratch_shapes=[pltpu.VMEM((GLA_KEY_DIM, GLA_DV), F32)],
        compiler_params=_cparams("parallel", "arbitrary"),
        name="gla_prompt",
    )(proj, proj, proj, proj, proj, w["gate_up"], w["gate_b"], w["gla_norm_w"])


def _gla_step_kernel(s_ref, q_ref, k_ref, v_ref, g_ref, small_ref, gate_up_ref, gate_b_ref, norm_w_ref,
                     snew_ref, o_ref, dec_ref, acc_ref):
    j = pl.program_id(1)

    @pl.when(j == 0)
    def _():
        dec_ref[...] = jnp.exp(_gla_gate(small_ref[...], gate_up_ref, gate_b_ref))
        acc_ref[...] = jnp.zeros(acc_ref.shape, F32)

    own = _own_row(j)
    head_of_lane = lax.broadcasted_iota(jnp.int32, (SUBLANES, GLA_KEY_DIM), 1) // GLA_DK

    def per_head_rows(x):
        return jnp.concatenate([jnp.where(head_of_lane == h, x, 0.0) for h in range(GLA_HEADS)], axis=0)

    k_rows = per_head_rows(jnp.where(own, k_ref[...], 0.0)).astype(BF16)
    q_rows = per_head_rows(jnp.where(own, q_ref[...] * (GLA_DK ** -0.5), 0.0)).astype(BF16)
    vv = v_ref[...]
    v_rows = jnp.concatenate(
        [vv[:, h * GLA_DV:(h + 1) * GLA_DV] for h in range(GLA_HEADS)], axis=0).astype(BF16)
    pieces = [p.astype(F32) for p in _split3(jnp.where(own, dec_ref[...], 0.0))]
    d_rows = jnp.concatenate(pieces + [jnp.zeros_like(pieces[0])], axis=0).astype(BF16)
    decay = _dot_tn(d_rows, jnp.ones((d_rows.shape[0], GLA_DV), BF16))
    s_new = s_ref[0] * decay + _dot_tn(k_rows, v_rows)
    snew_ref[0] = s_new
    o_rows = _dot(q_rows, s_new.astype(BF16))
    acc_ref[...] += jnp.concatenate(
        [o_rows[h * SUBLANES:(h + 1) * SUBLANES, :] for h in range(GLA_HEADS)], axis=1)

    @pl.when(j == pl.num_programs(1) - 1)
    def _():
        o_ref[...] = _gla_out_norm(acc_ref[...], g_ref[...], norm_w_ref)


def gla_step(state, proj, w):
    t = state.shape[0]
    assert t % SUBLANES == 0
    rows = lambda width, col: pl.BlockSpec((SUBLANES, width), lambda i, j: (i, col))
    st = pl.BlockSpec((1, GLA_KEY_DIM, GLA_DV), lambda i, j: (i * SUBLANES + j, 0, 0))
    return pl.pallas_call(
        _gla_step_kernel,
        grid=(t // SUBLANES, SUBLANES),
        in_specs=[st, rows(GLA_KEY_DIM, COL_Q // GLA_KEY_DIM), rows(GLA_KEY_DIM, COL_K // GLA_KEY_DIM),
                  rows(GLA_VAL_DIM, COL_V // GLA_VAL_DIM), rows(GLA_VAL_DIM, COL_G // GLA_VAL_DIM),
                  rows(LANES, COL_SMALL // LANES), _const_spec((LANES, GLA_KEY_DIM)),
                  _const_spec((1, GLA_KEY_DIM)), _const_spec((1, GLA_VAL_DIM))],
        out_specs=[st, rows(GLA_VAL_DIM, 0)],
        out_shape=[
            jax.ShapeDtypeStruct((t, GLA_KEY_DIM, GLA_DV), F32),
            jax.ShapeDtypeStruct((t, GLA_VAL_DIM), F32),
        ],
        scratch_shapes=[pltpu.VMEM((SUBLANES, GLA_KEY_DIM), F32), pltpu.VMEM((SUBLANES, GLA_VAL_DIM), F32)],
        compiler_params=_cparams("parallel", "arbitrary"),
        name="gla_step",
    )(state, proj, proj, proj, proj, proj, w["gate_up"], w["gate_b"], w["gla_norm_w"])


def _merge_kernel(x_ref, ys_ref, og_ref, nw_ref, wm_ref, wso_ref, wgo_ref, wout_ref, o_ref):
    x = x_ref[...]
    xn = (_rms_scale(x) * nw_ref[...]).astype(BF16)
    gates = jax.nn.sigmoid(_dot(xn, wm_ref[...]))
    y_ssd = _dot(ys_ref[...].astype(BF16), wso_ref[...])
    y_gla = _dot(og_ref[...].astype(BF16), wgo_ref[...])
    mix = gates[:, 0:D_MODEL] * y_ssd + gates[:, D_MODEL:] * y_gla
    o_ref[...] = x + _dot(mix.astype(BF16), wout_ref[...])


def merge(x, y_ssd, o_gla, w):
    t = x.shape[0]
    tm = _row_tile(t, 512)
    row = lambda width: pl.BlockSpec((tm, width), lambda i: (i, 0))
    return pl.pallas_call(
        _merge_kernel,
        grid=(t // tm,),
        in_specs=[row(D_MODEL), row(SSD_D_INNER), row(GLA_VAL_DIM), _const_spec((1, D_MODEL)),
                  _const_spec((D_MODEL, 2 * D_MODEL)), _const_spec((SSD_D_INNER, D_MODEL)),
                  _const_spec((GLA_VAL_DIM, D_MODEL)), _const_spec((D_MODEL, D_MODEL))],
        out_specs=row(D_MODEL),
        out_shape=jax.ShapeDtypeStruct((t, D_MODEL), F32),
        compiler_params=_cparams("parallel"),
        name="merge",
    )(x, y_ssd, o_gla, w["norm1_w"], w["w_merge"], w["w_ssd_out"], w["w_gla_out"], w["w_out"])


def _merge_exchange_pairs(n):
    pairs = []
    p = n // 2
    while p > 0:
        q, r, d = n // 2, 0, p
        while d > 0:
            pairs += [(i, i + d) for i in range(n - d) if (i & p) == r]
            d, q, r = q - p, q // 2, p
        p //= 2
    return pairs


_SORT16 = _merge_exchange_pairs(PEER_TOPK)
_BITONIC16 = [(i, i + d) for d in (8, 4, 2, 1) for i in range(PEER_TOPK) if (i & d) == 0]


def _exchange(v, ids, i, j):
    if ids is None:
        v[i], v[j] = jnp.maximum(v[i], v[j]), jnp.minimum(v[i], v[j])
    else:
        ge = v[i] >= v[j]
        v[i], v[j] = jnp.where(ge, v[i], v[j]), jnp.where(ge, v[j], v[i])
        ids[i], ids[j] = jnp.where(ge, ids[i], ids[j]), jnp.where(ge, ids[j], ids[i])


def _merge_top16(v, ids, w, wids):
    n, m = len(v), len(w)
    for i in range(n - m, n):
        o = n - 1 - i
        if ids is None:
            v[i] = jnp.maximum(v[i], w[o])
        else:
            ge = v[i] >= w[o]
            v[i] = jnp.where(ge, v[i], w[o])
            ids[i] = jnp.where(ge, ids[i], wids[o])
    for i, j in _BITONIC16:
        _exchange(v, ids, i, j)


def _top16_rows(x, with_ids):
    n = x.shape[0] // SUBLANES
    assert n == PEER_TOPK
    v = [x[SUBLANES * r:SUBLANES * (r + 1), :] for r in range(n)]
    ids = None
    if with_ids:
        row = lax.broadcasted_iota(jnp.int32, v[0].shape, 0).astype(F32)
        ids = [row + float(SUBLANES * r) for r in range(n)]
    for i, j in _SORT16:
        _exchange(v, ids, i, j)
    shift = SUBLANES // 2
    while shift:
        w = [pltpu.roll(a, shift, 0) for a in v]
        wids = [pltpu.roll(a, shift, 0) for a in ids] if with_ids else None
        _merge_top16(v, ids, w, wids)
        shift //= 2
    return v, ids


def _peer_kernel(x_ref, nw_ref, fnw_ref, wqt_ref, keys_ref, u_ref, v_ref, o_ref,
                 hn_ref, rw_ref, s2t_ref, svk_ref, sik_ref, stat_ref, g1k_ref, ids_ref, g1tm_ref,
                 kept8_ref, e8_ref, hid_ref, acc_ref):
    s = pl.program_id(1)
    ns = pl.num_programs(1)
    tb = x_ref.shape[0]
    nk = PEER_N_KEYS
    nj = PEER_HEADS * PEER_TOPK

    @pl.when(s == 0)
    def _select():
        acc_ref[...] = jnp.zeros(acc_ref.shape, F32)
        hid_ref[...] = jnp.zeros(hid_ref.shape, F32)
        hn_ref[...] = (_rms_scale(x_ref[...]) * nw_ref[...]).astype(BF16)
        stat_ref[...] = jnp.zeros(stat_ref.shape, F32)
        hn = hn_ref[...]
        for h in range(PEER_HEADS):
            for c in range(2):
                r0 = (h * 2 + c) * (PEER_DQ // 2)
                q_t = _dot_nt(wqt_ref[r0:r0 + PEER_DQ // 2, :], hn)
                sc_t = _dot(keys_ref[h * 2 + c], q_t.astype(BF16))
                vals, ids = _top16_rows(sc_t, c == 0)
                for k in range(PEER_TOPK):
                    svk_ref[c, k, h:h + 1, :] = vals[k][0:1, :]
                    if c == 0:
                        sik_ref[k, h:h + 1, :] = ids[k][0:1, :]
                if c == 1:
                    s2t_ref[h] = sc_t

        pairs = ([(0, k2) for k2 in range(PEER_TOPK)]
                 + [(k1, k2) for k1 in range(1, PEER_TOPK // 2) for k2 in range(PEER_TOPK // (k1 + 1))]
                 + [(k1, 0) for k1 in range(PEER_TOPK // 2, PEER_TOPK)])
        for g in range(tb // LANES):
            ls = slice(g * LANES, (g + 1) * LANES)
            a = [svk_ref[0, k, :, ls] for k in range(PEER_TOPK)]
            b = [svk_ref[1, k, :, ls] for k in range(PEER_TOPK)]
            best = [a[0] + b[k2] for k2 in range(PEER_TOPK)]
            for k1 in range(1, PEER_TOPK // 2):
                _merge_top16(best, None, [a[k1] + b[k2] for k2 in range(PEER_TOPK // (k1 + 1))], None)
            _merge_top16(best, None, [a[k1] + b[0] for k1 in range(PEER_TOPK // 2, PEER_TOPK)], None)
            tau, top = best[PEER_TOPK - 1], best[0]
            z = jnp.zeros_like(tau)
            for k1, k2 in pairs:
                cv = a[k1] + b[k2]
                z = z + jnp.where(cv >= tau, jnp.exp(cv - top), 0.0)
            stat_ref[0:PEER_HEADS, ls] = tau
            stat_ref[PEER_HEADS:2 * PEER_HEADS, ls] = b[0]
            inv_z = 1.0 / z
            for k in range(PEER_TOPK):
                g1k_ref[k, :, ls] = jnp.exp(a[k] - a[0]) * inv_z

        ids_ref[...] = sik_ref[...].reshape(nj, tb).T
        g1tm_ref[...] = g1k_ref[...].reshape(nj, tb).T
        def count_kept(h, carry):
            sc2 = s2t_ref[h]
            tau_t = stat_ref[pl.ds(h, 1), :]
            kept = jnp.zeros(sc2.shape, F32)
            for k1 in range(PEER_TOPK):
                kept = kept + jnp.where((sc2 + svk_ref[0, k1, pl.ds(h, 1), :]) >= tau_t, 1.0, 0.0)
            e2 = jnp.exp(sc2 - stat_ref[pl.ds(PEER_HEADS + h, 1), :])
            kept8_ref[pl.ds(h, tb, stride=PEER_HEADS), :] = kept.T
            e8_ref[pl.ds(h, tb, stride=PEER_HEADS), :] = e2.T
            return carry

        lax.fori_loop(0, PEER_HEADS, count_kept, 0)

        key1 = lax.broadcasted_iota(jnp.int32, (nk, nk), 0).astype(F32)

        def scatter(t, carry):
            hb = pl.multiple_of(t * PEER_HEADS, PEER_HEADS)
            kept8 = kept8_ref[pl.ds(hb, PEER_HEADS), :]
            e8 = e8_ref[pl.ds(hb, PEER_HEADS), :]
            rows = jnp.concatenate(
                [jnp.where(kept8 > float(k1), e8, 0.0) for k1 in range(PEER_TOPK)], axis=0).astype(BF16)
            place = jnp.where(key1 == ids_ref[pl.ds(t, 1), :], g1tm_ref[pl.ds(t, 1), :], 0.0)
            rw_ref[pl.ds(t * PEER_PITCH, nk), :] = _dot(place.astype(BF16), rows)
            return carry

        lax.fori_loop(0, tb, scatter, 0, unroll=PEER_SCATTER_UNROLL)

    prev = hid_ref[...]
    act = 0.5 * prev * (1.0 + lax.erf(prev * (2.0 ** -0.5)))
    slab0 = jnp.maximum(s - 1, 0) * PEER_SLABS_PER_STEP
    wts = jnp.concatenate(
        [rw_ref[pl.ds(slab0 + i, tb, stride=PEER_PITCH), :] for i in range(PEER_SLABS_PER_STEP)], axis=1)
    gated = (act * wts).astype(BF16)
    hn = hn_ref[...]
    half = PEER_STEP_EXPERTS // 2
    for c in range(2):
        hid_ref[:, c * half:(c + 1) * half] = _dot_nt(hn, u_ref[c * half:(c + 1) * half, :])
    half = D_MODEL // 2
    for c in range(2):
        acc_ref[:, c * half:(c + 1) * half] += _dot(gated, v_ref[:, c * half:(c + 1) * half])

    @pl.when(s == ns - 1)
    def _():
        o_ref[...] = _rms_scale(x_ref[...] + acc_ref[...]) * fnw_ref[...]


def peer_final(x, w):
    t = x.shape[0]
    tb = _row_tile(t, PEER_TOKEN_BLOCK)
    assert tb % LANES == 0
    nblk = PEER_N_EXPERTS // PEER_STEP_EXPERTS
    return pl.pallas_call(
        _peer_kernel,
        grid=(t // tb, nblk + 1),
        in_specs=[
            pl.BlockSpec((tb, D_MODEL), lambda i, s: (i, 0)),
            _const_spec((1, D_MODEL)),
            _const_spec((1, D_MODEL)),
            _const_spec((PEER_HEADS * PEER_DQ, D_MODEL)),
            _const_spec((2 * PEER_HEADS, PEER_N_KEYS, PEER_DQ // 2)),
            pl.BlockSpec((PEER_STEP_EXPERTS, D_MODEL), lambda i, s: (jnp.minimum(s, nblk - 1), 0)),
            pl.BlockSpec((PEER_STEP_EXPERTS, D_MODEL), lambda i, s: (jnp.maximum(s - 1, 0), 0)),
        ],
        out_specs=pl.BlockSpec((tb, D_MODEL), lambda i, s: (i, 0)),
        out_shape=jax.ShapeDtypeStruct((t, D_MODEL), F32),
        scratch_shapes=[
            pltpu.VMEM((tb, D_MODEL), BF16),
            pltpu.VMEM((tb * PEER_PITCH, PEER_N_KEYS), F32),
            pltpu.VMEM((PEER_HEADS, PEER_N_KEYS, tb), F32),
            pltpu.VMEM((2, PEER_TOPK, PEER_HEADS, tb), F32),
            pltpu.VMEM((PEER_TOPK, PEER_HEADS, tb), F32),
            pltpu.VMEM((LANES, tb), F32),
            pltpu.VMEM((PEER_TOPK, PEER_HEADS, tb), F32),
            pltpu.VMEM((tb, PEER_HEADS * PEER_TOPK), F32),
            pltpu.VMEM((tb, PEER_HEADS * PEER_TOPK), F32),
            pltpu.VMEM((tb * PEER_HEADS, PEER_N_KEYS), F32),
            pltpu.VMEM((tb * PEER_HEADS, PEER_N_KEYS), F32),
            pltpu.VMEM((tb, PEER_STEP_EXPERTS), F32),
            pltpu.VMEM((tb, D_MODEL), F32),
        ],
        compiler_params=_cparams("parallel", "arbitrary"),
        name="peer_final",
    )(x, w["norm2_w"], w["final_norm_w"], w["peer_wq_t"], w["peer_keys"], w["peer_u"], w["peer_v"])


def _prepare_weights(norm1_w, w_in, conv_w, conv_b, dt_bias, a_log, d_skip, ssd_norm_w, w_ssd_out,
                     gla_gate_up, gla_gate_b, gla_norm_w, w_gla_out, w_out, norm2_w, peer_w_q,
                     peer_sub_keys, peer_u, peer_v, final_norm_w):
    offs = [0]
    for sz in _IN_SIZES:
        offs.append(offs[-1] + sz)
    z0, xbc0, dt0, q0, _, _, _, glr0, mg0, end = offs
    small = jnp.concatenate(
        [w_in[:, dt0:q0], w_in[:, glr0:mg0],
         jnp.zeros((D_MODEL, LANES - SSD_HEADS - GLA_GATE_RANK), w_in.dtype)], axis=1)
    w_proj = jnp.concatenate([w_in[:, z0:dt0], w_in[:, q0:glr0], small], axis=1).astype(BF16)
    assert w_proj.shape[1] == IN_COLS

    def lane_pad(v):
        return jnp.pad(v, (0, LANES - v.shape[0])).reshape(1, LANES)

    head_of_col = jnp.arange(SSD_D_INNER) // SSD_HEAD_DIM
    expand = (jnp.arange(LANES)[:, None] == head_of_col[None, :]).astype(BF16)
    gate_up = jnp.zeros((LANES, GLA_KEY_DIM), F32).at[SMALL_GLR:SMALL_GLR + GLA_GATE_RANK].set(gla_gate_up)
    keys = jnp.transpose(peer_sub_keys, (1, 0, 2, 3)).reshape(2 * PEER_HEADS, PEER_N_KEYS, PEER_DQ // 2)
    return {
        "norm1_w": norm1_w.reshape(1, D_MODEL),
        "w_proj": w_proj,
        "w_merge": w_in[:, mg0:end].astype(BF16),
        "conv_w": conv_w,
        "conv_b": conv_b.reshape(1, SSD_CONV_DIM),
        "dt_bias": lane_pad(dt_bias),
        "a_log": lane_pad(a_log),
        "d_exp": jnp.repeat(d_skip, SSD_HEAD_DIM).reshape(1, SSD_D_INNER),
        "ssd_norm_w": ssd_norm_w.reshape(1, SSD_D_INNER),
        "expand": expand,
        "w_ssd_out": w_ssd_out.astype(BF16),
        "gate_up": gate_up.astype(BF16),
        "gate_b": gla_gate_b.reshape(1, GLA_KEY_DIM),
        "gla_norm_w": jnp.tile(gla_norm_w, GLA_HEADS).reshape(1, GLA_VAL_DIM),
        "w_gla_out": w_gla_out.astype(BF16),
        "w_out": w_out.astype(BF16),
        "norm2_w": norm2_w.reshape(1, D_MODEL),
        "peer_wq_t": peer_w_q.T.astype(BF16),
        "peer_keys": keys.astype(BF16),
        "peer_u": peer_u.astype(BF16),
        "peer_v": peer_v.astype(BF16),
        "final_norm_w": final_norm_w.reshape(1, D_MODEL),
    }


def _prompt_path(x_prompt, meta_tokens, w):
    b, seq, _ = x_prompt.shape
    assert seq % SSD_CHUNK == 0
    meta = jnp.broadcast_to(meta_tokens.astype(x_prompt.dtype)[None], (b, N_META, D_MODEL))
    xp = jnp.concatenate([jnp.zeros((b, PROMPT_ZERO_ROWS, D_MODEL), x_prompt.dtype), meta, x_prompt], axis=1)
    rows = xp.shape[1]
    proj = norm_matmul(xp.reshape(b * rows, D_MODEL), w["norm1_w"], w["w_proj"], IN_COLS // 5)
    y_ssd, st_ssd, conv_tail = ssd_prompt(proj, b, rows // SSD_CHUNK, PROMPT_ZERO_ROWS, PROMPT_SKIP_ROWS, w)
    o_gla, st_gla = gla_prompt(proj, b, rows // GLA_CHUNK, PROMPT_ZERO_ROWS, PROMPT_SKIP_ROWS, w)
    x1 = merge(x_prompt.reshape(b * seq, D_MODEL), y_ssd.reshape(b * seq, SSD_D_INNER),
               o_gla.reshape(b * seq, GLA_VAL_DIM), w)
    y = peer_final(x1, w).reshape(b, seq, D_MODEL)
    return (y,
            st_ssd.reshape(1, b, SSD_HEADS, SSD_HEAD_DIM, SSD_D_STATE),
            conv_tail[:, SUBLANES - (SSD_CONV - 1):, :][None],
            st_gla.reshape(1, b, GLA_HEADS, GLA_DK, GLA_DV))


def _sample_path(x_sample, state_ssd, state_conv, state_gla, w):
    b = x_sample.shape[0]
    x = x_sample.reshape(b, D_MODEL)
    proj = norm_matmul(x, w["norm1_w"], w["w_proj"], IN_COLS // 5)
    xact, xdt, dec, bm, cm, conv_new = ssd_step_pre(
        proj, state_conv.reshape(b, (SSD_CONV - 1) * SSD_CONV_DIM), w)
    st_ssd, y = ssd_step_state(state_ssd.reshape(b, SSD_D_INNER, SSD_D_STATE), xdt, dec, bm, cm)
    y_ssd = ssd_step_post(y, xact, proj, w)
    st_gla, o_gla = gla_step(state_gla.reshape(b, GLA_KEY_DIM, GLA_DV), proj, w)
    x1 = merge(x, y_ssd, o_gla, w)
    y = peer_final(x1, w).reshape(b, 1, D_MODEL)
    return (y,
            st_ssd.reshape(1, b, SSD_HEADS, SSD_HEAD_DIM, SSD_D_STATE),
            conv_new.reshape(1, b, SSD_CONV - 1, SSD_CONV_DIM),
            st_gla.reshape(1, b, GLA_HEADS, GLA_DK, GLA_DV))


def kernel(x_prompt, x_sample, state_ssd, state_conv, state_gla, meta_tokens, norm1_w, w_in, conv_w, conv_b,
           dt_bias, a_log, d_skip, ssd_norm_w, w_ssd_out, gla_gate_up, gla_gate_b, gla_norm_w, w_gla_out,
           w_out, norm2_w, peer_w_q, peer_sub_keys, peer_u, peer_v, final_norm_w):
    layer = (norm1_w, w_in, conv_w, conv_b, dt_bias, a_log, d_skip, ssd_norm_w, w_ssd_out, gla_gate_up,
             gla_gate_b, gla_norm_w, w_gla_out, w_out, norm2_w, peer_w_q, peer_sub_keys, peer_u, peer_v)
    assert all(p.shape[0] == 1 for p in layer), "single-layer step"
    w = _prepare_weights(*[p[0] for p in layer], final_norm_w)
    yp, p_ssd, p_conv, p_gla = _prompt_path(x_prompt, meta_tokens, w)
    ys, s_ssd, s_conv, s_gla = _sample_path(x_sample, state_ssd[0], state_conv[0], state_gla[0], w)
    return (yp, ys, p_ssd, p_conv, p_gla, s_ssd, s_conv, s_gla)
```

```python
import functools

import jax
import jax.numpy as jnp
from jax import lax
from jax.experimental import pallas as pl
from jax.experimental.pallas import tpu as pltpu

F32 = jnp.float32
BF16 = jnp.bfloat16

LANES = 128
SUBLANES = 8
VMEM_LIMIT_BYTES = 60 * 1024 * 1024

D_MODEL = 1024
N_META = 16
EPS = 1e-6
SSD_D_INNER = 2 * D_MODEL
SSD_HEAD_DIM = 64
SSD_HEADS = SSD_D_INNER // SSD_HEAD_DIM
SSD_GROUPS = 4
SSD_HPG = SSD_HEADS // SSD_GROUPS
SSD_D_STATE = 128
SSD_CONV = 4
SSD_CHUNK = 128
SSD_BC = 2 * SSD_GROUPS * SSD_D_STATE
SSD_CONV_DIM = SSD_D_INNER + SSD_BC
SSD_GROUP_W = SSD_HPG * SSD_HEAD_DIM
GLA_HEADS = 4
GLA_KEY_DIM = D_MODEL // 2
GLA_VAL_DIM = D_MODEL
GLA_DK = GLA_KEY_DIM // GLA_HEADS
GLA_DV = GLA_VAL_DIM // GLA_HEADS
GLA_GATE_RANK = 16
GLA_GATE_NORMALIZER = 16.0
GLA_CHUNK = 64
GLA_STEP_CHUNKS = 2
PEER_HEADS = 8
PEER_N_KEYS = 128
PEER_N_EXPERTS = PEER_N_KEYS * PEER_N_KEYS
PEER_DQ = 256
PEER_TOPK = 16
PEER_SLABS_PER_STEP = 16
PEER_STEP_EXPERTS = PEER_SLABS_PER_STEP * PEER_N_KEYS
PEER_SCATTER_UNROLL = 16
PEER_PITCH = PEER_N_KEYS + 4
PEER_TOKEN_BLOCK = 256

_IN_SIZES = (SSD_D_INNER, SSD_CONV_DIM, SSD_HEADS, GLA_KEY_DIM, GLA_KEY_DIM, GLA_VAL_DIM,
             GLA_VAL_DIM, GLA_GATE_RANK, 2 * D_MODEL)
COL_Z = 0
COL_XS = COL_Z + SSD_D_INNER
COL_BC = COL_XS + SSD_D_INNER
COL_Q = COL_BC + SSD_BC
COL_K = COL_Q + GLA_KEY_DIM
COL_V = COL_K + GLA_KEY_DIM
COL_G = COL_V + GLA_VAL_DIM
COL_SMALL = COL_G + GLA_VAL_DIM
IN_COLS = COL_SMALL + LANES
SMALL_GLR = SSD_HEADS
PROMPT_ZERO_ROWS = SSD_CHUNK - N_META
PROMPT_SKIP_ROWS = PROMPT_ZERO_ROWS + N_META


def _cparams(*sem):
    return pltpu.CompilerParams(dimension_semantics=sem, vmem_limit_bytes=VMEM_LIMIT_BYTES)


def _const_spec(shape):
    nd = len(shape)
    return pl.BlockSpec(shape, lambda *_: (0,) * nd, pipeline_mode=pl.Buffered(1))


def _row_tile(t, cap):
    tm = cap
    while t % tm:
        tm //= 2
    return tm


def _split3(x):
    hi = x.astype(BF16)
    r1 = x - hi.astype(F32)
    mid = r1.astype(BF16)
    lo = (r1 - mid.astype(F32)).astype(BF16)
    return hi, mid, lo


def _dot(a, b):
    return jnp.dot(a, b, preferred_element_type=F32)


def _dot_nt(a, b):
    return lax.dot_general(a, b, (((1,), (1,)), ((), ())), preferred_element_type=F32)


def _sel_right(sel01, x):
    hi, mid, lo = _split3(x)
    return _dot(hi, sel01) + _dot(mid, sel01) + _dot(lo, sel01)


def _sel_left(sel01, x):
    hi, mid, lo = _split3(x)
    return _dot(sel01, hi) + _dot(sel01, mid) + _dot(sel01, lo)


def _rms_scale(x):
    return x * lax.rsqrt(jnp.mean(x * x, axis=-1, keepdims=True) + EPS)


def _tril(n):
    r = lax.broadcasted_iota(jnp.int32, (n, n), 0)
    c = lax.broadcasted_iota(jnp.int32, (n, n), 1)
    return r >= c


def _norm_matmul_kernel(x_ref, nw_ref, w_ref, o_ref, xn_ref):
    @pl.when(pl.program_id(1) == 0)
    def _():
        xn_ref[...] = (_rms_scale(x_ref[...]) * nw_ref[...]).astype(BF16)

    o_ref[...] = _dot(xn_ref[...], w_ref[...])


def norm_matmul(x, norm_w, w, tn):
    t, d = x.shape
    n = w.shape[1]
    tm = _row_tile(t, 1024)
    assert n % tn == 0
    return pl.pallas_call(
        _norm_matmul_kernel,
        grid=(t // tm, n // tn),
        in_specs=[
            pl.BlockSpec((tm, d), lambda i, j: (i, 0)),
            pl.BlockSpec((1, d), lambda i, j: (0, 0)),
            pl.BlockSpec((d, tn), lambda i, j: (0, j)),
        ],
        out_specs=pl.BlockSpec((tm, tn), lambda i, j: (i, j)),
        out_shape=jax.ShapeDtypeStruct((t, n), F32),
        scratch_shapes=[pltpu.VMEM((tm, d), BF16)],
        compiler_params=_cparams("parallel", "arbitrary"),
        name="norm_in_proj",
    )(x, norm_w, w)


def _ssd_dt(small, dt_bias_ref, valid):
    lane = lax.broadcasted_iota(jnp.int32, small.shape, 1)
    dt = jax.nn.softplus(small + dt_bias_ref[...])
    keep = lane < SSD_HEADS
    if valid is not None:
        keep = jnp.logical_and(keep, valid)
    return jnp.where(keep, dt, 0.0)


def _ssd_gated_norm(y, z, norm_w_ref):
    y = y * jax.nn.silu(z)
    parts = []
    for g in range(SSD_GROUPS):
        parts.append(_rms_scale(y[:, g * SSD_GROUP_W:(g + 1) * SSD_GROUP_W]))
    return jnp.concatenate(parts, axis=1) * norm_w_ref[...]


def _ssd_prompt_kernel(z_ref, xs_ref, bc_ref, small_ref, conv_w_ref, conv_b_ref, dt_bias_ref,
                       a_log_ref, d_exp_ref, norm_w_ref, expand_ref,
                       y_ref, state_ref, convtail_ref,
                       hist_ref, st_ref, yacc_ref, *, zero_rows):
    c = pl.program_id(1)
    nc = pl.num_programs(1)
    q = SSD_CHUNK

    @pl.when(c == 0)
    def _():
        hist_ref[0:SUBLANES, :] = jnp.zeros((SUBLANES, SSD_CONV_DIM), F32)
        st_ref[...] = jnp.zeros(st_ref.shape, F32)

    hist_ref[SUBLANES:SUBLANES + q, 0:SSD_D_INNER] = xs_ref[...]
    hist_ref[SUBLANES:SUBLANES + q, SSD_D_INNER:SSD_CONV_DIM] = bc_ref[...]
    conv = conv_b_ref[...]
    for j in range(SSD_CONV):
        start = SUBLANES - (SSD_CONV - 1) + j
        conv = conv + conv_w_ref[j:j + 1, :] * hist_ref[start:start + q, :]
    xbc = jax.nn.silu(conv)
    tail = hist_ref[q:q + SUBLANES, :]
    hist_ref[0:SUBLANES, :] = tail
    convtail_ref[0] = tail

    xs = xbc[:, 0:SSD_D_INNER]
    bm = xbc[:, SSD_D_INNER:SSD_D_INNER + SSD_GROUPS * SSD_D_STATE].astype(BF16)
    cm = xbc[:, SSD_D_INNER + SSD_GROUPS * SSD_D_STATE:].astype(BF16)

    row = lax.broadcasted_iota(jnp.int32, (q, LANES), 0) + c * q
    dt = _ssd_dt(small_ref[...], dt_bias_ref, row >= zero_rows)
    da = dt * (-jnp.exp(a_log_ref[...]))
    tril = _tril(q)
    acum = _sel_left(tril.astype(BF16), da)
    acum_t = acum.T
    expand = expand_ref[...]
    dt_exp = _sel_right(expand, dt)
    acum_exp = _sel_right(expand, acum)
    acum_last = acum_exp[q - 1:q, :]
    xdt = xs * dt_exp
    xdt_bf = xdt.astype(BF16)
    x_end = (xdt * jnp.exp(acum_last - acum_exp)).astype(BF16)

    lane_lo = lax.broadcasted_iota(jnp.int32, (q, LANES), 1) < SSD_HEAD_DIM
    for g in range(SSD_GROUPS):
        bg = bm[:, g * SSD_D_STATE:(g + 1) * SSD_D_STATE]
        cg = cm[:, g * SSD_D_STATE:(g + 1) * SSD_D_STATE]
        cb = _dot_nt(cg, bg)
        for hp in range(SSD_HPG // 2):
            lo = (g * SSD_HPG + 2 * hp) * SSD_HEAD_DIM
            pair = []
            for k in range(2):
                h = g * SSD_HPG + 2 * hp + k
                diff = acum[:, h:h + 1] - acum_t[h:h + 1, :]
                seg = jnp.exp(jnp.where(tril, diff, -jnp.inf))
                pair.append(_dot((cb * seg).astype(BF16), xdt_bf[:, lo:lo + LANES]))
            yacc_ref[:, lo:lo + LANES] = jnp.where(lane_lo, pair[0], pair[1])

        gs = slice(g * SSD_GROUP_W, (g + 1) * SSD_GROUP_W)
        st_g = st_ref[:, gs]
        y_off = _dot(cg, st_g.astype(BF16)) * jnp.exp(acum_exp[:, gs])
        yacc_ref[:, gs] = yacc_ref[:, gs] + y_off
        st_ref[:, gs] = st_g * jnp.exp(acum_last[:, gs]) + _dot(bg.T, x_end[:, gs])

    y = yacc_ref[...] + d_exp_ref[...] * xs
    y_ref[0] = _ssd_gated_norm(y, z_ref[...], norm_w_ref).astype(BF16)

    @pl.when(c == nc - 1)
    def _():
        state_ref[0] = st_ref[...].T


def ssd_prompt(proj, batch, n_chunks, zero_rows, skip_rows, w):
    q = SSD_CHUNK
    assert skip_rows % q == 0
    first_real = skip_rows // q
    n_out = n_chunks - first_real

    def rows(b, c):
        return b * n_chunks + c

    return pl.pallas_call(
        functools.partial(_ssd_prompt_kernel, zero_rows=zero_rows),
        grid=(batch, n_chunks),
        in_specs=[
            pl.BlockSpec((q, SSD_D_INNER), lambda b, c: (rows(b, c), COL_Z // SSD_D_INNER)),
            pl.BlockSpec((q, SSD_D_INNER), lambda b, c: (rows(b, c), COL_XS // SSD_D_INNER)),
            pl.BlockSpec((q, SSD_BC), lambda b, c: (rows(b, c), COL_BC // SSD_BC)),
            pl.BlockSpec((q, LANES), lambda b, c: (rows(b, c), COL_SMALL // LANES)),
            _const_spec((SSD_CONV, SSD_CONV_DIM)),
            _const_spec((1, SSD_CONV_DIM)),
            _const_spec((1, LANES)),
            _const_spec((1, LANES)),
            _const_spec((1, SSD_D_INNER)),
            _const_spec((1, SSD_D_INNER)),
            _const_spec((LANES, SSD_D_INNER)),
        ],
        out_specs=[
            pl.BlockSpec((1, q, SSD_D_INNER), lambda b, c: (b, jnp.maximum(c - first_real, 0), 0)),
            pl.BlockSpec((1, SSD_D_INNER, SSD_D_STATE), lambda b, c: (b, 0, 0)),
            pl.BlockSpec((1, SUBLANES, SSD_CONV_DIM), lambda b, c: (b, 0, 0)),
        ],
        out_shape=[
            jax.ShapeDtypeStruct((batch, n_out * q, SSD_D_INNER), BF16),
            jax.ShapeDtypeStruct((batch, SSD_D_INNER, SSD_D_STATE), F32),
            jax.ShapeDtypeStruct((batch, SUBLANES, SSD_CONV_DIM), F32),
        ],
        scratch_shapes=[
            pltpu.VMEM((SUBLANES + q, SSD_CONV_DIM), F32),
            pltpu.VMEM((SSD_D_STATE, SSD_D_INNER), F32),
            pltpu.VMEM((q, SSD_D_INNER), F32),
        ],
        compiler_params=_cparams("parallel", "arbitrary"),
        name="ssd_prompt",
    )(proj, proj, proj, proj, w["conv_w"], w["conv_b"], w["dt_bias"], w["a_log"], w["d_exp"],
      w["ssd_norm_w"], w["expand"])


def _ssd_step_pre_kernel(xs_ref, bc_ref, small_ref, c0_ref, c1_ref, c2_ref, conv_w_ref, conv_b_ref,
                         dt_bias_ref, a_log_ref, expand_ref,
                         xact_ref, xdt_ref, decay_ref, bm_ref, cm_ref, convnew_ref):
    x = jnp.concatenate([xs_ref[...], bc_ref[...]], axis=1)
    hist = (c0_ref[...], c1_ref[...], c2_ref[...], x)
    conv = conv_b_ref[...]
    for j in range(SSD_CONV):
        conv = conv + conv_w_ref[j:j + 1, :] * hist[j]
    xbc = jax.nn.silu(conv)
    xs = xbc[:, 0:SSD_D_INNER]
    dt = _ssd_dt(small_ref[...], dt_bias_ref, None)
    da = dt * (-jnp.exp(a_log_ref[...]))
    expand = expand_ref[...]
    xact_ref[...] = xs
    xdt_ref[...] = xs * _sel_right(expand, dt)
    decay_ref[...] = jnp.exp(da)
    bm_ref[...] = xbc[:, SSD_D_INNER:SSD_D_INNER + SSD_GROUPS * SSD_D_STATE]
    cm_ref[...] = xbc[:, SSD_D_INNER + SSD_GROUPS * SSD_D_STATE:]
    convnew_ref[:, 0:SSD_CONV_DIM] = hist[1]
    convnew_ref[:, SSD_CONV_DIM:2 * SSD_CONV_DIM] = hist[2]
    convnew_ref[:, 2 * SSD_CONV_DIM:] = x


def ssd_step_pre(proj, conv_state, w):
    t = proj.shape[0]
    tm = _row_tile(t, 128)
    gw = SSD_GROUPS * SSD_D_STATE
    row = lambda width, col: pl.BlockSpec((tm, width), lambda i: (i, col))
    return pl.pallas_call(
        _ssd_step_pre_kernel,
        grid=(t // tm,),
        in_specs=[
            row(SSD_D_INNER, COL_XS // SSD_D_INNER),
            row(SSD_BC, COL_BC // SSD_BC),
            row(LANES, COL_SMALL // LANES),
            row(SSD_CONV_DIM, 0), row(SSD_CONV_DIM, 1), row(SSD_CONV_DIM, 2),
            _const_spec((SSD_CONV, SSD_CONV_DIM)),
            _const_spec((1, SSD_CONV_DIM)),
            _const_spec((1, LANES)),
            _const_spec((1, LANES)),
            _const_spec((LANES, SSD_D_INNER)),
        ],
        out_specs=[row(SSD_D_INNER, 0), row(SSD_D_INNER, 0), row(LANES, 0), row(gw, 0),
                   row(gw, 0), row((SSD_CONV - 1) * SSD_CONV_DIM, 0)],
        out_shape=[
            jax.ShapeDtypeStruct((t, SSD_D_INNER), F32),
            jax.ShapeDtypeStruct((t, SSD_D_INNER), F32),
            jax.ShapeDtypeStruct((t, LANES), F32),
            jax.ShapeDtypeStruct((t, gw), F32),
            jax.ShapeDtypeStruct((t, gw), F32),
            jax.ShapeDtypeStruct((t, (SSD_CONV - 1) * SSD_CONV_DIM), F32),
        ],
        compiler_params=_cparams("parallel"),
        name="ssd_step_pre",
    )(proj, proj, proj, conv_state, conv_state, conv_state, w["conv_w"], w["conv_b"], w["dt_bias"],
      w["a_log"], w["expand"])


def _dot_tn(a, b):
    return lax.dot_general(a, b, (((0,), (0,)), ((), ())), preferred_element_type=F32)


def _own_row(j):
    return lax.broadcasted_iota(jnp.int32, (SUBLANES, 1), 0) == j


def _ssd_step_state_kernel(s_ref, xdt_ref, dec_ref, bm_ref, cm_ref, snew_ref, y_ref):
    j = pl.program_id(1)
    own = _own_row(j)
    dec = dec_ref[pl.ds(j, 1), :]

    @pl.when(j == 0)
    def _():
        y_ref[...] = jnp.zeros(y_ref.shape, F32)

    def group_rows(x):
        return jnp.concatenate(
            [x[:, g * SSD_D_STATE:(g + 1) * SSD_D_STATE] for g in range(SSD_GROUPS)], axis=0)

    group_of_lane = lax.broadcasted_iota(jnp.int32, (SUBLANES, SSD_D_INNER), 1) // SSD_GROUP_W
    xdt = jnp.where(own, xdt_ref[...], 0.0)
    x_rows = jnp.concatenate(
        [jnp.where(group_of_lane == g, xdt, 0.0) for g in range(SSD_GROUPS)], axis=0).astype(BF16)
    b_rows = group_rows(bm_ref[...]).astype(BF16)
    c_rows = group_rows(jnp.where(own, cm_ref[...], 0.0)).astype(BF16)
    decayed = [s_ref[0, h * SSD_HEAD_DIM:(h + 1) * SSD_HEAD_DIM, :] * dec[:, h:h + 1]
               for h in range(SSD_HEADS)]
    s_new = jnp.concatenate(decayed, axis=0) + _dot_tn(x_rows, b_rows)
    snew_ref[0] = s_new
    y_rows = _dot_nt(c_rows, s_new.astype(BF16))
    y_ref[...] += jnp.concatenate(
        [y_rows[g * SUBLANES:(g + 1) * SUBLANES, g * SSD_GROUP_W:(g + 1) * SSD_GROUP_W]
         for g in range(SSD_GROUPS)], axis=1)


def ssd_step_state(state, xdt, dec, bm, cm):
    t = state.shape[0]
    assert t % SUBLANES == 0
    gw = SSD_GROUPS * SSD_D_STATE
    rows = lambda width: pl.BlockSpec((SUBLANES, width), lambda i, j: (i, 0))
    st = pl.BlockSpec((1, SSD_D_INNER, SSD_D_STATE), lambda i, j: (i * SUBLANES + j, 0, 0))
    return pl.pallas_call(
        _ssd_step_state_kernel,
        grid=(t // SUBLANES, SUBLANES),
        in_specs=[st, rows(SSD_D_INNER), rows(LANES), rows(gw), rows(gw)],
        out_specs=[st, rows(SSD_D_INNER)],
        out_shape=[
            jax.ShapeDtypeStruct((t, SSD_D_INNER, SSD_D_STATE), F32),
            jax.ShapeDtypeStruct((t, SSD_D_INNER), F32),
        ],
        compiler_params=_cparams("parallel", "arbitrary"),
        name="ssd_step_state",
    )(state, xdt, dec, bm, cm)


def _ssd_step_post_kernel(y_ref, xact_ref, z_ref, d_exp_ref, norm_w_ref, o_ref):
    y = y_ref[...] + d_exp_ref[...] * xact_ref[...]
    o_ref[...] = _ssd_gated_norm(y, z_ref[...], norm_w_ref).astype(BF16)


def ssd_step_post(y, xact, proj, w):
    t = y.shape[0]
    tm = _row_tile(t, 128)
    row = lambda col: pl.BlockSpec((tm, SSD_D_INNER), lambda i: (i, col))
    return pl.pallas_call(
        _ssd_step_post_kernel,
        grid=(t // tm,),
        in_specs=[row(0), row(0), row(COL_Z // SSD_D_INNER), _const_spec((1, SSD_D_INNER)),
                  _const_spec((1, SSD_D_INNER))],
        out_specs=row(0),
        out_shape=jax.ShapeDtypeStruct((t, SSD_D_INNER), BF16),
        compiler_params=_cparams("parallel"),
        name="ssd_step_post",
    )(y, xact, proj, w["d_exp"], w["ssd_norm_w"])


def _gla_gate(small, gate_up_ref, gate_b_ref):
    pre = _dot(small.astype(BF16), gate_up_ref[...]) + gate_b_ref[...]
    return jax.nn.log_sigmoid(pre) / GLA_GATE_NORMALIZER


def _gla_out_norm(o, g, norm_w_ref):
    parts = []
    for h in range(GLA_HEADS):
        parts.append(_rms_scale(o[:, h * GLA_DV:(h + 1) * GLA_DV]))
    return jnp.concatenate(parts, axis=1) * norm_w_ref[...] * jax.nn.silu(g)


def _gla_prompt_kernel(q_ref, k_ref, v_ref, g_ref, small_ref, gate_up_ref, gate_b_ref, norm_w_ref,
                       o_ref, state_ref, s_ref, *, zero_rows):
    c = pl.program_id(1)
    nc = pl.num_programs(1)
    q = GLA_CHUNK
    rows = q * GLA_STEP_CHUNKS

    @pl.when(c == 0)
    def _():
        s_ref[...] = jnp.zeros(s_ref.shape, F32)

    valid = (lax.broadcasted_iota(jnp.int32, (rows, GLA_KEY_DIM), 0) + c * rows) >= zero_rows
    gk = jnp.where(valid, _gla_gate(small_ref[...], gate_up_ref, gate_b_ref), 0.0)
    kk = jnp.where(valid, k_ref[...], 0.0)
    r = lax.broadcasted_iota(jnp.int32, (rows, rows), 0)
    cc = lax.broadcasted_iota(jnp.int32, (rows, rows), 1)
    chunk_tril = jnp.logical_and(r >= cc, r // q == cc // q)
    bcum = _sel_left(chunk_tril.astype(BF16), gk)
    lasts = [bcum[(i + 1) * q - 1:(i + 1) * q, :] for i in range(GLA_STEP_CHUNKS)]
    last_rows = jnp.concatenate([jnp.broadcast_to(x, (q, GLA_KEY_DIM)) for x in lasts], axis=0)
    qe = q_ref[...] * jnp.exp(bcum) * (GLA_DK ** -0.5)
    ke = kk * jnp.exp(-bcum)
    kend = kk * jnp.exp(last_rows - bcum)
    v_all = v_ref[...]
    vv = v_all.astype(BF16)
    chunk_of_row = lax.broadcasted_iota(jnp.int32, (rows, GLA_VAL_DIM), 0) // q
    v_chunk = [jnp.where(chunk_of_row == i, v_all, 0.0).astype(BF16) for i in range(GLA_STEP_CHUNKS)]
    outs = [[] for _ in range(GLA_STEP_CHUNKS)]
    for h in range(GLA_HEADS):
        ks = slice(h * GLA_DK, (h + 1) * GLA_DK)
        vs = slice(h * GLA_DV, (h + 1) * GLA_DV)
        qe_h = qe[:, ks].astype(BF16)
        att = jnp.where(chunk_tril, _dot_nt(qe_h, ke[:, ks].astype(BF16)), 0.0)
        o_intra = _dot(att.astype(BF16), vv[:, vs])
        kend_t = kend[:, ks].T.astype(BF16)
        s_h = s_ref[ks, :]
        for i in range(GLA_STEP_CHUNKS):
            rs = slice(i * q, (i + 1) * q)
            outs[i].append(o_intra[rs, :] + _dot(qe_h[rs, :], s_h.astype(BF16)))
            decay_col = jnp.broadcast_to(jnp.exp(lasts[i][:, ks]), (GLA_DK, GLA_DK)).T
            s_h = (jnp.concatenate([decay_col] * (GLA_DV // GLA_DK), axis=1) * s_h
                   + _dot(kend_t, v_chunk[i][:, vs]))
        s_ref[ks, :] = s_h
    o = jnp.concatenate([jnp.concatenate(o_i, axis=1) for o_i in outs], axis=0)
    o_ref[0] = _gla_out_norm(o, g_ref[...], norm_w_ref).astype(BF16)

    @pl.when(c == nc - 1)
    def _():
        state_ref[0] = s_ref[...]


def gla_prompt(proj, batch, n_rows, zero_rows, skip_rows, w):
    rows = GLA_CHUNK * GLA_STEP_CHUNKS
    assert skip_rows % rows == 0 and n_rows % rows == 0
    n_steps = n_rows // rows
    first_real = skip_rows // rows
    n_out = n_steps - first_real

    def blk(b, c):
        return b * n_steps + c

    return pl.pallas_call(
        functools.partial(_gla_prompt_kernel, zero_rows=zero_rows),
        grid=(batch, n_steps),
        in_specs=[
            pl.BlockSpec((rows, GLA_KEY_DIM), lambda b, c: (blk(b, c), COL_Q // GLA_KEY_DIM)),
            pl.BlockSpec((rows, GLA_KEY_DIM), lambda b, c: (blk(b, c), COL_K // GLA_KEY_DIM)),
            pl.BlockSpec((rows, GLA_VAL_DIM), lambda b, c: (blk(b, c), COL_V // GLA_VAL_DIM)),
            pl.BlockSpec((rows, GLA_VAL_DIM), lambda b, c: (blk(b, c), COL_G // GLA_VAL_DIM)),
            pl.BlockSpec((rows, LANES), lambda b, c: (blk(b, c), COL_SMALL // LANES)),
            _const_spec((LANES, GLA_KEY_DIM)),
            _const_spec((1, GLA_KEY_DIM)),
            _const_spec((1, GLA_VAL_DIM)),
        ],
        out_specs=[
            pl.BlockSpec((1, rows, GLA_VAL_DIM), lambda b, c: (b, jnp.maximum(c - first_real, 0), 0)),
            pl.BlockSpec((1, GLA_KEY_DIM, GLA_DV), lambda b, c: (b, 0, 0)),
        ],
        out_shape=[
            jax.ShapeDtypeStruct((batch, n_out * rows, GLA_VAL_DIM), BF16),
            jax.ShapeDtypeStruct((batch, GLA_KEY_DIM, GLA_DV), F32),
        ],
        scratch_shapes=[pltpu.VMEM((GLA_KEY_DIM, GLA_DV), F32)],
        compiler_params=_cparams("parallel", "arbitrary"),
        name="gla_prompt",
    )(proj, proj, proj, proj, proj, w["gate_up"], w["gate_b"], w["gla_norm_w"])


def _gla_step_kernel(s_ref, q_ref, k_ref, v_ref, g_ref, small_ref, gate_up_ref, gate_b_ref, norm_w_ref,
                     snew_ref, o_ref, dec_ref, acc_ref):
    j = pl.program_id(1)

    @pl.when(j == 0)
    def _():
        dec_ref[...] = jnp.exp(_gla_gate(small_ref[...], gate_up_ref, gate_b_ref))
        acc_ref[...] = jnp.zeros(acc_ref.shape, F32)

    own = _own_row(j)
    head_of_lane = lax.broadcasted_iota(jnp.int32, (SUBLANES, GLA_KEY_DIM), 1) // GLA_DK

    def per_head_rows(x):
        return jnp.concatenate([jnp.where(head_of_lane == h, x, 0.0) for h in range(GLA_HEADS)], axis=0)

    k_rows = per_head_rows(jnp.where(own, k_ref[...], 0.0)).astype(BF16)
    q_rows = per_head_rows(jnp.where(own, q_ref[...] * (GLA_DK ** -0.5), 0.0)).astype(BF16)
    vv = v_ref[...]
    v_rows = jnp.concatenate(
        [vv[:, h * GLA_DV:(h + 1) * GLA_DV] for h in range(GLA_HEADS)], axis=0).astype(BF16)
    pieces = [p.astype(F32) for p in _split3(jnp.where(own, dec_ref[...], 0.0))]
    d_rows = jnp.concatenate(pieces + [jnp.zeros_like(pieces[0])], axis=0).astype(BF16)
    decay = _dot_tn(d_rows, jnp.ones((d_rows.shape[0], GLA_DV), BF16))
    s_new = s_ref[0] * decay + _dot_tn(k_rows, v_rows)
    snew_ref[0] = s_new
    o_rows = _dot(q_rows, s_new.astype(BF16))
    acc_ref[...] += jnp.concatenate(
        [o_rows[h * SUBLANES:(h + 1) * SUBLANES, :] for h in range(GLA_HEADS)], axis=1)

    @pl.when(j == pl.num_programs(1) - 1)
    def _():
        o_ref[...] = _gla_out_norm(acc_ref[...], g_ref[...], norm_w_ref)


def gla_step(state, proj, w):
    t = state.shape[0]
    assert t % SUBLANES == 0
    rows = lambda width, col: pl.BlockSpec((SUBLANES, width), lambda i, j: (i, col))
    st = pl.BlockSpec((1, GLA_KEY_DIM, GLA_DV), lambda i, j: (i * SUBLANES + j, 0, 0))
    return pl.pallas_call(
        _gla_step_kernel,
        grid=(t // SUBLANES, SUBLANES),
        in_specs=[st, rows(GLA_KEY_DIM, COL_Q // GLA_KEY_DIM), rows(GLA_KEY_DIM, COL_K // GLA_KEY_DIM),
                  rows(GLA_VAL_DIM, COL_V // GLA_VAL_DIM), rows(GLA_VAL_DIM, COL_G // GLA_VAL_DIM),
                  rows(LANES, COL_SMALL // LANES), _const_spec((LANES, GLA_KEY_DIM)),
                  _const_spec((1, GLA_KEY_DIM)), _const_spec((1, GLA_VAL_DIM))],
        out_specs=[st, rows(GLA_VAL_DIM, 0)],
        out_shape=[
            jax.ShapeDtypeStruct((t, GLA_KEY_DIM, GLA_DV), F32),
            jax.ShapeDtypeStruct((t, GLA_VAL_DIM), F32),
        ],
        scratch_shapes=[pltpu.VMEM((SUBLANES, GLA_KEY_DIM), F32), pltpu.VMEM((SUBLANES, GLA_VAL_DIM), F32)],
        compiler_params=_cparams("parallel", "arbitrary"),
        name="gla_step",
    )(state, proj, proj, proj, proj, proj, w["gate_up"], w["gate_b"], w["gla_norm_w"])


def _merge_kernel(x_ref, ys_ref, og_ref, nw_ref, wm_ref, wso_ref, wgo_ref, wout_ref, o_ref):
    x = x_ref[...]
    xn = (_rms_scale(x) * nw_ref[...]).astype(BF16)
    gates = jax.nn.sigmoid(_dot(xn, wm_ref[...]))
    y_ssd = _dot(ys_ref[...].astype(BF16), wso_ref[...])
    y_gla = _dot(og_ref[...].astype(BF16), wgo_ref[...])
    mix = gates[:, 0:D_MODEL] * y_ssd + gates[:, D_MODEL:] * y_gla
    o_ref[...] = x + _dot(mix.astype(BF16), wout_ref[...])


def merge(x, y_ssd, o_gla, w):
    t = x.shape[0]
    tm = _row_tile(t, 512)
    row = lambda width: pl.BlockSpec((tm, width), lambda i: (i, 0))
    return pl.pallas_call(
        _merge_kernel,
        grid=(t // tm,),
        in_specs=[row(D_MODEL), row(SSD_D_INNER), row(GLA_VAL_DIM), _const_spec((1, D_MODEL)),
                  _const_spec((D_MODEL, 2 * D_MODEL)), _const_spec((SSD_D_INNER, D_MODEL)),
                  _const_spec((GLA_VAL_DIM, D_MODEL)), _const_spec((D_MODEL, D_MODEL))],
        out_specs=row(D_MODEL),
        out_shape=jax.ShapeDtypeStruct((t, D_MODEL), F32),
        compiler_params=_cparams("parallel"),
        name="merge",
    )(x, y_ssd, o_gla, w["norm1_w"], w["w_merge"], w["w_ssd_out"], w["w_gla_out"], w["w_out"])


def _merge_exchange_pairs(n):
    pairs = []
    p = n // 2
    while p > 0:
        q, r, d = n // 2, 0, p
        while d > 0:
            pairs += [(i, i + d) for i in range(n - d) if (i & p) == r]
            d, q, r = q - p, q // 2, p
        p //= 2
    return pairs


_SORT16 = _merge_exchange_pairs(PEER_TOPK)
_BITONIC16 = [(i, i + d) for d in (8, 4, 2, 1) for i in range(PEER_TOPK) if (i & d) == 0]


def _exchange(v, ids, i, j):
    if ids is None:
        v[i], v[j] = jnp.maximum(v[i], v[j]), jnp.minimum(v[i], v[j])
    else:
        ge = v[i] >= v[j]
        v[i], v[j] = jnp.where(ge, v[i], v[j]), jnp.where(ge, v[j], v[i])
        ids[i], ids[j] = jnp.where(ge, ids[i], ids[j]), jnp.where(ge, ids[j], ids[i])


def _merge_top16(v, ids, w, wids):
    n, m = len(v), len(w)
    for i in range(n - m, n):
        o = n - 1 - i
        if ids is None:
            v[i] = jnp.maximum(v[i], w[o])
        else:
            ge = v[i] >= w[o]
            v[i] = jnp.where(ge, v[i], w[o])
            ids[i] = jnp.where(ge, ids[i], wids[o])
    for i, j in _BITONIC16:
        _exchange(v, ids, i, j)


def _top16_rows(x, with_ids):
    n = x.shape[0] // SUBLANES
    assert n == PEER_TOPK
    v = [x[SUBLANES * r:SUBLANES * (r + 1), :] for r in range(n)]
    ids = None
    if with_ids:
        row = lax.broadcasted_iota(jnp.int32, v[0].shape, 0).astype(F32)
        ids = [row + float(SUBLANES * r) for r in range(n)]
    for i, j in _SORT16:
        _exchange(v, ids, i, j)
    shift = SUBLANES // 2
    while shift:
        w = [pltpu.roll(a, shift, 0) for a in v]
        wids = [pltpu.roll(a, shift, 0) for a in ids] if with_ids else None
        _merge_top16(v, ids, w, wids)
        shift //= 2
    return v, ids


def _peer_kernel(x_ref, nw_ref, fnw_ref, wqt_ref, keys_ref, u_ref, v_ref, o_ref,
                 hn_ref, qt_ref, rw_ref, s2t_ref, svk_ref, sik_ref, stat_ref, g1k_ref, ids_ref, g1tm_ref,
                 kept8_ref, e8_ref, hid_ref, acc_ref):
    s = pl.program_id(1)
    ns = pl.num_programs(1)
    tb = x_ref.shape[0]
    nk = PEER_N_KEYS
    nj = PEER_HEADS * PEER_TOPK

    @pl.when(s == 0)
    def _select():
        acc_ref[...] = jnp.zeros(acc_ref.shape, F32)
        hid_ref[...] = jnp.zeros(hid_ref.shape, F32)
        hn_ref[...] = (_rms_scale(x_ref[...]) * nw_ref[...]).astype(BF16)
        stat_ref[...] = jnp.zeros(stat_ref.shape, F32)
        qt_ref[...] = _dot_nt(wqt_ref[...], hn_ref[...]).astype(BF16)
        for h in range(PEER_HEADS):
            for c in range(2):
                r0 = (h * 2 + c) * (PEER_DQ // 2)
                sc_t = _dot(keys_ref[h * 2 + c], qt_ref[r0:r0 + PEER_DQ // 2, :])
                for g in range(tb // LANES):
                    ls = slice(g * LANES, (g + 1) * LANES)
                    vals, ids = _top16_rows(sc_t[:, ls], c == 0)
                    for k in range(PEER_TOPK):
                        svk_ref[c, k, h:h + 1, ls] = vals[k][0:1, :]
                        if c == 0:
                            sik_ref[k, h:h + 1, ls] = ids[k][0:1, :]
                if c == 1:
                    s2t_ref[h] = sc_t

        pairs = ([(0, k2) for k2 in range(PEER_TOPK)]
                 + [(k1, k2) for k1 in range(1, PEER_TOPK // 2) for k2 in range(PEER_TOPK // (k1 + 1))]
                 + [(k1, 0) for k1 in range(PEER_TOPK // 2, PEER_TOPK)])
        for g in range(tb // LANES):
            ls = slice(g * LANES, (g + 1) * LANES)
            a = [svk_ref[0, k, :, ls] for k in range(PEER_TOPK)]
            b = [svk_ref[1, k, :, ls] for k in range(PEER_TOPK)]
            best = [a[0] + b[k2] for k2 in range(PEER_TOPK)]
            for k1 in range(1, PEER_TOPK // 2):
                _merge_top16(best, None, [a[k1] + b[k2] for k2 in range(PEER_TOPK // (k1 + 1))], None)
            _merge_top16(best, None, [a[k1] + b[0] for k1 in range(PEER_TOPK // 2, PEER_TOPK)], None)
            tau, top = best[PEER_TOPK - 1], best[0]
            z = jnp.zeros_like(tau)
            for k1, k2 in pairs:
                cv = a[k1] + b[k2]
                z = z + jnp.where(cv >= tau, jnp.exp(cv - top), 0.0)
            stat_ref[0:PEER_HEADS, ls] = tau
            stat_ref[PEER_HEADS:2 * PEER_HEADS, ls] = b[0]
            inv_z = 1.0 / z
            for k in range(PEER_TOPK):
                g1k_ref[k, :, ls] = jnp.exp(a[k] - a[0]) * inv_z

        ids_ref[...] = sik_ref[...].reshape(nj, tb).T
        g1tm_ref[...] = g1k_ref[...].reshape(nj, tb).T
        def count_kept(h, carry):
            sc2 = s2t_ref[h]
            tau_t = stat_ref[pl.ds(h, 1), :]
            kept = jnp.zeros(sc2.shape, F32)
            for k1 in range(PEER_TOPK):
                kept = kept + jnp.where((sc2 + svk_ref[0, k1, pl.ds(h, 1), :]) >= tau_t, 1.0, 0.0)
            e2 = jnp.exp(sc2 - stat_ref[pl.ds(PEER_HEADS + h, 1), :])
            kept8_ref[pl.ds(h, tb, stride=PEER_HEADS), :] = kept.T
            e8_ref[pl.ds(h, tb, stride=PEER_HEADS), :] = e2.T
            return carry

        lax.fori_loop(0, PEER_HEADS, count_kept, 0)

        key1 = lax.broadcasted_iota(jnp.int32, (nk, nk), 0).astype(F32)

        def scatter(t, carry):
            hb = pl.multiple_of(t * PEER_HEADS, PEER_HEADS)
            kept8 = kept8_ref[pl.ds(hb, PEER_HEADS), :]
            e8 = e8_ref[pl.ds(hb, PEER_HEADS), :]
            rows = jnp.concatenate(
                [jnp.where(kept8 > float(k1), e8, 0.0) for k1 in range(PEER_TOPK)], axis=0).astype(BF16)
            place = jnp.where(key1 == ids_ref[pl.ds(t, 1), :], g1tm_ref[pl.ds(t, 1), :], 0.0)
            rw_ref[pl.ds(t * PEER_PITCH, nk), :] = _dot(place.astype(BF16), rows)
            return carry

        lax.fori_loop(0, tb, scatter, 0, unroll=PEER_SCATTER_UNROLL)

    prev = hid_ref[...]
    act = 0.5 * prev * (1.0 + lax.erf(prev * (2.0 ** -0.5)))
    slab0 = jnp.maximum(s - 1, 0) * PEER_SLABS_PER_STEP
    wts = jnp.concatenate(
        [rw_ref[pl.ds(slab0 + i, tb, stride=PEER_PITCH), :] for i in range(PEER_SLABS_PER_STEP)], axis=1)
    gated = (act * wts).astype(BF16)
    hn = hn_ref[...]
    half = PEER_STEP_EXPERTS // 2
    for c in range(2):
        hid_ref[:, c * half:(c + 1) * half] = _dot_nt(hn, u_ref[c * half:(c + 1) * half, :])
    half = D_MODEL // 2
    for c in range(2):
        acc_ref[:, c * half:(c + 1) * half] += _dot(gated, v_ref[:, c * half:(c + 1) * half])

    @pl.when(s == ns - 1)
    def _():
        o_ref[...] = _rms_scale(x_ref[...] + acc_ref[...]) * fnw_ref[...]


def peer_final(x, w):
    t = x.shape[0]
    tb = _row_tile(t, PEER_TOKEN_BLOCK)
    assert tb % LANES == 0
    nblk = PEER_N_EXPERTS // PEER_STEP_EXPERTS
    return pl.pallas_call(
        _peer_kernel,
        grid=(t // tb, nblk + 1),
        in_specs=[
            pl.BlockSpec((tb, D_MODEL), lambda i, s: (i, 0)),
            _const_spec((1, D_MODEL)),
            _const_spec((1, D_MODEL)),
            _const_spec((PEER_HEADS * PEER_DQ, D_MODEL)),
            _const_spec((2 * PEER_HEADS, PEER_N_KEYS, PEER_DQ // 2)),
            pl.BlockSpec((PEER_STEP_EXPERTS, D_MODEL), lambda i, s: (jnp.minimum(s, nblk - 1), 0)),
            pl.BlockSpec((PEER_STEP_EXPERTS, D_MODEL), lambda i, s: (jnp.maximum(s - 1, 0), 0)),
        ],
        out_specs=pl.BlockSpec((tb, D_MODEL), lambda i, s: (i, 0)),
        out_shape=jax.ShapeDtypeStruct((t, D_MODEL), F32),
        scratch_shapes=[
            pltpu.VMEM((tb, D_MODEL), BF16),
            pltpu.VMEM((PEER_HEADS * PEER_DQ, tb), BF16),
            pltpu.VMEM((tb * PEER_PITCH, PEER_N_KEYS), F32),
            pltpu.VMEM((PEER_HEADS, PEER_N_KEYS, tb), F32),
            pltpu.VMEM((2, PEER_TOPK, PEER_HEADS, tb), F32),
            pltpu.VMEM((PEER_TOPK, PEER_HEADS, tb), F32),
            pltpu.VMEM((LANES, tb), F32),
            pltpu.VMEM((PEER_TOPK, PEER_HEADS, tb), F32),
            pltpu.VMEM((tb, PEER_HEADS * PEER_TOPK), F32),
            pltpu.VMEM((tb, PEER_HEADS * PEER_TOPK), F32),
            pltpu.VMEM((tb * PEER_HEADS, PEER_N_KEYS), F32),
            pltpu.VMEM((tb * PEER_HEADS, PEER_N_KEYS), F32),
            pltpu.VMEM((tb, PEER_STEP_EXPERTS), F32),
            pltpu.VMEM((tb, D_MODEL), F32),
        ],
        compiler_params=_cparams("parallel", "arbitrary"),
        name="peer_final",
    )(x, w["norm2_w"], w["final_norm_w"], w["peer_wq_t"], w["peer_keys"], w["peer_u"], w["peer_v"])


def _prepare_weights(norm1_w, w_in, conv_w, conv_b, dt_bias, a_log, d_skip, ssd_norm_w, w_ssd_out,
                     gla_gate_up, gla_gate_b, gla_norm_w, w_gla_out, w_out, norm2_w, peer_w_q,
                     peer_sub_keys, peer_u, peer_v, final_norm_w):
    offs = [0]
    for sz in _IN_SIZES:
        offs.append(offs[-1] + sz)
    z0, xbc0, dt0, q0, _, _, _, glr0, mg0, end = offs
    small = jnp.concatenate(
        [w_in[:, dt0:q0], w_in[:, glr0:mg0],
         jnp.zeros((D_MODEL, LANES - SSD_HEADS - GLA_GATE_RANK), w_in.dtype)], axis=1)
    w_proj = jnp.concatenate([w_in[:, z0:dt0], w_in[:, q0:glr0], small], axis=1).astype(BF16)
    assert w_proj.shape[1] == IN_COLS

    def lane_pad(v):
        return jnp.pad(v, (0, LANES - v.shape[0])).reshape(1, LANES)

    head_of_col = jnp.arange(SSD_D_INNER) // SSD_HEAD_DIM
    expand = (jnp.arange(LANES)[:, None] == head_of_col[None, :]).astype(BF16)
    gate_up = jnp.zeros((LANES, GLA_KEY_DIM), F32).at[SMALL_GLR:SMALL_GLR + GLA_GATE_RANK].set(gla_gate_up)
    keys = jnp.transpose(peer_sub_keys, (1, 0, 2, 3)).reshape(2 * PEER_HEADS, PEER_N_KEYS, PEER_DQ // 2)
    return {
        "norm1_w": norm1_w.reshape(1, D_MODEL),
        "w_proj": w_proj,
        "w_merge": w_in[:, mg0:end].astype(BF16),
        "conv_w": conv_w,
        "conv_b": conv_b.reshape(1, SSD_CONV_DIM),
        "dt_bias": lane_pad(dt_bias),
        "a_log": lane_pad(a_log),
        "d_exp": jnp.repeat(d_skip, SSD_HEAD_DIM).reshape(1, SSD_D_INNER),
        "ssd_norm_w": ssd_norm_w.reshape(1, SSD_D_INNER),
        "expand": expand,
        "w_ssd_out": w_ssd_out.astype(BF16),
        "gate_up": gate_up.astype(BF16),
        "gate_b": gla_gate_b.reshape(1, GLA_KEY_DIM),
        "gla_norm_w": jnp.tile(gla_norm_w, GLA_HEADS).reshape(1, GLA_VAL_DIM),
        "w_gla_out": w_gla_out.astype(BF16),
        "w_out": w_out.astype(BF16),
        "norm2_w": norm2_w.reshape(1, D_MODEL),
        "peer_wq_t": peer_w_q.T.astype(BF16),
        "peer_keys": keys.astype(BF16),
        "peer_u": peer_u.astype(BF16),
        "peer_v": peer_v.astype(BF16),
        "final_norm_w": final_norm_w.reshape(1, D_MODEL),
    }


def _prompt_path(x_prompt, meta_tokens, w):
    b, seq, _ = x_prompt.shape
    assert seq % SSD_CHUNK == 0
    meta = jnp.broadcast_to(meta_tokens.astype(x_prompt.dtype)[None], (b, N_META, D_MODEL))
    xp = jnp.concatenate([jnp.zeros((b, PROMPT_ZERO_ROWS, D_MODEL), x_prompt.dtype), meta, x_prompt], axis=1)
    rows = xp.shape[1]
    proj = norm_matmul(xp.reshape(b * rows, D_MODEL), w["norm1_w"], w["w_proj"], IN_COLS // 5)
    y_ssd, st_ssd, conv_tail = ssd_prompt(proj, b, rows // SSD_CHUNK, PROMPT_ZERO_ROWS, PROMPT_SKIP_ROWS, w)
    o_gla, st_gla = gla_prompt(proj, b, rows, PROMPT_ZERO_ROWS, PROMPT_SKIP_ROWS, w)
    x1 = merge(x_prompt.reshape(b * seq, D_MODEL), y_ssd.reshape(b * seq, SSD_D_INNER),
               o_gla.reshape(b * seq, GLA_VAL_DIM), w)
    y = peer_final(x1, w).reshape(b, seq, D_MODEL)
    return (y,
            st_ssd.reshape(1, b, SSD_HEADS, SSD_HEAD_DIM, SSD_D_STATE),
            conv_tail[:, SUBLANES - (SSD_CONV - 1):, :][None],
            st_gla.reshape(1, b, GLA_HEADS, GLA_DK, GLA_DV))


def _sample_path(x_sample, state_ssd, state_conv, state_gla, w):
    b = x_sample.shape[0]
    x = x_sample.reshape(b, D_MODEL)
    proj = norm_matmul(x, w["norm1_w"], w["w_proj"], IN_COLS // 5)
    xact, xdt, dec, bm, cm, conv_new = ssd_step_pre(
        proj, state_conv.reshape(b, (SSD_CONV - 1) * SSD_CONV_DIM), w)
    st_ssd, y = ssd_step_state(state_ssd.reshape(b, SSD_D_INNER, SSD_D_STATE), xdt, dec, bm, cm)
    y_ssd = ssd_step_post(y, xact, proj, w)
    st_gla, o_gla = gla_step(state_gla.reshape(b, GLA_KEY_DIM, GLA_DV), proj, w)
    x1 = merge(x, y_ssd, o_gla, w)
    y = peer_final(x1, w).reshape(b, 1, D_MODEL)
    return (y,
            st_ssd.reshape(1, b, SSD_HEADS, SSD_HEAD_DIM, SSD_D_STATE),
            conv_new.reshape(1, b, SSD_CONV - 1, SSD_CONV_DIM),
            st_gla.reshape(1, b, GLA_HEADS, GLA_DK, GLA_DV))


def kernel(x_prompt, x_sample, state_ssd, state_conv, state_gla, meta_tokens, norm1_w, w_in, conv_w, conv_b,
           dt_bias, a_log, d_skip, ssd_norm_w, w_ssd_out, gla_gate_up, gla_gate_b, gla_norm_w, w_gla_out,
           w_out, norm2_w, peer_w_q, peer_sub_keys, peer_u, peer_v, final_norm_w):
    layer = (norm1_w, w_in, conv_w, conv_b, dt_bias, a_log, d_skip, ssd_norm_w, w_ssd_out, gla_gate_up,
             gla_gate_b, gla_norm_w, w_gla_out, w_out, norm2_w, peer_w_q, peer_sub_keys, peer_u, peer_v)
    assert all(p.shape[0] == 1 for p in layer), "single-layer step"
    w = _prepare_weights(*[p[0] for p in layer], final_norm_w)
    yp, p_ssd, p_conv, p_gla = _prompt_path(x_prompt, meta_tokens, w)
    ys, s_ssd, s_conv, s_gla = _sample_path(x_sample, state_ssd[0], state_conv[0], state_gla[0], w)
    return (yp, ys, p_ssd, p_conv, p_gla, s_ssd, s_conv, s_gla)
```

```python
import functools

import jax
import jax.numpy as jnp
from jax import lax
from jax.experimental import pallas as pl
from jax.experimental.pallas import tpu as pltpu

F32 = jnp.float32
BF16 = jnp.bfloat16

LANES = 128
SUBLANES = 8
VMEM_LIMIT_BYTES = 60 * 1024 * 1024

D_MODEL = 1024
N_META = 16
EPS = 1e-6
SSD_D_INNER = 2 * D_MODEL
SSD_HEAD_DIM = 64
SSD_HEADS = SSD_D_INNER // SSD_HEAD_DIM
SSD_GROUPS = 4
SSD_HPG = SSD_HEADS // SSD_GROUPS
SSD_D_STATE = 128
SSD_CONV = 4
SSD_CHUNK = 128
SSD_BC = 2 * SSD_GROUPS * SSD_D_STATE
SSD_CONV_DIM = SSD_D_INNER + SSD_BC
SSD_GROUP_W = SSD_HPG * SSD_HEAD_DIM
GLA_HEADS = 4
GLA_KEY_DIM = D_MODEL // 2
GLA_VAL_DIM = D_MODEL
GLA_DK = GLA_KEY_DIM // GLA_HEADS
GLA_DV = GLA_VAL_DIM // GLA_HEADS
GLA_GATE_RANK = 16
GLA_GATE_NORMALIZER = 16.0
GLA_CHUNK = 64
GLA_STEP_CHUNKS = 2
DECODE_STEP_SEQS = 2
PEER_HEADS = 8
PEER_N_KEYS = 128
PEER_N_EXPERTS = PEER_N_KEYS * PEER_N_KEYS
PEER_DQ = 256
PEER_TOPK = 16
PEER_SLABS_PER_STEP = 16
PEER_STEP_EXPERTS = PEER_SLABS_PER_STEP * PEER_N_KEYS
PEER_SCATTER_UNROLL = 16
PEER_PITCH = PEER_N_KEYS + 4
PEER_TOKEN_BLOCK = 256

_IN_SIZES = (SSD_D_INNER, SSD_CONV_DIM, SSD_HEADS, GLA_KEY_DIM, GLA_KEY_DIM, GLA_VAL_DIM,
             GLA_VAL_DIM, GLA_GATE_RANK, 2 * D_MODEL)
COL_Z = 0
COL_XS = COL_Z + SSD_D_INNER
COL_BC = COL_XS + SSD_D_INNER
COL_Q = COL_BC + SSD_BC
COL_K = COL_Q + GLA_KEY_DIM
COL_V = COL_K + GLA_KEY_DIM
COL_G = COL_V + GLA_VAL_DIM
COL_SMALL = COL_G + GLA_VAL_DIM
MXU_WIDTH = 256
SMALL_COLS = MXU_WIDTH
IN_COLS = COL_SMALL + SMALL_COLS
IN_COL_TILE = IN_COLS // 3
SMALL_GLR = SSD_HEADS
PROMPT_ZERO_ROWS = SSD_CHUNK - N_META
PROMPT_SKIP_ROWS = PROMPT_ZERO_ROWS + N_META


def _cparams(*sem):
    return pltpu.CompilerParams(dimension_semantics=sem, vmem_limit_bytes=VMEM_LIMIT_BYTES)


def _const_spec(shape):
    nd = len(shape)
    return pl.BlockSpec(shape, lambda *_: (0,) * nd, pipeline_mode=pl.Buffered(1))


def _row_tile(t, cap):
    tm = cap
    while t % tm:
        tm //= 2
    return tm


def _split3(x):
    hi = x.astype(BF16)
    r1 = x - hi.astype(F32)
    mid = r1.astype(BF16)
    lo = (r1 - mid.astype(F32)).astype(BF16)
    return hi, mid, lo


def _dot(a, b):
    return jnp.dot(a, b, preferred_element_type=F32)


def _dot_nt(a, b):
    return lax.dot_general(a, b, (((1,), (1,)), ((), ())), preferred_element_type=F32)


def _sel_right(sel01, x):
    hi, mid, lo = _split3(x)
    return _dot(hi, sel01) + _dot(mid, sel01) + _dot(lo, sel01)


def _sel_left(sel01, x):
    hi, mid, lo = _split3(x)
    return _dot(sel01, hi) + _dot(sel01, mid) + _dot(sel01, lo)


def _rms_scale(x):
    return x * lax.rsqrt(jnp.mean(x * x, axis=-1, keepdims=True) + EPS)


def _tril(n):
    r = lax.broadcasted_iota(jnp.int32, (n, n), 0)
    c = lax.broadcasted_iota(jnp.int32, (n, n), 1)
    return r >= c


def _norm_matmul_kernel(x_ref, nw_ref, w_ref, o_ref, xn_ref):
    @pl.when(pl.program_id(1) == 0)
    def _():
        xn_ref[...] = (_rms_scale(x_ref[...]) * nw_ref[...]).astype(BF16)

    o_ref[...] = _dot(xn_ref[...], w_ref[...])


def norm_matmul(x, norm_w, w, tn):
    t, d = x.shape
    n = w.shape[1]
    tm = _row_tile(t, 1024)
    assert n % tn == 0
    return pl.pallas_call(
        _norm_matmul_kernel,
        grid=(t // tm, n // tn),
        in_specs=[
            pl.BlockSpec((tm, d), lambda i, j: (i, 0)),
            pl.BlockSpec((1, d), lambda i, j: (0, 0)),
            pl.BlockSpec((d, tn), lambda i, j: (0, j)),
        ],
        out_specs=pl.BlockSpec((tm, tn), lambda i, j: (i, j)),
        out_shape=jax.ShapeDtypeStruct((t, n), F32),
        scratch_shapes=[pltpu.VMEM((tm, d), BF16)],
        compiler_params=_cparams("parallel", "arbitrary"),
        name="norm_in_proj",
    )(x, norm_w, w)


def _ssd_dt(small, dt_bias_ref, valid):
    lane = lax.broadcasted_iota(jnp.int32, small.shape, 1)
    dt = jax.nn.softplus(small + dt_bias_ref[...])
    keep = lane < SSD_HEADS
    if valid is not None:
        keep = jnp.logical_and(keep, valid)
    return jnp.where(keep, dt, 0.0)


def _ssd_gated_norm(y, z, norm_w_ref):
    y = y * jax.nn.silu(z)
    parts = []
    for g in range(SSD_GROUPS):
        parts.append(_rms_scale(y[:, g * SSD_GROUP_W:(g + 1) * SSD_GROUP_W]))
    return jnp.concatenate(parts, axis=1) * norm_w_ref[...]


def _ssd_prompt_kernel(z_ref, xs_ref, bc_ref, small_ref, conv_w_ref, conv_b_ref, dt_bias_ref,
                       a_log_ref, d_exp_ref, norm_w_ref, expand_ref,
                       y_ref, state_ref, convtail_ref,
                       hist_ref, st_ref, yacc_ref, *, zero_rows):
    c = pl.program_id(1)
    nc = pl.num_programs(1)
    q = SSD_CHUNK

    @pl.when(c == 0)
    def _():
        hist_ref[0:SUBLANES, :] = jnp.zeros((SUBLANES, SSD_CONV_DIM), F32)
        st_ref[...] = jnp.zeros(st_ref.shape, F32)

    hist_ref[SUBLANES:SUBLANES + q, 0:SSD_D_INNER] = xs_ref[...]
    hist_ref[SUBLANES:SUBLANES + q, SSD_D_INNER:SSD_CONV_DIM] = bc_ref[...]
    conv = conv_b_ref[...]
    for j in range(SSD_CONV):
        start = SUBLANES - (SSD_CONV - 1) + j
        conv = conv + conv_w_ref[j:j + 1, :] * hist_ref[start:start + q, :]
    xbc = jax.nn.silu(conv)
    tail = hist_ref[q:q + SUBLANES, :]
    hist_ref[0:SUBLANES, :] = tail
    convtail_ref[0] = tail

    xs = xbc[:, 0:SSD_D_INNER]
    bm = xbc[:, SSD_D_INNER:SSD_D_INNER + SSD_GROUPS * SSD_D_STATE].astype(BF16)
    cm = xbc[:, SSD_D_INNER + SSD_GROUPS * SSD_D_STATE:].astype(BF16)

    row = lax.broadcasted_iota(jnp.int32, (q, LANES), 0) + c * q
    dt = _ssd_dt(small_ref[...], dt_bias_ref, row >= zero_rows)
    da = dt * (-jnp.exp(a_log_ref[...]))
    tril = _tril(q)
    acum = _sel_left(tril.astype(BF16), da)
    acum_t = acum.T
    expand = expand_ref[...]
    dt_exp = _sel_right(expand, dt)
    acum_exp = _sel_right(expand, acum)
    acum_last = acum_exp[q - 1:q, :]
    xdt = xs * dt_exp
    xdt_bf = xdt.astype(BF16)
    x_end = (xdt * jnp.exp(acum_last - acum_exp)).astype(BF16)

    lane_lo = lax.broadcasted_iota(jnp.int32, (q, LANES), 1) < SSD_HEAD_DIM
    for g in range(SSD_GROUPS):
        bg = bm[:, g * SSD_D_STATE:(g + 1) * SSD_D_STATE]
        cg = cm[:, g * SSD_D_STATE:(g + 1) * SSD_D_STATE]
        cb = _dot_nt(cg, bg)
        for hp in range(SSD_HPG // 2):
            lo = (g * SSD_HPG + 2 * hp) * SSD_HEAD_DIM
            pair = []
            for k in range(2):
                h = g * SSD_HPG + 2 * hp + k
                diff = acum[:, h:h + 1] - acum_t[h:h + 1, :]
                seg = jnp.exp(jnp.where(tril, diff, -jnp.inf))
                pair.append(_dot((cb * seg).astype(BF16), xdt_bf[:, lo:lo + LANES]))
            yacc_ref[:, lo:lo + LANES] = jnp.where(lane_lo, pair[0], pair[1])

        gs = slice(g * SSD_GROUP_W, (g + 1) * SSD_GROUP_W)
        st_g = st_ref[:, gs]
        y_off = _dot(cg, st_g.astype(BF16)) * jnp.exp(acum_exp[:, gs])
        yacc_ref[:, gs] = yacc_ref[:, gs] + y_off
        st_ref[:, gs] = st_g * jnp.exp(acum_last[:, gs]) + _dot(bg.T, x_end[:, gs])

    y = yacc_ref[...] + d_exp_ref[...] * xs
    y_ref[0] = _ssd_gated_norm(y, z_ref[...], norm_w_ref).astype(BF16)

    @pl.when(c == nc - 1)
    def _():
        state_ref[0] = st_ref[...].T


def ssd_prompt(proj, batch, n_chunks, zero_rows, skip_rows, w):
    q = SSD_CHUNK
    assert skip_rows % q == 0
    first_real = skip_rows // q
    n_out = n_chunks - first_real

    def rows(b, c):
        return b * n_chunks + c

    return pl.pallas_call(
        functools.partial(_ssd_prompt_kernel, zero_rows=zero_rows),
        grid=(batch, n_chunks),
        in_specs=[
            pl.BlockSpec((q, SSD_D_INNER), lambda b, c: (rows(b, c), COL_Z // SSD_D_INNER)),
            pl.BlockSpec((q, SSD_D_INNER), lambda b, c: (rows(b, c), COL_XS // SSD_D_INNER)),
            pl.BlockSpec((q, SSD_BC), lambda b, c: (rows(b, c), COL_BC // SSD_BC)),
            pl.BlockSpec((q, LANES), lambda b, c: (rows(b, c), COL_SMALL // LANES)),
            _const_spec((SSD_CONV, SSD_CONV_DIM)),
            _const_spec((1, SSD_CONV_DIM)),
            _const_spec((1, LANES)),
            _const_spec((1, LANES)),
            _const_spec((1, SSD_D_INNER)),
            _const_spec((1, SSD_D_INNER)),
            _const_spec((LANES, SSD_D_INNER)),
        ],
        out_specs=[
            pl.BlockSpec((1, q, SSD_D_INNER), lambda b, c: (b, jnp.maximum(c - first_real, 0), 0)),
            pl.BlockSpec((1, SSD_D_INNER, SSD_D_STATE), lambda b, c: (b, 0, 0)),
            pl.BlockSpec((1, SUBLANES, SSD_CONV_DIM), lambda b, c: (b, 0, 0)),
        ],
        out_shape=[
            jax.ShapeDtypeStruct((batch, n_out * q, SSD_D_INNER), BF16),
            jax.ShapeDtypeStruct((batch, SSD_D_INNER, SSD_D_STATE), F32),
            jax.ShapeDtypeStruct((batch, SUBLANES, SSD_CONV_DIM), F32),
        ],
        scratch_shapes=[
            pltpu.VMEM((SUBLANES + q, SSD_CONV_DIM), F32),
            pltpu.VMEM((SSD_D_STATE, SSD_D_INNER), F32),
            pltpu.VMEM((q, SSD_D_INNER), F32),
        ],
        compiler_params=_cparams("parallel", "arbitrary"),
        name="ssd_prompt",
    )(proj, proj, proj, proj, w["conv_w"], w["conv_b"], w["dt_bias"], w["a_log"], w["d_exp"],
      w["ssd_norm_w"], w["expand"])


def _ssd_step_pre_kernel(xs_ref, bc_ref, small_ref, c0_ref, c1_ref, c2_ref, conv_w_ref, conv_b_ref,
                         dt_bias_ref, a_log_ref, expand_ref,
                         xact_ref, xdt_ref, decay_ref, bm_ref, cm_ref, convnew_ref):
    x = jnp.concatenate([xs_ref[...], bc_ref[...]], axis=1)
    hist = (c0_ref[...], c1_ref[...], c2_ref[...], x)
    conv = conv_b_ref[...]
    for j in range(SSD_CONV):
        conv = conv + conv_w_ref[j:j + 1, :] * hist[j]
    xbc = jax.nn.silu(conv)
    xs = xbc[:, 0:SSD_D_INNER]
    dt = _ssd_dt(small_ref[...], dt_bias_ref, None)
    da = dt * (-jnp.exp(a_log_ref[...]))
    expand = expand_ref[...]
    xact_ref[...] = xs
    xdt_ref[...] = xs * _sel_right(expand, dt)
    decay_ref[...] = jnp.exp(da)
    bm_ref[...] = xbc[:, SSD_D_INNER:SSD_D_INNER + SSD_GROUPS * SSD_D_STATE]
    cm_ref[...] = xbc[:, SSD_D_INNER + SSD_GROUPS * SSD_D_STATE:]
    convnew_ref[:, 0:SSD_CONV_DIM] = hist[1]
    convnew_ref[:, SSD_CONV_DIM:2 * SSD_CONV_DIM] = hist[2]
    convnew_ref[:, 2 * SSD_CONV_DIM:] = x


def ssd_step_pre(proj, conv_state, w):
    t = proj.shape[0]
    tm = _row_tile(t, 128)
    gw = SSD_GROUPS * SSD_D_STATE
    row = lambda width, col: pl.BlockSpec((tm, width), lambda i: (i, col))
    return pl.pallas_call(
        _ssd_step_pre_kernel,
        grid=(t // tm,),
        in_specs=[
            row(SSD_D_INNER, COL_XS // SSD_D_INNER),
            row(SSD_BC, COL_BC // SSD_BC),
            row(LANES, COL_SMALL // LANES),
            row(SSD_CONV_DIM, 0), row(SSD_CONV_DIM, 1), row(SSD_CONV_DIM, 2),
            _const_spec((SSD_CONV, SSD_CONV_DIM)),
            _const_spec((1, SSD_CONV_DIM)),
            _const_spec((1, LANES)),
            _const_spec((1, LANES)),
            _const_spec((LANES, SSD_D_INNER)),
        ],
        out_specs=[row(SSD_D_INNER, 0), row(SSD_D_INNER, 0), row(LANES, 0), row(gw, 0),
                   row(gw, 0), row((SSD_CONV - 1) * SSD_CONV_DIM, 0)],
        out_shape=[
            jax.ShapeDtypeStruct((t, SSD_D_INNER), F32),
            jax.ShapeDtypeStruct((t, SSD_D_INNER), F32),
            jax.ShapeDtypeStruct((t, LANES), F32),
            jax.ShapeDtypeStruct((t, gw), F32),
            jax.ShapeDtypeStruct((t, gw), F32),
            jax.ShapeDtypeStruct((t, (SSD_CONV - 1) * SSD_CONV_DIM), F32),
        ],
        compiler_params=_cparams("parallel"),
        name="ssd_step_pre",
    )(proj, proj, proj, conv_state, conv_state, conv_state, w["conv_w"], w["conv_b"], w["dt_bias"],
      w["a_log"], w["expand"])


def _dot_tn(a, b):
    return lax.dot_general(a, b, (((0,), (0,)), ((), ())), preferred_element_type=F32)


def _own_row(j):
    return lax.broadcasted_iota(jnp.int32, (SUBLANES, 1), 0) == j


def _ssd_step_state_kernel(s_ref, xdt_ref, dec_ref, bm_ref, cm_ref, snew_ref, y_ref):
    j = pl.program_id(1)

    @pl.when(j == 0)
    def _():
        y_ref[...] = jnp.zeros(y_ref.shape, F32)

    def group_rows(x):
        return jnp.concatenate(
            [x[:, g * SSD_D_STATE:(g + 1) * SSD_D_STATE] for g in range(SSD_GROUPS)], axis=0)

    group_of_lane = lax.broadcasted_iota(jnp.int32, (SUBLANES, SSD_D_INNER), 1) // SSD_GROUP_W
    b_rows = group_rows(bm_ref[...]).astype(BF16)
    y_new = jnp.zeros(y_ref.shape, F32)
    for u in range(DECODE_STEP_SEQS):
        row = j * DECODE_STEP_SEQS + u
        own = _own_row(row)
        dec = dec_ref[pl.ds(row, 1), :]
        xdt = jnp.where(own, xdt_ref[...], 0.0)
        x_rows = jnp.concatenate(
            [jnp.where(group_of_lane == g, xdt, 0.0) for g in range(SSD_GROUPS)], axis=0).astype(BF16)
        c_rows = group_rows(jnp.where(own, cm_ref[...], 0.0)).astype(BF16)
        decayed = [s_ref[u, h * SSD_HEAD_DIM:(h + 1) * SSD_HEAD_DIM, :] * dec[:, h:h + 1]
                   for h in range(SSD_HEADS)]
        s_new = jnp.concatenate(decayed, axis=0) + _dot_tn(x_rows, b_rows)
        snew_ref[u] = s_new
        y_rows = _dot_nt(c_rows, s_new.astype(BF16))
        y_new = y_new + jnp.concatenate(
            [y_rows[g * SUBLANES:(g + 1) * SUBLANES, g * SSD_GROUP_W:(g + 1) * SSD_GROUP_W]
             for g in range(SSD_GROUPS)], axis=1)
    y_ref[...] += y_new


def _decode_state_specs(rows, cols):
    steps = SUBLANES // DECODE_STEP_SEQS
    shape = (DECODE_STEP_SEQS, rows, cols)
    index = lambda i, j: (i * steps + j, 0, 0)
    return pl.BlockSpec(shape, index), pl.BlockSpec(shape, index)


def ssd_step_state(state, xdt, dec, bm, cm):
    t = state.shape[0]
    assert t % SUBLANES == 0
    gw = SSD_GROUPS * SSD_D_STATE
    rows = lambda width: pl.BlockSpec((SUBLANES, width), lambda i, j: (i, 0))
    st_in, st_out = _decode_state_specs(SSD_D_INNER, SSD_D_STATE)
    return pl.pallas_call(
        _ssd_step_state_kernel,
        grid=(t // SUBLANES, SUBLANES // DECODE_STEP_SEQS),
        in_specs=[st_in, rows(SSD_D_INNER), rows(LANES), rows(gw), rows(gw)],
        out_specs=[st_out, rows(SSD_D_INNER)],
        out_shape=[
            jax.ShapeDtypeStruct((t, SSD_D_INNER, SSD_D_STATE), F32),
            jax.ShapeDtypeStruct((t, SSD_D_INNER), F32),
        ],
        compiler_params=_cparams("parallel", "arbitrary"),
        name="ssd_step_state",
    )(state, xdt, dec, bm, cm)


def _ssd_step_post_kernel(y_ref, xact_ref, z_ref, d_exp_ref, norm_w_ref, o_ref):
    y = y_ref[...] + d_exp_ref[...] * xact_ref[...]
    o_ref[...] = _ssd_gated_norm(y, z_ref[...], norm_w_ref).astype(BF16)


def ssd_step_post(y, xact, proj, w):
    t = y.shape[0]
    tm = _row_tile(t, 128)
    row = lambda col: pl.BlockSpec((tm, SSD_D_INNER), lambda i: (i, col))
    return pl.pallas_call(
        _ssd_step_post_kernel,
        grid=(t // tm,),
        in_specs=[row(0), row(0), row(COL_Z // SSD_D_INNER), _const_spec((1, SSD_D_INNER)),
                  _const_spec((1, SSD_D_INNER))],
        out_specs=row(0),
        out_shape=jax.ShapeDtypeStruct((t, SSD_D_INNER), BF16),
        compiler_params=_cparams("parallel"),
        name="ssd_step_post",
    )(y, xact, proj, w["d_exp"], w["ssd_norm_w"])


def _gla_gate(small, gate_up_ref, gate_b_ref):
    pre = _dot(small.astype(BF16), gate_up_ref[...]) + gate_b_ref[...]
    return jax.nn.log_sigmoid(pre) / GLA_GATE_NORMALIZER


def _gla_out_norm(o, g, norm_w_ref):
    parts = []
    for h in range(GLA_HEADS):
        parts.append(_rms_scale(o[:, h * GLA_DV:(h + 1) * GLA_DV]))
    return jnp.concatenate(parts, axis=1) * norm_w_ref[...] * jax.nn.silu(g)


def _gla_prompt_kernel(q_ref, k_ref, v_ref, g_ref, small_ref, gate_up_ref, gate_b_ref, norm_w_ref,
                       o_ref, state_ref, s_ref, *, zero_rows):
    c = pl.program_id(1)
    nc = pl.num_programs(1)
    q = GLA_CHUNK
    rows = q * GLA_STEP_CHUNKS

    @pl.when(c == 0)
    def _():
        s_ref[...] = jnp.zeros(s_ref.shape, F32)

    valid = (lax.broadcasted_iota(jnp.int32, (rows, GLA_KEY_DIM), 0) + c * rows) >= zero_rows
    gk = jnp.where(valid, _gla_gate(small_ref[...], gate_up_ref, gate_b_ref), 0.0)
    kk = jnp.where(valid, k_ref[...], 0.0)
    r = lax.broadcasted_iota(jnp.int32, (rows, rows), 0)
    cc = lax.broadcasted_iota(jnp.int32, (rows, rows), 1)
    chunk_tril = jnp.logical_and(r >= cc, r // q == cc // q)
    bcum = _sel_left(chunk_tril.astype(BF16), gk)
    lasts = [bcum[(i + 1) * q - 1:(i + 1) * q, :] for i in range(GLA_STEP_CHUNKS)]
    last_rows = jnp.concatenate([jnp.broadcast_to(x, (q, GLA_KEY_DIM)) for x in lasts], axis=0)
    qe = q_ref[...] * jnp.exp(bcum) * (GLA_DK ** -0.5)
    ke = kk * jnp.exp(-bcum)
    kend = kk * jnp.exp(last_rows - bcum)
    v_all = v_ref[...]
    vv = v_all.astype(BF16)
    chunk_of_row = lax.broadcasted_iota(jnp.int32, (rows, GLA_VAL_DIM), 0) // q
    v_chunk = [jnp.where(chunk_of_row == i, v_all, 0.0).astype(BF16) for i in range(GLA_STEP_CHUNKS)]
    outs = [[] for _ in range(GLA_STEP_CHUNKS)]
    for h in range(GLA_HEADS):
        ks = slice(h * GLA_DK, (h + 1) * GLA_DK)
        vs = slice(h * GLA_DV, (h + 1) * GLA_DV)
        qe_h = qe[:, ks].astype(BF16)
        att = jnp.where(chunk_tril, _dot_nt(qe_h, ke[:, ks].astype(BF16)), 0.0)
        o_intra = _dot(att.astype(BF16), vv[:, vs])
        kend_t = kend[:, ks].T.astype(BF16)
        s_h = s_ref[ks, :]
        for i in range(GLA_STEP_CHUNKS):
            rs = slice(i * q, (i + 1) * q)
            outs[i].append(o_intra[rs, :] + _dot(qe_h[rs, :], s_h.astype(BF16)))
            decay_col = jnp.broadcast_to(jnp.exp(lasts[i][:, ks]), (GLA_DK, GLA_DK)).T
            s_h = (jnp.concatenate([decay_col] * (GLA_DV // GLA_DK), axis=1) * s_h
                   + _dot(kend_t, v_chunk[i][:, vs]))
        s_ref[ks, :] = s_h
    o = jnp.concatenate([jnp.concatenate(o_i, axis=1) for o_i in outs], axis=0)
    o_ref[0] = _gla_out_norm(o, g_ref[...], norm_w_ref).astype(BF16)

    @pl.when(c == nc - 1)
    def _():
        state_ref[0] = s_ref[...]


def gla_prompt(proj, batch, n_rows, zero_rows, skip_rows, w):
    rows = GLA_CHUNK * GLA_STEP_CHUNKS
    assert skip_rows % rows == 0 and n_rows % rows == 0
    n_steps = n_rows // rows
    first_real = skip_rows // rows
    n_out = n_steps - first_real

    def blk(b, c):
        return b * n_steps + c

    return pl.pallas_call(
        functools.partial(_gla_prompt_kernel, zero_rows=zero_rows),
        grid=(batch, n_steps),
        in_specs=[
            pl.BlockSpec((rows, GLA_KEY_DIM), lambda b, c: (blk(b, c), COL_Q // GLA_KEY_DIM)),
            pl.BlockSpec((rows, GLA_KEY_DIM), lambda b, c: (blk(b, c), COL_K // GLA_KEY_DIM)),
            pl.BlockSpec((rows, GLA_VAL_DIM), lambda b, c: (blk(b, c), COL_V // GLA_VAL_DIM)),
            pl.BlockSpec((rows, GLA_VAL_DIM), lambda b, c: (blk(b, c), COL_G // GLA_VAL_DIM)),
            pl.BlockSpec((rows, LANES), lambda b, c: (blk(b, c), COL_SMALL // LANES)),
            _const_spec((LANES, GLA_KEY_DIM)),
            _const_spec((1, GLA_KEY_DIM)),
            _const_spec((1, GLA_VAL_DIM)),
        ],
        out_specs=[
            pl.BlockSpec((1, rows, GLA_VAL_DIM), lambda b, c: (b, jnp.maximum(c - first_real, 0), 0)),
            pl.BlockSpec((1, GLA_KEY_DIM, GLA_DV), lambda b, c: (b, 0, 0)),
        ],
        out_shape=[
            jax.ShapeDtypeStruct((batch, n_out * rows, GLA_VAL_DIM), BF16),
            jax.ShapeDtypeStruct((batch, GLA_KEY_DIM, GLA_DV), F32),
        ],
        scratch_shapes=[pltpu.VMEM((GLA_KEY_DIM, GLA_DV), F32)],
        compiler_params=_cparams("parallel", "arbitrary"),
        name="gla_prompt",
    )(proj, proj, proj, proj, proj, w["gate_up"], w["gate_b"], w["gla_norm_w"])


def _gla_step_kernel(s_ref, q_ref, k_ref, v_ref, g_ref, small_ref, gate_up_ref, gate_b_ref, norm_w_ref,
                     snew_ref, o_ref, dec_ref, acc_ref):
    j = pl.program_id(1)

    @pl.when(j == 0)
    def _():
        dec_ref[...] = jnp.exp(_gla_gate(small_ref[...], gate_up_ref, gate_b_ref))
        acc_ref[...] = jnp.zeros(acc_ref.shape, F32)

    head_of_lane = lax.broadcasted_iota(jnp.int32, (SUBLANES, GLA_KEY_DIM), 1) // GLA_DK

    def per_head_rows(x):
        return jnp.concatenate([jnp.where(head_of_lane == h, x, 0.0) for h in range(GLA_HEADS)], axis=0)

    vv = v_ref[...]
    v_rows = jnp.concatenate(
        [vv[:, h * GLA_DV:(h + 1) * GLA_DV] for h in range(GLA_HEADS)], axis=0).astype(BF16)
    o_new = jnp.zeros(acc_ref.shape, F32)
    for u in range(DECODE_STEP_SEQS):
        own = _own_row(j * DECODE_STEP_SEQS + u)
        k_rows = per_head_rows(jnp.where(own, k_ref[...], 0.0)).astype(BF16)
        q_rows = per_head_rows(jnp.where(own, q_ref[...] * (GLA_DK ** -0.5), 0.0)).astype(BF16)
        pieces = [p.astype(F32) for p in _split3(jnp.where(own, dec_ref[...], 0.0))]
        d_rows = jnp.concatenate(pieces + [jnp.zeros_like(pieces[0])], axis=0).astype(BF16)
        decay = _dot_tn(d_rows, jnp.ones((d_rows.shape[0], GLA_DV), BF16))
        s_new = s_ref[u] * decay + _dot_tn(k_rows, v_rows)
        snew_ref[u] = s_new
        o_rows = _dot(q_rows, s_new.astype(BF16))
        o_new = o_new + jnp.concatenate(
            [o_rows[h * SUBLANES:(h + 1) * SUBLANES, :] for h in range(GLA_HEADS)], axis=1)
    acc_ref[...] += o_new

    @pl.when(j == pl.num_programs(1) - 1)
    def _():
        o_ref[...] = _gla_out_norm(acc_ref[...], g_ref[...], norm_w_ref)


def gla_step(state, proj, w):
    t = state.shape[0]
    assert t % SUBLANES == 0
    rows = lambda width, col: pl.BlockSpec((SUBLANES, width), lambda i, j: (i, col))
    st_in, st_out = _decode_state_specs(GLA_KEY_DIM, GLA_DV)
    return pl.pallas_call(
        _gla_step_kernel,
        grid=(t // SUBLANES, SUBLANES // DECODE_STEP_SEQS),
        in_specs=[st_in, rows(GLA_KEY_DIM, COL_Q // GLA_KEY_DIM), rows(GLA_KEY_DIM, COL_K // GLA_KEY_DIM),
                  rows(GLA_VAL_DIM, COL_V // GLA_VAL_DIM), rows(GLA_VAL_DIM, COL_G // GLA_VAL_DIM),
                  rows(LANES, COL_SMALL // LANES), _const_spec((LANES, GLA_KEY_DIM)),
                  _const_spec((1, GLA_KEY_DIM)), _const_spec((1, GLA_VAL_DIM))],
        out_specs=[st_out, rows(GLA_VAL_DIM, 0)],
        out_shape=[
            jax.ShapeDtypeStruct((t, GLA_KEY_DIM, GLA_DV), F32),
            jax.ShapeDtypeStruct((t, GLA_VAL_DIM), F32),
        ],
        scratch_shapes=[pltpu.VMEM((SUBLANES, GLA_KEY_DIM), F32), pltpu.VMEM((SUBLANES, GLA_VAL_DIM), F32)],
        compiler_params=_cparams("parallel", "arbitrary"),
        name="gla_step",
    )(state, proj, proj, proj, proj, proj, w["gate_up"], w["gate_b"], w["gla_norm_w"])


def _merge_kernel(x_ref, ys_ref, og_ref, nw_ref, wm_ref, wso_ref, wgo_ref, wout_ref, o_ref):
    x = x_ref[...]
    xn = (_rms_scale(x) * nw_ref[...]).astype(BF16)
    gates = jax.nn.sigmoid(_dot(xn, wm_ref[...]))
    y_ssd = _dot(ys_ref[...].astype(BF16), wso_ref[...])
    y_gla = _dot(og_ref[...].astype(BF16), wgo_ref[...])
    mix = gates[:, 0:D_MODEL] * y_ssd + gates[:, D_MODEL:] * y_gla
    o_ref[...] = x + _dot(mix.astype(BF16), wout_ref[...])


def merge(x, y_ssd, o_gla, w):
    t = x.shape[0]
    tm = _row_tile(t, 512)
    row = lambda width: pl.BlockSpec((tm, width), lambda i: (i, 0))
    return pl.pallas_call(
        _merge_kernel,
        grid=(t // tm,),
        in_specs=[row(D_MODEL), row(SSD_D_INNER), row(GLA_VAL_DIM), _const_spec((1, D_MODEL)),
                  _const_spec((D_MODEL, 2 * D_MODEL)), _const_spec((SSD_D_INNER, D_MODEL)),
                  _const_spec((GLA_VAL_DIM, D_MODEL)), _const_spec((D_MODEL, D_MODEL))],
        out_specs=row(D_MODEL),
        out_shape=jax.ShapeDtypeStruct((t, D_MODEL), F32),
        compiler_params=_cparams("parallel"),
        name="merge",
    )(x, y_ssd, o_gla, w["norm1_w"], w["w_merge"], w["w_ssd_out"], w["w_gla_out"], w["w_out"])


def _merge_exchange_pairs(n):
    pairs = []
    p = n // 2
    while p > 0:
        q, r, d = n // 2, 0, p
        while d > 0:
            pairs += [(i, i + d) for i in range(n - d) if (i & p) == r]
            d, q, r = q - p, q // 2, p
        p //= 2
    return pairs


_SORT16 = _merge_exchange_pairs(PEER_TOPK)
_BITONIC16 = [(i, i + d) for d in (8, 4, 2, 1) for i in range(PEER_TOPK) if (i & d) == 0]


def _exchange(v, ids, i, j):
    if ids is None:
        v[i], v[j] = jnp.maximum(v[i], v[j]), jnp.minimum(v[i], v[j])
    else:
        ge = v[i] >= v[j]
        v[i], v[j] = jnp.where(ge, v[i], v[j]), jnp.where(ge, v[j], v[i])
        ids[i], ids[j] = jnp.where(ge, ids[i], ids[j]), jnp.where(ge, ids[j], ids[i])


def _merge_top16(v, ids, w, wids):
    n, m = len(v), len(w)
    for i in range(n - m, n):
        o = n - 1 - i
        if ids is None:
            v[i] = jnp.maximum(v[i], w[o])
        else:
            ge = v[i] >= w[o]
            v[i] = jnp.where(ge, v[i], w[o])
            ids[i] = jnp.where(ge, ids[i], wids[o])
    for i, j in _BITONIC16:
        _exchange(v, ids, i, j)


def _top16_rows(x, with_ids):
    n = x.shape[0] // SUBLANES
    assert n == PEER_TOPK
    v = [x[SUBLANES * r:SUBLANES * (r + 1), :] for r in range(n)]
    ids = None
    if with_ids:
        row = lax.broadcasted_iota(jnp.int32, v[0].shape, 0).astype(F32)
        ids = [row + float(SUBLANES * r) for r in range(n)]
    for i, j in _SORT16:
        _exchange(v, ids, i, j)
    shift = SUBLANES // 2
    while shift:
        w = [pltpu.roll(a, shift, 0) for a in v]
        wids = [pltpu.roll(a, shift, 0) for a in ids] if with_ids else None
        _merge_top16(v, ids, w, wids)
        shift //= 2
    return v, ids


def _peer_kernel(x_ref, nw_ref, fnw_ref, wqt_ref, keys_ref, u_ref, v_ref, o_ref,
                 hn_ref, qt_ref, rw_ref, s2t_ref, svk_ref, sik_ref, stat_ref, g1k_ref, ids_ref, g1tm_ref,
                 kept8_ref, e8_ref, hid_ref, acc_ref):
    s = pl.program_id(1)
    ns = pl.num_programs(1)
    tb = x_ref.shape[0]
    nk = PEER_N_KEYS
    nj = PEER_HEADS * PEER_TOPK

    @pl.when(s == 0)
    def _select():
        acc_ref[...] = jnp.zeros(acc_ref.shape, F32)
        hid_ref[...] = jnp.zeros(hid_ref.shape, F32)
        hn_ref[...] = (_rms_scale(x_ref[...]) * nw_ref[...]).astype(BF16)
        stat_ref[...] = jnp.zeros(stat_ref.shape, F32)
        qt_ref[...] = _dot_nt(wqt_ref[...], hn_ref[...]).astype(BF16)
        for h in range(PEER_HEADS):
            for c in range(2):
                r0 = (h * 2 + c) * (PEER_DQ // 2)
                sc_t = _dot(keys_ref[h * 2 + c], qt_ref[r0:r0 + PEER_DQ // 2, :])
                for g in range(tb // LANES):
                    ls = slice(g * LANES, (g + 1) * LANES)
                    vals, ids = _top16_rows(sc_t[:, ls], c == 0)
                    for k in range(PEER_TOPK):
                        svk_ref[c, k, h:h + 1, ls] = vals[k][0:1, :]
                        if c == 0:
                            sik_ref[k, h:h + 1, ls] = ids[k][0:1, :]
                if c == 1:
                    s2t_ref[h] = sc_t

        pairs = ([(0, k2) for k2 in range(PEER_TOPK)]
                 + [(k1, k2) for k1 in range(1, PEER_TOPK // 2) for k2 in range(PEER_TOPK // (k1 + 1))]
                 + [(k1, 0) for k1 in range(PEER_TOPK // 2, PEER_TOPK)])
        for g in range(tb // LANES):
            ls = slice(g * LANES, (g + 1) * LANES)
            a = [svk_ref[0, k, :, ls] for k in range(PEER_TOPK)]
            b = [svk_ref[1, k, :, ls] for k in range(PEER_TOPK)]
            best = [a[0] + b[k2] for k2 in range(PEER_TOPK)]
            for k1 in range(1, PEER_TOPK // 2):
                _merge_top16(best, None, [a[k1] + b[k2] for k2 in range(PEER_TOPK // (k1 + 1))], None)
            _merge_top16(best, None, [a[k1] + b[0] for k1 in range(PEER_TOPK // 2, PEER_TOPK)], None)
            tau, top = best[PEER_TOPK - 1], best[0]
            z = jnp.zeros_like(tau)
            for k1, k2 in pairs:
                cv = a[k1] + b[k2]
                z = z + jnp.where(cv >= tau, jnp.exp(cv - top), 0.0)
            stat_ref[0:PEER_HEADS, ls] = tau
            stat_ref[PEER_HEADS:2 * PEER_HEADS, ls] = b[0]
            inv_z = 1.0 / z
            for k in range(PEER_TOPK):
                g1k_ref[k, :, ls] = jnp.exp(a[k] - a[0]) * inv_z

        ids_ref[...] = sik_ref[...].reshape(nj, tb).T
        g1tm_ref[...] = g1k_ref[...].reshape(nj, tb).T
        def count_kept(h, carry):
            sc2 = s2t_ref[h]
            tau_t = stat_ref[pl.ds(h, 1), :]
            kept = jnp.zeros(sc2.shape, F32)
            for k1 in range(PEER_TOPK):
                kept = jnp.where((sc2 + svk_ref[0, k1, pl.ds(h, 1), :]) >= tau_t, float(k1 + 1), kept)
            e2 = jnp.exp(sc2 - stat_ref[pl.ds(PEER_HEADS + h, 1), :])
            kept8_ref[pl.ds(h, tb, stride=PEER_HEADS), :] = kept.T
            e8_ref[pl.ds(h, tb, stride=PEER_HEADS), :] = e2.T
            return carry

        lax.fori_loop(0, PEER_HEADS, count_kept, 0)

        key1 = lax.broadcasted_iota(jnp.int32, (nk, nk), 0).astype(F32)

        def scatter(t, carry):
            hb = pl.multiple_of(t * PEER_HEADS, PEER_HEADS)
            kept8 = kept8_ref[pl.ds(hb, PEER_HEADS), :]
            e8 = e8_ref[pl.ds(hb, PEER_HEADS), :]
            rows = jnp.concatenate(
                [jnp.where(kept8 > float(k1), e8, 0.0) for k1 in range(PEER_TOPK)], axis=0).astype(BF16)
            place = jnp.where(key1 == ids_ref[pl.ds(t, 1), :], g1tm_ref[pl.ds(t, 1), :], 0.0)
            rw_ref[pl.ds(t * PEER_PITCH, nk), :] = _dot(place.astype(BF16), rows)
            return carry

        lax.fori_loop(0, tb, scatter, 0, unroll=PEER_SCATTER_UNROLL)

    prev = hid_ref[...]
    act = 0.5 * prev * (1.0 + lax.erf(prev * (2.0 ** -0.5)))
    slab0 = jnp.maximum(s - 1, 0) * PEER_SLABS_PER_STEP
    wts = jnp.concatenate(
        [rw_ref[pl.ds(slab0 + i, tb, stride=PEER_PITCH), :] for i in range(PEER_SLABS_PER_STEP)], axis=1)
    gated = (act * wts).astype(BF16)
    hn = hn_ref[...]
    half = PEER_STEP_EXPERTS // 2
    for c in range(2):
        hid_ref[:, c * half:(c + 1) * half] = _dot_nt(hn, u_ref[c * half:(c + 1) * half, :])
    half = D_MODEL // 2
    for c in range(2):
        acc_ref[:, c * half:(c + 1) * half] += _dot(gated, v_ref[:, c * half:(c + 1) * half])

    @pl.when(s == ns - 1)
    def _():
        o_ref[...] = _rms_scale(x_ref[...] + acc_ref[...]) * fnw_ref[...]


def peer_final(x, w):
    t = x.shape[0]
    tb = _row_tile(t, PEER_TOKEN_BLOCK)
    assert tb % LANES == 0
    nblk = PEER_N_EXPERTS // PEER_STEP_EXPERTS
    return pl.pallas_call(
        _peer_kernel,
        grid=(t // tb, nblk + 1),
        in_specs=[
            pl.BlockSpec((tb, D_MODEL), lambda i, s: (i, 0)),
            _const_spec((1, D_MODEL)),
            _const_spec((1, D_MODEL)),
            _const_spec((PEER_HEADS * PEER_DQ, D_MODEL)),
            _const_spec((2 * PEER_HEADS, PEER_N_KEYS, PEER_DQ // 2)),
            pl.BlockSpec((PEER_STEP_EXPERTS, D_MODEL), lambda i, s: (jnp.minimum(s, nblk - 1), 0)),
            pl.BlockSpec((PEER_STEP_EXPERTS, D_MODEL), lambda i, s: (jnp.maximum(s - 1, 0), 0)),
        ],
        out_specs=pl.BlockSpec((tb, D_MODEL), lambda i, s: (i, 0)),
        out_shape=jax.ShapeDtypeStruct((t, D_MODEL), F32),
        scratch_shapes=[
            pltpu.VMEM((tb, D_MODEL), BF16),
            pltpu.VMEM((PEER_HEADS * PEER_DQ, tb), BF16),
            pltpu.VMEM((tb * PEER_PITCH, PEER_N_KEYS), F32),
            pltpu.VMEM((PEER_HEADS, PEER_N_KEYS, tb), F32),
            pltpu.VMEM((2, PEER_TOPK, PEER_HEADS, tb), F32),
            pltpu.VMEM((PEER_TOPK, PEER_HEADS, tb), F32),
            pltpu.VMEM((LANES, tb), F32),
            pltpu.VMEM((PEER_TOPK, PEER_HEADS, tb), F32),
            pltpu.VMEM((tb, PEER_HEADS * PEER_TOPK), F32),
            pltpu.VMEM((tb, PEER_HEADS * PEER_TOPK), F32),
            pltpu.VMEM((tb * PEER_HEADS, PEER_N_KEYS), F32),
            pltpu.VMEM((tb * PEER_HEADS, PEER_N_KEYS), F32),
            pltpu.VMEM((tb, PEER_STEP_EXPERTS), F32),
            pltpu.VMEM((tb, D_MODEL), F32),
        ],
        compiler_params=_cparams("parallel", "arbitrary"),
        name="peer_final",
    )(x, w["norm2_w"], w["final_norm_w"], w["peer_wq_t"], w["peer_keys"], w["peer_u"], w["peer_v"])


def _prepare_weights(norm1_w, w_in, conv_w, conv_b, dt_bias, a_log, d_skip, ssd_norm_w, w_ssd_out,
                     gla_gate_up, gla_gate_b, gla_norm_w, w_gla_out, w_out, norm2_w, peer_w_q,
                     peer_sub_keys, peer_u, peer_v, final_norm_w):
    offs = [0]
    for sz in _IN_SIZES:
        offs.append(offs[-1] + sz)
    z0, xbc0, dt0, q0, _, _, _, glr0, mg0, end = offs
    small = jnp.concatenate(
        [w_in[:, dt0:q0], w_in[:, glr0:mg0],
         jnp.zeros((D_MODEL, SMALL_COLS - SSD_HEADS - GLA_GATE_RANK), w_in.dtype)], axis=1)
    w_proj = jnp.concatenate([w_in[:, z0:dt0], w_in[:, q0:glr0], small], axis=1).astype(BF16)
    assert w_proj.shape[1] == IN_COLS

    def lane_pad(v):
        return jnp.pad(v, (0, LANES - v.shape[0])).reshape(1, LANES)

    head_of_col = jnp.arange(SSD_D_INNER) // SSD_HEAD_DIM
    expand = (jnp.arange(LANES)[:, None] == head_of_col[None, :]).astype(BF16)
    gate_up = jnp.zeros((LANES, GLA_KEY_DIM), F32).at[SMALL_GLR:SMALL_GLR + GLA_GATE_RANK].set(gla_gate_up)
    keys = jnp.transpose(peer_sub_keys, (1, 0, 2, 3)).reshape(2 * PEER_HEADS, PEER_N_KEYS, PEER_DQ // 2)
    return {
        "norm1_w": norm1_w.reshape(1, D_MODEL),
        "w_proj": w_proj,
        "w_merge": w_in[:, mg0:end].astype(BF16),
        "conv_w": conv_w,
        "conv_b": conv_b.reshape(1, SSD_CONV_DIM),
        "dt_bias": lane_pad(dt_bias),
        "a_log": lane_pad(a_log),
        "d_exp": jnp.repeat(d_skip, SSD_HEAD_DIM).reshape(1, SSD_D_INNER),
        "ssd_norm_w": ssd_norm_w.reshape(1, SSD_D_INNER),
        "expand": expand,
        "w_ssd_out": w_ssd_out.astype(BF16),
        "gate_up": gate_up.astype(BF16),
        "gate_b": gla_gate_b.reshape(1, GLA_KEY_DIM),
        "gla_norm_w": jnp.tile(gla_norm_w, GLA_HEADS).reshape(1, GLA_VAL_DIM),
        "w_gla_out": w_gla_out.astype(BF16),
        "w_out": w_out.astype(BF16),
        "norm2_w": norm2_w.reshape(1, D_MODEL),
        "peer_wq_t": peer_w_q.T.astype(BF16),
        "peer_keys": keys.astype(BF16),
        "peer_u": peer_u.astype(BF16),
        "peer_v": peer_v.astype(BF16),
        "final_norm_w": final_norm_w.reshape(1, D_MODEL),
    }


def _prompt_path(x_prompt, meta_tokens, w):
    b, seq, _ = x_prompt.shape
    assert seq % SSD_CHUNK == 0
    meta = jnp.broadcast_to(meta_tokens.astype(x_prompt.dtype)[None], (b, N_META, D_MODEL))
    xp = jnp.concatenate([jnp.zeros((b, PROMPT_ZERO_ROWS, D_MODEL), x_prompt.dtype), meta, x_prompt], axis=1)
    rows = xp.shape[1]
    proj = norm_matmul(xp.reshape(b * rows, D_MODEL), w["norm1_w"], w["w_proj"], IN_COL_TILE)
    y_ssd, st_ssd, conv_tail = ssd_prompt(proj, b, rows // SSD_CHUNK, PROMPT_ZERO_ROWS, PROMPT_SKIP_ROWS, w)
    o_gla, st_gla = gla_prompt(proj, b, rows, PROMPT_ZERO_ROWS, PROMPT_SKIP_ROWS, w)
    x1 = merge(x_prompt.reshape(b * seq, D_MODEL), y_ssd.reshape(b * seq, SSD_D_INNER),
               o_gla.reshape(b * seq, GLA_VAL_DIM), w)
    y = peer_final(x1, w).reshape(b, seq, D_MODEL)
    return (y,
            st_ssd.reshape(1, b, SSD_HEADS, SSD_HEAD_DIM, SSD_D_STATE),
            conv_tail[:, SUBLANES - (SSD_CONV - 1):, :][None],
            st_gla.reshape(1, b, GLA_HEADS, GLA_DK, GLA_DV))


def _sample_path(x_sample, state_ssd, state_conv, state_gla, w):
    b = x_sample.shape[0]
    x = x_sample.reshape(b, D_MODEL)
    proj = norm_matmul(x, w["norm1_w"], w["w_proj"], IN_COL_TILE)
    xact, xdt, dec, bm, cm, conv_new = ssd_step_pre(
        proj, state_conv.reshape(b, (SSD_CONV - 1) * SSD_CONV_DIM), w)
    st_ssd, y = ssd_step_state(state_ssd.reshape(b, SSD_D_INNER, SSD_D_STATE), xdt, dec, bm, cm)
    y_ssd = ssd_step_post(y, xact, proj, w)
    st_gla, o_gla = gla_step(state_gla.reshape(b, GLA_KEY_DIM, GLA_DV), proj, w)
    x1 = merge(x, y_ssd, o_gla, w)
    y = peer_final(x1, w).reshape(b, 1, D_MODEL)
    return (y,
            st_ssd.reshape(1, b, SSD_HEADS, SSD_HEAD_DIM, SSD_D_STATE),
            conv_new.reshape(1, b, SSD_CONV - 1, SSD_CONV_DIM),
            st_gla.reshape(1, b, GLA_HEADS, GLA_DK, GLA_DV))


def kernel(x_prompt, x_sample, state_ssd, state_conv, state_gla, meta_tokens, norm1_w, w_in, conv_w, conv_b,
           dt_bias, a_log, d_skip, ssd_norm_w, w_ssd_out, gla_gate_up, gla_gate_b, gla_norm_w, w_gla_out,
           w_out, norm2_w, peer_w_q, peer_sub_keys, peer_u, peer_v, final_norm_w):
    layer = (norm1_w, w_in, conv_w, conv_b, dt_bias, a_log, d_skip, ssd_norm_w, w_ssd_out, gla_gate_up,
             gla_gate_b, gla_norm_w, w_gla_out, w_out, norm2_w, peer_w_q, peer_sub_keys, peer_u, peer_v)
    assert all(p.shape[0] == 1 for p in layer), "single-layer step"
    w = _prepare_weights(*[p[0] for p in layer], final_norm_w)
    yp, p_ssd, p_conv, p_gla = _prompt_path(x_prompt, meta_tokens, w)
    ys, s_ssd, s_conv, s_gla = _sample_path(x_sample, state_ssd[0], state_conv[0], state_gla[0], w)
    return (yp, ys, p_ssd, p_conv, p_gla, s_ssd, s_conv, s_gla)
```

```python
import functools

import jax
import jax.numpy as jnp
from jax import lax
from jax.experimental import pallas as pl
from jax.experimental.pallas import tpu as pltpu

F32 = jnp.float32
BF16 = jnp.bfloat16

LANES = 128
SUBLANES = 8
VMEM_LIMIT_BYTES = 60 * 1024 * 1024

D_MODEL = 1024
N_META = 16
EPS = 1e-6
SSD_D_INNER = 2 * D_MODEL
SSD_HEAD_DIM = 64
SSD_HEADS = SSD_D_INNER // SSD_HEAD_DIM
SSD_GROUPS = 4
SSD_HPG = SSD_HEADS // SSD_GROUPS
SSD_D_STATE = 128
SSD_CONV = 4
SSD_CHUNK = 128
SSD_BC = 2 * SSD_GROUPS * SSD_D_STATE
SSD_CONV_DIM = SSD_D_INNER + SSD_BC
SSD_GROUP_W = SSD_HPG * SSD_HEAD_DIM
GLA_HEADS = 4
GLA_KEY_DIM = D_MODEL // 2
GLA_VAL_DIM = D_MODEL
GLA_DK = GLA_KEY_DIM // GLA_HEADS
GLA_DV = GLA_VAL_DIM // GLA_HEADS
GLA_GATE_RANK = 16
GLA_GATE_NORMALIZER = 16.0
GLA_CHUNK = 64
GLA_STEP_CHUNKS = 2
DECODE_STEP_SEQS = 4
PEER_HEADS = 8
PEER_N_KEYS = 128
PEER_N_EXPERTS = PEER_N_KEYS * PEER_N_KEYS
PEER_DQ = 256
PEER_TOPK = 16
PEER_SLABS_PER_STEP = 16
PEER_STEP_EXPERTS = PEER_SLABS_PER_STEP * PEER_N_KEYS
PEER_SCATTER_UNROLL = 16
PEER_PITCH = PEER_N_KEYS + 4
PEER_TOKEN_BLOCK = 256

_IN_SIZES = (SSD_D_INNER, SSD_CONV_DIM, SSD_HEADS, GLA_KEY_DIM, GLA_KEY_DIM, GLA_VAL_DIM,
             GLA_VAL_DIM, GLA_GATE_RANK, 2 * D_MODEL)
COL_Z = 0
COL_XS = COL_Z + SSD_D_INNER
COL_BC = COL_XS + SSD_D_INNER
COL_Q = COL_BC + SSD_BC
COL_K = COL_Q + GLA_KEY_DIM
COL_V = COL_K + GLA_KEY_DIM
COL_G = COL_V + GLA_VAL_DIM
COL_SMALL = COL_G + GLA_VAL_DIM
MXU_WIDTH = 256
SMALL_COLS = MXU_WIDTH
IN_COLS = COL_SMALL + SMALL_COLS
IN_COL_TILE = IN_COLS // 3
SMALL_GLR = SSD_HEADS
PROMPT_ZERO_ROWS = SSD_CHUNK - N_META


def _cparams(*sem):
    return pltpu.CompilerParams(dimension_semantics=sem, vmem_limit_bytes=VMEM_LIMIT_BYTES)


def _const_spec(shape):
    nd = len(shape)
    return pl.BlockSpec(shape, lambda *_: (0,) * nd, pipeline_mode=pl.Buffered(1))


def _row_tile(t, cap):
    tm = cap
    while t % tm:
        tm //= 2
    return tm


def _split3(x):
    hi = x.astype(BF16)
    r1 = x - hi.astype(F32)
    mid = r1.astype(BF16)
    lo = (r1 - mid.astype(F32)).astype(BF16)
    return hi, mid, lo


def _dot(a, b):
    return jnp.dot(a, b, preferred_element_type=F32)


def _dot_nt(a, b):
    return lax.dot_general(a, b, (((1,), (1,)), ((), ())), preferred_element_type=F32)


def _sel_right(sel01, x):
    hi, mid, lo = _split3(x)
    return _dot(hi, sel01) + _dot(mid, sel01) + _dot(lo, sel01)


def _sel_left(sel01, x):
    hi, mid, lo = _split3(x)
    return _dot(sel01, hi) + _dot(sel01, mid) + _dot(sel01, lo)


def _rms_scale(x):
    return x * lax.rsqrt(jnp.mean(x * x, axis=-1, keepdims=True) + EPS)


def _tril(n):
    r = lax.broadcasted_iota(jnp.int32, (n, n), 0)
    c = lax.broadcasted_iota(jnp.int32, (n, n), 1)
    return r >= c


def _norm_matmul_kernel(x_ref, nw_ref, w_ref, o_ref, xn_ref):
    @pl.when(pl.program_id(1) == 0)
    def _():
        xn_ref[...] = (_rms_scale(x_ref[...]) * nw_ref[...]).astype(BF16)

    o_ref[...] = _dot(xn_ref[...], w_ref[...])


def norm_matmul(x, norm_w, w, tn):
    t, d = x.shape
    n = w.shape[1]
    tm = _row_tile(t, 1024)
    assert n % tn == 0
    return pl.pallas_call(
        _norm_matmul_kernel,
        grid=(t // tm, n // tn),
        in_specs=[
            pl.BlockSpec((tm, d), lambda i, j: (i, 0)),
            pl.BlockSpec((1, d), lambda i, j: (0, 0)),
            pl.BlockSpec((d, tn), lambda i, j: (0, j)),
        ],
        out_specs=pl.BlockSpec((tm, tn), lambda i, j: (i, j)),
        out_shape=jax.ShapeDtypeStruct((t, n), F32),
        scratch_shapes=[pltpu.VMEM((tm, d), BF16)],
        compiler_params=_cparams("parallel", "arbitrary"),
        name="norm_in_proj",
    )(x, norm_w, w)


def _ssd_dt(small, dt_bias_ref, valid):
    lane = lax.broadcasted_iota(jnp.int32, small.shape, 1)
    dt = jax.nn.softplus(small + dt_bias_ref[...])
    keep = lane < SSD_HEADS
    if valid is not None:
        keep = jnp.logical_and(keep, valid)
    return jnp.where(keep, dt, 0.0)


def _ssd_gated_norm(y, z, norm_w_ref):
    y = y * jax.nn.silu(z)
    parts = []
    for g in range(SSD_GROUPS):
        parts.append(_rms_scale(y[:, g * SSD_GROUP_W:(g + 1) * SSD_GROUP_W]))
    return jnp.concatenate(parts, axis=1) * norm_w_ref[...]


def _ssd_prompt_kernel(z_ref, xs_ref, bc_ref, small_ref, conv_w_ref, conv_b_ref, dt_bias_ref,
                       a_log_ref, d_exp_ref, norm_w_ref, expand_ref, state0_ref, tail0_ref,
                       y_ref, state_ref, convtail_ref,
                       hist_ref, st_ref, yacc_ref, *, zero_rows):
    c = pl.program_id(1)
    nc = pl.num_programs(1)
    q = SSD_CHUNK

    @pl.when(c == 0)
    def _():
        hist_ref[0:SUBLANES, :] = tail0_ref[0]
        st_ref[...] = state0_ref[0].T

    hist_ref[SUBLANES:SUBLANES + q, 0:SSD_D_INNER] = xs_ref[...]
    hist_ref[SUBLANES:SUBLANES + q, SSD_D_INNER:SSD_CONV_DIM] = bc_ref[...]
    conv = conv_b_ref[...]
    for j in range(SSD_CONV):
        start = SUBLANES - (SSD_CONV - 1) + j
        conv = conv + conv_w_ref[j:j + 1, :] * hist_ref[start:start + q, :]
    xbc = jax.nn.silu(conv)
    tail = hist_ref[q:q + SUBLANES, :]
    hist_ref[0:SUBLANES, :] = tail
    convtail_ref[0] = tail

    xs = xbc[:, 0:SSD_D_INNER]
    bm = xbc[:, SSD_D_INNER:SSD_D_INNER + SSD_GROUPS * SSD_D_STATE].astype(BF16)
    cm = xbc[:, SSD_D_INNER + SSD_GROUPS * SSD_D_STATE:].astype(BF16)

    row = lax.broadcasted_iota(jnp.int32, (q, LANES), 0) + c * q
    dt = _ssd_dt(small_ref[...], dt_bias_ref, row >= zero_rows)
    da = dt * (-jnp.exp(a_log_ref[...]))
    tril = _tril(q)
    acum = _sel_left(tril.astype(BF16), da)
    acum_t = acum.T
    expand = expand_ref[...]
    dt_exp = _sel_right(expand, dt)
    acum_exp = _sel_right(expand, acum)
    acum_last = acum_exp[q - 1:q, :]
    xdt = xs * dt_exp
    xdt_bf = xdt.astype(BF16)
    x_end = (xdt * jnp.exp(acum_last - acum_exp)).astype(BF16)

    lane_lo = lax.broadcasted_iota(jnp.int32, (q, LANES), 1) < SSD_HEAD_DIM
    for g in range(SSD_GROUPS):
        bg = bm[:, g * SSD_D_STATE:(g + 1) * SSD_D_STATE]
        cg = cm[:, g * SSD_D_STATE:(g + 1) * SSD_D_STATE]
        cb = _dot_nt(cg, bg)
        for hp in range(SSD_HPG // 2):
            lo = (g * SSD_HPG + 2 * hp) * SSD_HEAD_DIM
            pair = []
            for k in range(2):
                h = g * SSD_HPG + 2 * hp + k
                diff = acum[:, h:h + 1] - acum_t[h:h + 1, :]
                seg = jnp.exp(jnp.where(tril, diff, -jnp.inf))
                pair.append(_dot((cb * seg).astype(BF16), xdt_bf[:, lo:lo + LANES]))
            yacc_ref[:, lo:lo + LANES] = jnp.where(lane_lo, pair[0], pair[1])

        gs = slice(g * SSD_GROUP_W, (g + 1) * SSD_GROUP_W)
        st_g = st_ref[:, gs]
        y_off = _dot(cg, st_g.astype(BF16)) * jnp.exp(acum_exp[:, gs])
        yacc_ref[:, gs] = yacc_ref[:, gs] + y_off
        st_ref[:, gs] = st_g * jnp.exp(acum_last[:, gs]) + _dot(bg.T, x_end[:, gs])

    y = yacc_ref[...] + d_exp_ref[...] * xs
    y_ref[0] = _ssd_gated_norm(y, z_ref[...], norm_w_ref).astype(BF16)

    @pl.when(c == nc - 1)
    def _():
        state_ref[0] = st_ref[...].T


def ssd_prompt(proj, batch, n_chunks, zero_rows, state0, tail0, w):
    q = SSD_CHUNK

    def rows(b, c):
        return b * n_chunks + c

    return pl.pallas_call(
        functools.partial(_ssd_prompt_kernel, zero_rows=zero_rows),
        grid=(batch, n_chunks),
        in_specs=[
            pl.BlockSpec((q, SSD_D_INNER), lambda b, c: (rows(b, c), COL_Z // SSD_D_INNER)),
            pl.BlockSpec((q, SSD_D_INNER), lambda b, c: (rows(b, c), COL_XS // SSD_D_INNER)),
            pl.BlockSpec((q, SSD_BC), lambda b, c: (rows(b, c), COL_BC // SSD_BC)),
            pl.BlockSpec((q, LANES), lambda b, c: (rows(b, c), COL_SMALL // LANES)),
            _const_spec((SSD_CONV, SSD_CONV_DIM)),
            _const_spec((1, SSD_CONV_DIM)),
            _const_spec((1, LANES)),
            _const_spec((1, LANES)),
            _const_spec((1, SSD_D_INNER)),
            _const_spec((1, SSD_D_INNER)),
            _const_spec((LANES, SSD_D_INNER)),
            _const_spec((1, SSD_D_INNER, SSD_D_STATE)),
            _const_spec((1, SUBLANES, SSD_CONV_DIM)),
        ],
        out_specs=[
            pl.BlockSpec((1, q, SSD_D_INNER), lambda b, c: (b, c, 0)),
            pl.BlockSpec((1, SSD_D_INNER, SSD_D_STATE), lambda b, c: (b, 0, 0)),
            pl.BlockSpec((1, SUBLANES, SSD_CONV_DIM), lambda b, c: (b, 0, 0)),
        ],
        out_shape=[
            jax.ShapeDtypeStruct((batch, n_chunks * q, SSD_D_INNER), BF16),
            jax.ShapeDtypeStruct((batch, SSD_D_INNER, SSD_D_STATE), F32),
            jax.ShapeDtypeStruct((batch, SUBLANES, SSD_CONV_DIM), F32),
        ],
        scratch_shapes=[
            pltpu.VMEM((SUBLANES + q, SSD_CONV_DIM), F32),
            pltpu.VMEM((SSD_D_STATE, SSD_D_INNER), F32),
            pltpu.VMEM((q, SSD_D_INNER), F32),
        ],
        compiler_params=_cparams("parallel", "arbitrary"),
        name="ssd_prompt",
    )(proj, proj, proj, proj, w["conv_w"], w["conv_b"], w["dt_bias"], w["a_log"], w["d_exp"],
      w["ssd_norm_w"], w["expand"], state0, tail0)


def _ssd_step_pre_kernel(xs_ref, bc_ref, small_ref, c0_ref, c1_ref, c2_ref, conv_w_ref, conv_b_ref,
                         dt_bias_ref, a_log_ref, expand_ref,
                         xact_ref, xdt_ref, decay_ref, bm_ref, cm_ref, convnew_ref):
    x = jnp.concatenate([xs_ref[...], bc_ref[...]], axis=1)
    hist = (c0_ref[...], c1_ref[...], c2_ref[...], x)
    conv = conv_b_ref[...]
    for j in range(SSD_CONV):
        conv = conv + conv_w_ref[j:j + 1, :] * hist[j]
    xbc = jax.nn.silu(conv)
    xs = xbc[:, 0:SSD_D_INNER]
    dt = _ssd_dt(small_ref[...], dt_bias_ref, None)
    da = dt * (-jnp.exp(a_log_ref[...]))
    expand = expand_ref[...]
    xact_ref[...] = xs
    xdt_ref[...] = xs * _sel_right(expand, dt)
    decay_ref[...] = jnp.exp(da)
    bm_ref[...] = xbc[:, SSD_D_INNER:SSD_D_INNER + SSD_GROUPS * SSD_D_STATE]
    cm_ref[...] = xbc[:, SSD_D_INNER + SSD_GROUPS * SSD_D_STATE:]
    convnew_ref[:, 0:SSD_CONV_DIM] = hist[1]
    convnew_ref[:, SSD_CONV_DIM:2 * SSD_CONV_DIM] = hist[2]
    convnew_ref[:, 2 * SSD_CONV_DIM:] = x


def ssd_step_pre(proj, conv_state, w):
    t = proj.shape[0]
    tm = _row_tile(t, 128)
    gw = SSD_GROUPS * SSD_D_STATE
    row = lambda width, col: pl.BlockSpec((tm, width), lambda i: (i, col))
    return pl.pallas_call(
        _ssd_step_pre_kernel,
        grid=(t // tm,),
        in_specs=[
            row(SSD_D_INNER, COL_XS // SSD_D_INNER),
            row(SSD_BC, COL_BC // SSD_BC),
            row(LANES, COL_SMALL // LANES),
            row(SSD_CONV_DIM, 0), row(SSD_CONV_DIM, 1), row(SSD_CONV_DIM, 2),
            _const_spec((SSD_CONV, SSD_CONV_DIM)),
            _const_spec((1, SSD_CONV_DIM)),
            _const_spec((1, LANES)),
            _const_spec((1, LANES)),
            _const_spec((LANES, SSD_D_INNER)),
        ],
        out_specs=[row(SSD_D_INNER, 0), row(SSD_D_INNER, 0), row(LANES, 0), row(gw, 0),
                   row(gw, 0), row((SSD_CONV - 1) * SSD_CONV_DIM, 0)],
        out_shape=[
            jax.ShapeDtypeStruct((t, SSD_D_INNER), F32),
            jax.ShapeDtypeStruct((t, SSD_D_INNER), F32),
            jax.ShapeDtypeStruct((t, LANES), F32),
            jax.ShapeDtypeStruct((t, gw), F32),
            jax.ShapeDtypeStruct((t, gw), F32),
            jax.ShapeDtypeStruct((t, (SSD_CONV - 1) * SSD_CONV_DIM), F32),
        ],
        compiler_params=_cparams("parallel"),
        name="ssd_step_pre",
    )(proj, proj, proj, conv_state, conv_state, conv_state, w["conv_w"], w["conv_b"], w["dt_bias"],
      w["a_log"], w["expand"])


def _dot_tn(a, b):
    return lax.dot_general(a, b, (((0,), (0,)), ((), ())), preferred_element_type=F32)


def _own_row(j):
    return lax.broadcasted_iota(jnp.int32, (SUBLANES, 1), 0) == j


def _ssd_step_state_kernel(s_ref, xdt_ref, dec_ref, bm_ref, cm_ref, snew_ref, y_ref):
    j = pl.program_id(1)

    @pl.when(j == 0)
    def _():
        y_ref[...] = jnp.zeros(y_ref.shape, F32)

    def group_rows(x):
        return jnp.concatenate(
            [x[:, g * SSD_D_STATE:(g + 1) * SSD_D_STATE] for g in range(SSD_GROUPS)], axis=0)

    group_of_lane = lax.broadcasted_iota(jnp.int32, (SUBLANES, SSD_D_INNER), 1) // SSD_GROUP_W
    b_rows = group_rows(bm_ref[...]).astype(BF16)
    y_new = jnp.zeros(y_ref.shape, F32)
    for u in range(DECODE_STEP_SEQS):
        row = j * DECODE_STEP_SEQS + u
        own = _own_row(row)
        dec = dec_ref[pl.ds(row, 1), :]
        xdt = jnp.where(own, xdt_ref[...], 0.0)
        x_rows = jnp.concatenate(
            [jnp.where(group_of_lane == g, xdt, 0.0) for g in range(SSD_GROUPS)], axis=0).astype(BF16)
        c_rows = group_rows(jnp.where(own, cm_ref[...], 0.0)).astype(BF16)
        decayed = [s_ref[u, h * SSD_HEAD_DIM:(h + 1) * SSD_HEAD_DIM, :] * dec[:, h:h + 1]
                   for h in range(SSD_HEADS)]
        s_new = jnp.concatenate(decayed, axis=0) + _dot_tn(x_rows, b_rows)
        snew_ref[u] = s_new
        y_rows = _dot_nt(c_rows, s_new.astype(BF16))
        y_new = y_new + jnp.concatenate(
            [y_rows[g * SUBLANES:(g + 1) * SUBLANES, g * SSD_GROUP_W:(g + 1) * SSD_GROUP_W]
             for g in range(SSD_GROUPS)], axis=1)
    y_ref[...] += y_new


def _decode_state_specs(rows, cols):
    steps = SUBLANES // DECODE_STEP_SEQS
    shape = (DECODE_STEP_SEQS, rows, cols)
    index = lambda i, j: (i * steps + j, 0, 0)
    return pl.BlockSpec(shape, index), pl.BlockSpec(shape, index)


def ssd_step_state(state, xdt, dec, bm, cm):
    t = state.shape[0]
    assert t % SUBLANES == 0
    gw = SSD_GROUPS * SSD_D_STATE
    rows = lambda width: pl.BlockSpec((SUBLANES, width), lambda i, j: (i, 0))
    st_in, st_out = _decode_state_specs(SSD_D_INNER, SSD_D_STATE)
    return pl.pallas_call(
        _ssd_step_state_kernel,
        grid=(t // SUBLANES, SUBLANES // DECODE_STEP_SEQS),
        in_specs=[st_in, rows(SSD_D_INNER), rows(LANES), rows(gw), rows(gw)],
        out_specs=[st_out, rows(SSD_D_INNER)],
        out_shape=[
            jax.ShapeDtypeStruct((t, SSD_D_INNER, SSD_D_STATE), F32),
            jax.ShapeDtypeStruct((t, SSD_D_INNER), F32),
        ],
        compiler_params=_cparams("parallel", "arbitrary"),
        name="ssd_step_state",
    )(state, xdt, dec, bm, cm)


def _ssd_step_post_kernel(y_ref, xact_ref, z_ref, d_exp_ref, norm_w_ref, o_ref):
    y = y_ref[...] + d_exp_ref[...] * xact_ref[...]
    o_ref[...] = _ssd_gated_norm(y, z_ref[...], norm_w_ref).astype(BF16)


def ssd_step_post(y, xact, proj, w):
    t = y.shape[0]
    tm = _row_tile(t, 128)
    row = lambda col: pl.BlockSpec((tm, SSD_D_INNER), lambda i: (i, col))
    return pl.pallas_call(
        _ssd_step_post_kernel,
        grid=(t // tm,),
        in_specs=[row(0), row(0), row(COL_Z // SSD_D_INNER), _const_spec((1, SSD_D_INNER)),
                  _const_spec((1, SSD_D_INNER))],
        out_specs=row(0),
        out_shape=jax.ShapeDtypeStruct((t, SSD_D_INNER), BF16),
        compiler_params=_cparams("parallel"),
        name="ssd_step_post",
    )(y, xact, proj, w["d_exp"], w["ssd_norm_w"])


def _gla_gate(small, gate_up_ref, gate_b_ref):
    pre = _dot(small.astype(BF16), gate_up_ref[...]) + gate_b_ref[...]
    return jax.nn.log_sigmoid(pre) / GLA_GATE_NORMALIZER


def _gla_out_norm(o, g, norm_w_ref):
    parts = []
    for h in range(GLA_HEADS):
        parts.append(_rms_scale(o[:, h * GLA_DV:(h + 1) * GLA_DV]))
    return jnp.concatenate(parts, axis=1) * norm_w_ref[...] * jax.nn.silu(g)


def _gla_prompt_kernel(q_ref, k_ref, v_ref, g_ref, small_ref, gate_up_ref, gate_b_ref, norm_w_ref,
                       state0_ref, o_ref, state_ref, s_ref, *, zero_rows):
    c = pl.program_id(1)
    nc = pl.num_programs(1)
    q = GLA_CHUNK
    rows = q * GLA_STEP_CHUNKS

    @pl.when(c == 0)
    def _():
        s_ref[...] = state0_ref[0]

    valid = (lax.broadcasted_iota(jnp.int32, (rows, GLA_KEY_DIM), 0) + c * rows) >= zero_rows
    gk = jnp.where(valid, _gla_gate(small_ref[...], gate_up_ref, gate_b_ref), 0.0)
    kk = jnp.where(valid, k_ref[...], 0.0)
    r = lax.broadcasted_iota(jnp.int32, (rows, rows), 0)
    cc = lax.broadcasted_iota(jnp.int32, (rows, rows), 1)
    chunk_tril = jnp.logical_and(r >= cc, r // q == cc // q)
    bcum = _sel_left(chunk_tril.astype(BF16), gk)
    lasts = [bcum[(i + 1) * q - 1:(i + 1) * q, :] for i in range(GLA_STEP_CHUNKS)]
    last_rows = jnp.concatenate([jnp.broadcast_to(x, (q, GLA_KEY_DIM)) for x in lasts], axis=0)
    qe = q_ref[...] * jnp.exp(bcum) * (GLA_DK ** -0.5)
    ke = kk * jnp.exp(-bcum)
    kend = kk * jnp.exp(last_rows - bcum)
    v_all = v_ref[...]
    vv = v_all.astype(BF16)
    chunk_of_row = lax.broadcasted_iota(jnp.int32, (rows, GLA_VAL_DIM), 0) // q
    v_chunk = [jnp.where(chunk_of_row == i, v_all, 0.0).astype(BF16) for i in range(GLA_STEP_CHUNKS)]
    outs = [[] for _ in range(GLA_STEP_CHUNKS)]
    for h in range(GLA_HEADS):
        ks = slice(h * GLA_DK, (h + 1) * GLA_DK)
        vs = slice(h * GLA_DV, (h + 1) * GLA_DV)
        qe_h = qe[:, ks].astype(BF16)
        att = jnp.where(chunk_tril, _dot_nt(qe_h, ke[:, ks].astype(BF16)), 0.0)
        o_intra = _dot(att.astype(BF16), vv[:, vs])
        kend_t = kend[:, ks].T.astype(BF16)
        s_h = s_ref[ks, :]
        for i in range(GLA_STEP_CHUNKS):
            rs = slice(i * q, (i + 1) * q)
            outs[i].append(o_intra[rs, :] + _dot(qe_h[rs, :], s_h.astype(BF16)))
            decay_col = jnp.broadcast_to(jnp.exp(lasts[i][:, ks]), (GLA_DK, GLA_DK)).T
            s_h = (jnp.concatenate([decay_col] * (GLA_DV // GLA_DK), axis=1) * s_h
                   + _dot(kend_t, v_chunk[i][:, vs]))
        s_ref[ks, :] = s_h
    o = jnp.concatenate([jnp.concatenate(o_i, axis=1) for o_i in outs], axis=0)
    o_ref[0] = _gla_out_norm(o, g_ref[...], norm_w_ref).astype(BF16)

    @pl.when(c == nc - 1)
    def _():
        state_ref[0] = s_ref[...]


def gla_prompt(proj, batch, n_rows, zero_rows, state0, w):
    rows = GLA_CHUNK * GLA_STEP_CHUNKS
    assert n_rows % rows == 0
    n_steps = n_rows // rows

    def blk(b, c):
        return b * n_steps + c

    return pl.pallas_call(
        functools.partial(_gla_prompt_kernel, zero_rows=zero_rows),
        grid=(batch, n_steps),
        in_specs=[
            pl.BlockSpec((rows, GLA_KEY_DIM), lambda b, c: (blk(b, c), COL_Q // GLA_KEY_DIM)),
            pl.BlockSpec((rows, GLA_KEY_DIM), lambda b, c: (blk(b, c), COL_K // GLA_KEY_DIM)),
            pl.BlockSpec((rows, GLA_VAL_DIM), lambda b, c: (blk(b, c), COL_V // GLA_VAL_DIM)),
            pl.BlockSpec((rows, GLA_VAL_DIM), lambda b, c: (blk(b, c), COL_G // GLA_VAL_DIM)),
            pl.BlockSpec((rows, LANES), lambda b, c: (blk(b, c), COL_SMALL // LANES)),
            _const_spec((LANES, GLA_KEY_DIM)),
            _const_spec((1, GLA_KEY_DIM)),
            _const_spec((1, GLA_VAL_DIM)),
            _const_spec((1, GLA_KEY_DIM, GLA_DV)),
        ],
        out_specs=[
            pl.BlockSpec((1, rows, GLA_VAL_DIM), lambda b, c: (b, c, 0)),
            pl.BlockSpec((1, GLA_KEY_DIM, GLA_DV), lambda b, c: (b, 0, 0)),
        ],
        out_shape=[
            jax.ShapeDtypeStruct((batch, n_steps * rows, GLA_VAL_DIM), BF16),
            jax.ShapeDtypeStruct((batch, GLA_KEY_DIM, GLA_DV), F32),
        ],
        scratch_shapes=[pltpu.VMEM((GLA_KEY_DIM, GLA_DV), F32)],
        compiler_params=_cparams("parallel", "arbitrary"),
        name="gla_prompt",
    )(proj, proj, proj, proj, proj, w["gate_up"], w["gate_b"], w["gla_norm_w"], state0)


def _gla_step_kernel(s_ref, q_ref, k_ref, v_ref, g_ref, small_ref, gate_up_ref, gate_b_ref, norm_w_ref,
                     snew_ref, o_ref, dec_ref, acc_ref):
    j = pl.program_id(1)

    @pl.when(j == 0)
    def _():
        dec_ref[...] = jnp.exp(_gla_gate(small_ref[...], gate_up_ref, gate_b_ref))
        acc_ref[...] = jnp.zeros(acc_ref.shape, F32)

    head_of_lane = lax.broadcasted_iota(jnp.int32, (SUBLANES, GLA_KEY_DIM), 1) // GLA_DK

    def per_head_rows(x):
        return jnp.concatenate([jnp.where(head_of_lane == h, x, 0.0) for h in range(GLA_HEADS)], axis=0)

    vv = v_ref[...]
    v_rows = jnp.concatenate(
        [vv[:, h * GLA_DV:(h + 1) * GLA_DV] for h in range(GLA_HEADS)], axis=0).astype(BF16)
    o_new = jnp.zeros(acc_ref.shape, F32)
    for u in range(DECODE_STEP_SEQS):
        own = _own_row(j * DECODE_STEP_SEQS + u)
        k_rows = per_head_rows(jnp.where(own, k_ref[...], 0.0)).astype(BF16)
        q_rows = per_head_rows(jnp.where(own, q_ref[...] * (GLA_DK ** -0.5), 0.0)).astype(BF16)
        pieces = [p.astype(F32) for p in _split3(jnp.where(own, dec_ref[...], 0.0))]
        d_rows = jnp.concatenate(pieces + [jnp.zeros_like(pieces[0])], axis=0).astype(BF16)
        decay = _dot_tn(d_rows, jnp.ones((d_rows.shape[0], GLA_DV), BF16))
        s_new = s_ref[u] * decay + _dot_tn(k_rows, v_rows)
        snew_ref[u] = s_new
        o_rows = _dot(q_rows, s_new.astype(BF16))
        o_new = o_new + jnp.concatenate(
            [o_rows[h * SUBLANES:(h + 1) * SUBLANES, :] for h in range(GLA_HEADS)], axis=1)
    acc_ref[...] += o_new

    @pl.when(j == pl.num_programs(1) - 1)
    def _():
        o_ref[...] = _gla_out_norm(acc_ref[...], g_ref[...], norm_w_ref)


def gla_step(state, proj, w):
    t = state.shape[0]
    assert t % SUBLANES == 0
    rows = lambda width, col: pl.BlockSpec((SUBLANES, width), lambda i, j: (i, col))
    st_in, st_out = _decode_state_specs(GLA_KEY_DIM, GLA_DV)
    return pl.pallas_call(
        _gla_step_kernel,
        grid=(t // SUBLANES, SUBLANES // DECODE_STEP_SEQS),
        in_specs=[st_in, rows(GLA_KEY_DIM, COL_Q // GLA_KEY_DIM), rows(GLA_KEY_DIM, COL_K // GLA_KEY_DIM),
                  rows(GLA_VAL_DIM, COL_V // GLA_VAL_DIM), rows(GLA_VAL_DIM, COL_G // GLA_VAL_DIM),
                  rows(LANES, COL_SMALL // LANES), _const_spec((LANES, GLA_KEY_DIM)),
                  _const_spec((1, GLA_KEY_DIM)), _const_spec((1, GLA_VAL_DIM))],
        out_specs=[st_out, rows(GLA_VAL_DIM, 0)],
        out_shape=[
            jax.ShapeDtypeStruct((t, GLA_KEY_DIM, GLA_DV), F32),
            jax.ShapeDtypeStruct((t, GLA_VAL_DIM), F32),
        ],
        scratch_shapes=[pltpu.VMEM((SUBLANES, GLA_KEY_DIM), F32), pltpu.VMEM((SUBLANES, GLA_VAL_DIM), F32)],
        compiler_params=_cparams("parallel", "arbitrary"),
        name="gla_step",
    )(state, proj, proj, proj, proj, proj, w["gate_up"], w["gate_b"], w["gla_norm_w"])


def _merge_kernel(x_ref, ys_ref, og_ref, nw_ref, wm_ref, wso_ref, wgo_ref, wout_ref, o_ref):
    x = x_ref[...]
    xn = (_rms_scale(x) * nw_ref[...]).astype(BF16)
    gates = jax.nn.sigmoid(_dot(xn, wm_ref[...]))
    y_ssd = _dot(ys_ref[...].astype(BF16), wso_ref[...])
    y_gla = _dot(og_ref[...].astype(BF16), wgo_ref[...])
    mix = gates[:, 0:D_MODEL] * y_ssd + gates[:, D_MODEL:] * y_gla
    o_ref[...] = x + _dot(mix.astype(BF16), wout_ref[...])


def merge(x, y_ssd, o_gla, w):
    t = x.shape[0]
    tm = _row_tile(t, 512)
    row = lambda width: pl.BlockSpec((tm, width), lambda i: (i, 0))
    return pl.pallas_call(
        _merge_kernel,
        grid=(t // tm,),
        in_specs=[row(D_MODEL), row(SSD_D_INNER), row(GLA_VAL_DIM), _const_spec((1, D_MODEL)),
                  _const_spec((D_MODEL, 2 * D_MODEL)), _const_spec((SSD_D_INNER, D_MODEL)),
                  _const_spec((GLA_VAL_DIM, D_MODEL)), _const_spec((D_MODEL, D_MODEL))],
        out_specs=row(D_MODEL),
        out_shape=jax.ShapeDtypeStruct((t, D_MODEL), F32),
        compiler_params=_cparams("parallel"),
        name="merge",
    )(x, y_ssd, o_gla, w["norm1_w"], w["w_merge"], w["w_ssd_out"], w["w_gla_out"], w["w_out"])


def _merge_exchange_pairs(n):
    pairs = []
    p = n // 2
    while p > 0:
        q, r, d = n // 2, 0, p
        while d > 0:
            pairs += [(i, i + d) for i in range(n - d) if (i & p) == r]
            d, q, r = q - p, q // 2, p
        p //= 2
    return pairs


_SORT16 = _merge_exchange_pairs(PEER_TOPK)
_BITONIC16 = [(i, i + d) for d in (8, 4, 2, 1) for i in range(PEER_TOPK) if (i & d) == 0]


def _exchange(v, ids, i, j):
    if ids is None:
        v[i], v[j] = jnp.maximum(v[i], v[j]), jnp.minimum(v[i], v[j])
    else:
        ge = v[i] >= v[j]
        v[i], v[j] = jnp.where(ge, v[i], v[j]), jnp.where(ge, v[j], v[i])
        ids[i], ids[j] = jnp.where(ge, ids[i], ids[j]), jnp.where(ge, ids[j], ids[i])


def _merge_top16(v, ids, w, wids):
    n, m = len(v), len(w)
    for i in range(n - m, n):
        o = n - 1 - i
        if ids is None:
            v[i] = jnp.maximum(v[i], w[o])
        else:
            ge = v[i] >= w[o]
            v[i] = jnp.where(ge, v[i], w[o])
            ids[i] = jnp.where(ge, ids[i], wids[o])
    for i, j in _BITONIC16:
        _exchange(v, ids, i, j)


def _top16_rows(x, with_ids):
    n = x.shape[0] // SUBLANES
    assert n == PEER_TOPK
    v = [x[SUBLANES * r:SUBLANES * (r + 1), :] for r in range(n)]
    ids = None
    if with_ids:
        row = lax.broadcasted_iota(jnp.int32, v[0].shape, 0).astype(F32)
        ids = [row + float(SUBLANES * r) for r in range(n)]
    for i, j in _SORT16:
        _exchange(v, ids, i, j)
    shift = SUBLANES // 2
    while shift:
        w = [pltpu.roll(a, shift, 0) for a in v]
        wids = [pltpu.roll(a, shift, 0) for a in ids] if with_ids else None
        _merge_top16(v, ids, w, wids)
        shift //= 2
    return v, ids


def _peer_kernel(x_ref, nw_ref, fnw_ref, wqt_ref, keys_ref, u_ref, v_ref, o_ref,
                 hn_ref, qt_ref, rw_ref, s2t_ref, svk_ref, sik_ref, stat_ref, g1k_ref, ids_ref, g1tm_ref,
                 kept8_ref, e8_ref, hid_ref, acc_ref):
    s = pl.program_id(1)
    ns = pl.num_programs(1)
    tb = x_ref.shape[0]
    nk = PEER_N_KEYS
    nj = PEER_HEADS * PEER_TOPK

    @pl.when(s == 0)
    def _select():
        acc_ref[...] = jnp.zeros(acc_ref.shape, F32)
        hid_ref[...] = jnp.zeros(hid_ref.shape, F32)
        hn_ref[...] = (_rms_scale(x_ref[...]) * nw_ref[...]).astype(BF16)
        stat_ref[...] = jnp.zeros(stat_ref.shape, F32)
        qt_ref[...] = _dot_nt(wqt_ref[...], hn_ref[...]).astype(BF16)
        for h in range(PEER_HEADS):
            for c in range(2):
                r0 = (h * 2 + c) * (PEER_DQ // 2)
                sc_t = _dot(keys_ref[h * 2 + c], qt_ref[r0:r0 + PEER_DQ // 2, :])
                for g in range(tb // LANES):
                    ls = slice(g * LANES, (g + 1) * LANES)
                    vals, ids = _top16_rows(sc_t[:, ls], c == 0)
                    for k in range(PEER_TOPK):
                        svk_ref[c, k, h:h + 1, ls] = vals[k][0:1, :]
                        if c == 0:
                            sik_ref[k, h:h + 1, ls] = ids[k][0:1, :]
                if c == 1:
                    s2t_ref[h] = sc_t

        pairs = ([(0, k2) for k2 in range(PEER_TOPK)]
                 + [(k1, k2) for k1 in range(1, PEER_TOPK // 2) for k2 in range(PEER_TOPK // (k1 + 1))]
                 + [(k1, 0) for k1 in range(PEER_TOPK // 2, PEER_TOPK)])
        for g in range(tb // LANES):
            ls = slice(g * LANES, (g + 1) * LANES)
            a = [svk_ref[0, k, :, ls] for k in range(PEER_TOPK)]
            b = [svk_ref[1, k, :, ls] for k in range(PEER_TOPK)]
            best = [a[0] + b[k2] for k2 in range(PEER_TOPK)]
            for k1 in range(1, PEER_TOPK // 2):
                _merge_top16(best, None, [a[k1] + b[k2] for k2 in range(PEER_TOPK // (k1 + 1))], None)
            _merge_top16(best, None, [a[k1] + b[0] for k1 in range(PEER_TOPK // 2, PEER_TOPK)], None)
            tau, top = best[PEER_TOPK - 1], best[0]
            z = jnp.zeros_like(tau)
            for k1, k2 in pairs:
                cv = a[k1] + b[k2]
                z = z + jnp.where(cv >= tau, jnp.exp(cv - top), 0.0)
            stat_ref[0:PEER_HEADS, ls] = tau
            stat_ref[PEER_HEADS:2 * PEER_HEADS, ls] = b[0]
            inv_z = 1.0 / z
            for k in range(PEER_TOPK):
                g1k_ref[k, :, ls] = jnp.exp(a[k] - a[0]) * inv_z

        ids_ref[...] = sik_ref[...].reshape(nj, tb).T
        g1tm_ref[...] = g1k_ref[...].reshape(nj, tb).T
        def count_kept(h, carry):
            sc2 = s2t_ref[h]
            tau_t = stat_ref[pl.ds(h, 1), :]
            kept = jnp.zeros(sc2.shape, F32)
            for k1 in range(PEER_TOPK):
                kept = jnp.where((sc2 + svk_ref[0, k1, pl.ds(h, 1), :]) >= tau_t, float(k1 + 1), kept)
            e2 = jnp.exp(sc2 - stat_ref[pl.ds(PEER_HEADS + h, 1), :])
            kept8_ref[pl.ds(h, tb, stride=PEER_HEADS), :] = kept.T
            e8_ref[pl.ds(h, tb, stride=PEER_HEADS), :] = e2.T
            return carry

        lax.fori_loop(0, PEER_HEADS, count_kept, 0)

        key1 = lax.broadcasted_iota(jnp.int32, (nk, nk), 0).astype(F32)

        def scatter(t, carry):
            hb = pl.multiple_of(t * PEER_HEADS, PEER_HEADS)
            kept8 = kept8_ref[pl.ds(hb, PEER_HEADS), :]
            e8 = e8_ref[pl.ds(hb, PEER_HEADS), :]
            rows = jnp.concatenate(
                [jnp.where(kept8 > float(k1), e8, 0.0) for k1 in range(PEER_TOPK)], axis=0).astype(BF16)
            place = jnp.where(key1 == ids_ref[pl.ds(t, 1), :], g1tm_ref[pl.ds(t, 1), :], 0.0)
            rw_ref[pl.ds(t * PEER_PITCH, nk), :] = _dot(place.astype(BF16), rows)
            return carry

        lax.fori_loop(0, tb, scatter, 0, unroll=PEER_SCATTER_UNROLL)

    prev = hid_ref[...]
    act = 0.5 * prev * (1.0 + lax.erf(prev * (2.0 ** -0.5)))
    slab0 = jnp.maximum(s - 1, 0) * PEER_SLABS_PER_STEP
    wts = jnp.concatenate(
        [rw_ref[pl.ds(slab0 + i, tb, stride=PEER_PITCH), :] for i in range(PEER_SLABS_PER_STEP)], axis=1)
    gated = (act * wts).astype(BF16)
    hn = hn_ref[...]
    half = PEER_STEP_EXPERTS // 2
    for c in range(2):
        hid_ref[:, c * half:(c + 1) * half] = _dot_nt(hn, u_ref[c * half:(c + 1) * half, :])
    half = D_MODEL // 2
    for c in range(2):
        acc_ref[:, c * half:(c + 1) * half] += _dot(gated, v_ref[:, c * half:(c + 1) * half])

    @pl.when(s == ns - 1)
    def _():
        o_ref[...] = _rms_scale(x_ref[...] + acc_ref[...]) * fnw_ref[...]


def peer_final(x, w):
    t = x.shape[0]
    tb = _row_tile(t, PEER_TOKEN_BLOCK)
    assert tb % LANES == 0
    nblk = PEER_N_EXPERTS // PEER_STEP_EXPERTS
    return pl.pallas_call(
        _peer_kernel,
        grid=(t // tb, nblk + 1),
        in_specs=[
            pl.BlockSpec((tb, D_MODEL), lambda i, s: (i, 0)),
            _const_spec((1, D_MODEL)),
            _const_spec((1, D_MODEL)),
            _const_spec((PEER_HEADS * PEER_DQ, D_MODEL)),
            _const_spec((2 * PEER_HEADS, PEER_N_KEYS, PEER_DQ // 2)),
            pl.BlockSpec((PEER_STEP_EXPERTS, D_MODEL), lambda i, s: (jnp.minimum(s, nblk - 1), 0)),
            pl.BlockSpec((PEER_STEP_EXPERTS, D_MODEL), lambda i, s: (jnp.maximum(s - 1, 0), 0)),
        ],
        out_specs=pl.BlockSpec((tb, D_MODEL), lambda i, s: (i, 0)),
        out_shape=jax.ShapeDtypeStruct((t, D_MODEL), F32),
        scratch_shapes=[
            pltpu.VMEM((tb, D_MODEL), BF16),
            pltpu.VMEM((PEER_HEADS * PEER_DQ, tb), BF16),
            pltpu.VMEM((tb * PEER_PITCH, PEER_N_KEYS), F32),
            pltpu.VMEM((PEER_HEADS, PEER_N_KEYS, tb), F32),
            pltpu.VMEM((2, PEER_TOPK, PEER_HEADS, tb), F32),
            pltpu.VMEM((PEER_TOPK, PEER_HEADS, tb), F32),
            pltpu.VMEM((LANES, tb), F32),
            pltpu.VMEM((PEER_TOPK, PEER_HEADS, tb), F32),
            pltpu.VMEM((tb, PEER_HEADS * PEER_TOPK), F32),
            pltpu.VMEM((tb, PEER_HEADS * PEER_TOPK), F32),
            pltpu.VMEM((tb * PEER_HEADS, PEER_N_KEYS), F32),
            pltpu.VMEM((tb * PEER_HEADS, PEER_N_KEYS), F32),
            pltpu.VMEM((tb, PEER_STEP_EXPERTS), F32),
            pltpu.VMEM((tb, D_MODEL), F32),
        ],
        compiler_params=_cparams("parallel", "arbitrary"),
        name="peer_final",
    )(x, w["norm2_w"], w["final_norm_w"], w["peer_wq_t"], w["peer_keys"], w["peer_u"], w["peer_v"])


def _prepare_weights(norm1_w, w_in, conv_w, conv_b, dt_bias, a_log, d_skip, ssd_norm_w, w_ssd_out,
                     gla_gate_up, gla_gate_b, gla_norm_w, w_gla_out, w_out, norm2_w, peer_w_q,
                     peer_sub_keys, peer_u, peer_v, final_norm_w):
    offs = [0]
    for sz in _IN_SIZES:
        offs.append(offs[-1] + sz)
    z0, xbc0, dt0, q0, _, _, _, glr0, mg0, end = offs
    small = jnp.concatenate(
        [w_in[:, dt0:q0], w_in[:, glr0:mg0],
         jnp.zeros((D_MODEL, SMALL_COLS - SSD_HEADS - GLA_GATE_RANK), w_in.dtype)], axis=1)
    w_proj = jnp.concatenate([w_in[:, z0:dt0], w_in[:, q0:glr0], small], axis=1).astype(BF16)
    assert w_proj.shape[1] == IN_COLS

    def lane_pad(v):
        return jnp.pad(v, (0, LANES - v.shape[0])).reshape(1, LANES)

    head_of_col = jnp.arange(SSD_D_INNER) // SSD_HEAD_DIM
    expand = (jnp.arange(LANES)[:, None] == head_of_col[None, :]).astype(BF16)
    gate_up = jnp.zeros((LANES, GLA_KEY_DIM), F32).at[SMALL_GLR:SMALL_GLR + GLA_GATE_RANK].set(gla_gate_up)
    keys = jnp.transpose(peer_sub_keys, (1, 0, 2, 3)).reshape(2 * PEER_HEADS, PEER_N_KEYS, PEER_DQ // 2)
    return {
        "norm1_w": norm1_w.reshape(1, D_MODEL),
        "w_proj": w_proj,
        "w_merge": w_in[:, mg0:end].astype(BF16),
        "conv_w": conv_w,
        "conv_b": conv_b.reshape(1, SSD_CONV_DIM),
        "dt_bias": lane_pad(dt_bias),
        "a_log": lane_pad(a_log),
        "d_exp": jnp.repeat(d_skip, SSD_HEAD_DIM).reshape(1, SSD_D_INNER),
        "ssd_norm_w": ssd_norm_w.reshape(1, SSD_D_INNER),
        "expand": expand,
        "w_ssd_out": w_ssd_out.astype(BF16),
        "gate_up": gate_up.astype(BF16),
        "gate_b": gla_gate_b.reshape(1, GLA_KEY_DIM),
        "gla_norm_w": jnp.tile(gla_norm_w, GLA_HEADS).reshape(1, GLA_VAL_DIM),
        "w_gla_out": w_gla_out.astype(BF16),
        "w_out": w_out.astype(BF16),
        "norm2_w": norm2_w.reshape(1, D_MODEL),
        "peer_wq_t": peer_w_q.T.astype(BF16),
        "peer_keys": keys.astype(BF16),
        "peer_u": peer_u.astype(BF16),
        "peer_v": peer_v.astype(BF16),
        "final_norm_w": final_norm_w.reshape(1, D_MODEL),
    }


def _prompt_path(x_prompt, meta_tokens, w):
    b, seq, _ = x_prompt.shape
    assert seq % SSD_CHUNK == 0
    meta_rows = jnp.concatenate(
        [jnp.zeros((PROMPT_ZERO_ROWS, D_MODEL), x_prompt.dtype), meta_tokens.astype(x_prompt.dtype)], axis=0)
    meta_proj = norm_matmul(meta_rows, w["norm1_w"], w["w_proj"], IN_COL_TILE)
    zero_ssd = jnp.zeros((1, SSD_D_INNER, SSD_D_STATE), F32)
    zero_tail = jnp.zeros((1, SUBLANES, SSD_CONV_DIM), F32)
    zero_gla = jnp.zeros((1, GLA_KEY_DIM, GLA_DV), F32)
    _, ssd0, tail0 = ssd_prompt(meta_proj, 1, 1, PROMPT_ZERO_ROWS, zero_ssd, zero_tail, w)
    _, gla0 = gla_prompt(meta_proj, 1, SSD_CHUNK, PROMPT_ZERO_ROWS, zero_gla, w)
    proj = norm_matmul(x_prompt.reshape(b * seq, D_MODEL), w["norm1_w"], w["w_proj"], IN_COL_TILE)
    y_ssd, st_ssd, conv_tail = ssd_prompt(proj, b, seq // SSD_CHUNK, 0, ssd0, tail0, w)
    o_gla, st_gla = gla_prompt(proj, b, seq, 0, gla0, w)
    x1 = merge(x_prompt.reshape(b * seq, D_MODEL), y_ssd.reshape(b * seq, SSD_D_INNER),
               o_gla.reshape(b * seq, GLA_VAL_DIM), w)
    y = peer_final(x1, w).reshape(b, seq, D_MODEL)
    return (y,
            st_ssd.reshape(1, b, SSD_HEADS, SSD_HEAD_DIM, SSD_D_STATE),
            conv_tail[:, SUBLANES - (SSD_CONV - 1):, :][None],
            st_gla.reshape(1, b, GLA_HEADS, GLA_DK, GLA_DV))


def _sample_path(x_sample, state_ssd, state_conv, state_gla, w):
    b = x_sample.shape[0]
    x = x_sample.reshape(b, D_MODEL)
    proj = norm_matmul(x, w["norm1_w"], w["w_proj"], IN_COL_TILE)
    xact, xdt, dec, bm, cm, conv_new = ssd_step_pre(
        proj, state_conv.reshape(b, (SSD_CONV - 1) * SSD_CONV_DIM), w)
    st_ssd, y = ssd_step_state(state_ssd.reshape(b, SSD_D_INNER, SSD_D_STATE), xdt, dec, bm, cm)
    y_ssd = ssd_step_post(y, xact, proj, w)
    st_gla, o_gla = gla_step(state_gla.reshape(b, GLA_KEY_DIM, GLA_DV), proj, w)
    x1 = merge(x, y_ssd, o_gla, w)
    y = peer_final(x1, w).reshape(b, 1, D_MODEL)
    return (y,
            st_ssd.reshape(1, b, SSD_HEADS, SSD_HEAD_DIM, SSD_D_STATE),
            conv_new.reshape(1, b, SSD_CONV - 1, SSD_CONV_DIM),
            st_gla.reshape(1, b, GLA_HEADS, GLA_DK, GLA_DV))


def kernel(x_prompt, x_sample, state_ssd, state_conv, state_gla, meta_tokens, norm1_w, w_in, conv_w, conv_b,
           dt_bias, a_log, d_skip, ssd_norm_w, w_ssd_out, gla_gate_up, gla_gate_b, gla_norm_w, w_gla_out,
           w_out, norm2_w, peer_w_q, peer_sub_keys, peer_u, peer_v, final_norm_w):
    layer = (norm1_w, w_in, conv_w, conv_b, dt_bias, a_log, d_skip, ssd_norm_w, w_ssd_out, gla_gate_up,
             gla_gate_b, gla_norm_w, w_gla_out, w_out, norm2_w, peer_w_q, peer_sub_keys, peer_u, peer_v)
    assert all(p.shape[0] == 1 for p in layer), "single-layer step"
    w = _prepare_weights(*[p[0] for p in layer], final_norm_w)
    yp, p_ssd, p_conv, p_gla = _prompt_path(x_prompt, meta_tokens, w)
    ys, s_ssd, s_conv, s_gla = _sample_path(x_sample, state_ssd[0], state_conv[0], state_gla[0], w)
    return (yp, ys, p_ssd, p_conv, p_gla, s_ssd, s_conv, s_gla)
```

```python
import functools

import jax
import jax.numpy as jnp
from jax import lax
from jax.experimental import pallas as pl
from jax.experimental.pallas import tpu as pltpu

F32 = jnp.float32
BF16 = jnp.bfloat16

LANES = 128
SUBLANES = 8
VMEM_LIMIT_BYTES = 60 * 1024 * 1024

D_MODEL = 1024
N_META = 16
EPS = 1e-6
SSD_D_INNER = 2 * D_MODEL
SSD_HEAD_DIM = 64
SSD_HEADS = SSD_D_INNER // SSD_HEAD_DIM
SSD_GROUPS = 4
SSD_HPG = SSD_HEADS // SSD_GROUPS
SSD_D_STATE = 128
SSD_CONV = 4
SSD_CHUNK = 128
SSD_BC = 2 * SSD_GROUPS * SSD_D_STATE
SSD_CONV_DIM = SSD_D_INNER + SSD_BC
SSD_GROUP_W = SSD_HPG * SSD_HEAD_DIM
GLA_HEADS = 4
GLA_KEY_DIM = D_MODEL // 2
GLA_VAL_DIM = D_MODEL
GLA_DK = GLA_KEY_DIM // GLA_HEADS
GLA_DV = GLA_VAL_DIM // GLA_HEADS
GLA_GATE_RANK = 16
GLA_GATE_NORMALIZER = 16.0
GLA_CHUNK = 64
GLA_STEP_CHUNKS = 2
DECODE_STEP_SEQS = 8
PEER_HEADS = 8
PEER_N_KEYS = 128
PEER_N_EXPERTS = PEER_N_KEYS * PEER_N_KEYS
PEER_DQ = 256
PEER_TOPK = 16
PEER_SLABS_PER_STEP = 16
PEER_STEP_EXPERTS = PEER_SLABS_PER_STEP * PEER_N_KEYS
PEER_SCATTER_UNROLL = 16
PEER_PITCH = PEER_N_KEYS + 4
PEER_TOKEN_BLOCK = 256

_IN_SIZES = (SSD_D_INNER, SSD_CONV_DIM, SSD_HEADS, GLA_KEY_DIM, GLA_KEY_DIM, GLA_VAL_DIM,
             GLA_VAL_DIM, GLA_GATE_RANK, 2 * D_MODEL)
COL_Z = 0
COL_XS = COL_Z + SSD_D_INNER
COL_BC = COL_XS + SSD_D_INNER
COL_Q = COL_BC + SSD_BC
COL_K = COL_Q + GLA_KEY_DIM
COL_V = COL_K + GLA_KEY_DIM
COL_G = COL_V + GLA_VAL_DIM
COL_SMALL = COL_G + GLA_VAL_DIM
MXU_WIDTH = 256
SMALL_COLS = MXU_WIDTH
IN_COLS = COL_SMALL + SMALL_COLS
IN_COL_TILE = IN_COLS // 3
SMALL_GLR = SSD_HEADS
PROMPT_ZERO_ROWS = SSD_CHUNK - N_META


def _cparams(*sem):
    return pltpu.CompilerParams(dimension_semantics=sem, vmem_limit_bytes=VMEM_LIMIT_BYTES)


def _const_spec(shape):
    nd = len(shape)
    return pl.BlockSpec(shape, lambda *_: (0,) * nd, pipeline_mode=pl.Buffered(1))


def _row_tile(t, cap):
    tm = cap
    while t % tm:
        tm //= 2
    return tm


def _split3(x):
    hi = x.astype(BF16)
    r1 = x - hi.astype(F32)
    mid = r1.astype(BF16)
    lo = (r1 - mid.astype(F32)).astype(BF16)
    return hi, mid, lo


def _dot(a, b):
    return jnp.dot(a, b, preferred_element_type=F32)


def _dot_nt(a, b):
    return lax.dot_general(a, b, (((1,), (1,)), ((), ())), preferred_element_type=F32)


def _sel_right(sel01, x):
    hi, mid, lo = _split3(x)
    return _dot(hi, sel01) + _dot(mid, sel01) + _dot(lo, sel01)


def _sel_left(sel01, x):
    hi, mid, lo = _split3(x)
    return _dot(sel01, hi) + _dot(sel01, mid) + _dot(sel01, lo)


def _rms_scale(x):
    return x * lax.rsqrt(jnp.mean(x * x, axis=-1, keepdims=True) + EPS)


def _tril(n):
    r = lax.broadcasted_iota(jnp.int32, (n, n), 0)
    c = lax.broadcasted_iota(jnp.int32, (n, n), 1)
    return r >= c


def _norm_matmul_kernel(x_ref, nw_ref, w_ref, o_ref, xn_ref):
    @pl.when(pl.program_id(1) == 0)
    def _():
        xn_ref[...] = (_rms_scale(x_ref[...]) * nw_ref[...]).astype(BF16)

    o_ref[...] = _dot(xn_ref[...], w_ref[...])


def norm_matmul(x, norm_w, w, tn):
    t, d = x.shape
    n = w.shape[1]
    tm = _row_tile(t, 1024)
    assert n % tn == 0
    return pl.pallas_call(
        _norm_matmul_kernel,
        grid=(t // tm, n // tn),
        in_specs=[
            pl.BlockSpec((tm, d), lambda i, j: (i, 0)),
            pl.BlockSpec((1, d), lambda i, j: (0, 0)),
            pl.BlockSpec((d, tn), lambda i, j: (0, j)),
        ],
        out_specs=pl.BlockSpec((tm, tn), lambda i, j: (i, j)),
        out_shape=jax.ShapeDtypeStruct((t, n), F32),
        scratch_shapes=[pltpu.VMEM((tm, d), BF16)],
        compiler_params=_cparams("parallel", "arbitrary"),
        name="norm_in_proj",
    )(x, norm_w, w)


def _ssd_dt(small, dt_bias_ref, valid):
    lane = lax.broadcasted_iota(jnp.int32, small.shape, 1)
    dt = jax.nn.softplus(small + dt_bias_ref[...])
    keep = lane < SSD_HEADS
    if valid is not None:
        keep = jnp.logical_and(keep, valid)
    return jnp.where(keep, dt, 0.0)


def _ssd_gated_norm(y, z, norm_w_ref):
    y = y * jax.nn.silu(z)
    parts = []
    for g in range(SSD_GROUPS):
        parts.append(_rms_scale(y[:, g * SSD_GROUP_W:(g + 1) * SSD_GROUP_W]))
    return jnp.concatenate(parts, axis=1) * norm_w_ref[...]


def _ssd_prompt_kernel(z_ref, xs_ref, bc_ref, small_ref, conv_w_ref, conv_b_ref, dt_bias_ref,
                       a_log_ref, d_exp_ref, norm_w_ref, expand_ref, state0_ref, tail0_ref,
                       y_ref, state_ref, convtail_ref,
                       hist_ref, st_ref, yacc_ref, *, zero_rows):
    c = pl.program_id(1)
    nc = pl.num_programs(1)
    q = SSD_CHUNK

    @pl.when(c == 0)
    def _():
        hist_ref[0:SUBLANES, :] = tail0_ref[0]
        st_ref[...] = state0_ref[0].T

    hist_ref[SUBLANES:SUBLANES + q, 0:SSD_D_INNER] = xs_ref[...]
    hist_ref[SUBLANES:SUBLANES + q, SSD_D_INNER:SSD_CONV_DIM] = bc_ref[...]
    conv = conv_b_ref[...]
    for j in range(SSD_CONV):
        start = SUBLANES - (SSD_CONV - 1) + j
        conv = conv + conv_w_ref[j:j + 1, :] * hist_ref[start:start + q, :]
    xbc = jax.nn.silu(conv)
    tail = hist_ref[q:q + SUBLANES, :]
    hist_ref[0:SUBLANES, :] = tail
    convtail_ref[0] = tail

    xs = xbc[:, 0:SSD_D_INNER]
    bm = xbc[:, SSD_D_INNER:SSD_D_INNER + SSD_GROUPS * SSD_D_STATE].astype(BF16)
    cm = xbc[:, SSD_D_INNER + SSD_GROUPS * SSD_D_STATE:].astype(BF16)

    row = lax.broadcasted_iota(jnp.int32, (q, LANES), 0) + c * q
    dt = _ssd_dt(small_ref[...], dt_bias_ref, row >= zero_rows)
    da = dt * (-jnp.exp(a_log_ref[...]))
    tril = _tril(q)
    acum = _sel_left(tril.astype(BF16), da)
    acum_t = acum.T
    expand = expand_ref[...]
    dt_exp = _sel_right(expand, dt)
    acum_exp = _sel_right(expand, acum)
    acum_last = acum_exp[q - 1:q, :]
    xdt = xs * dt_exp
    xdt_bf = xdt.astype(BF16)
    x_end = (xdt * jnp.exp(acum_last - acum_exp)).astype(BF16)

    lane_lo = lax.broadcasted_iota(jnp.int32, (q, LANES), 1) < SSD_HEAD_DIM
    for g in range(SSD_GROUPS):
        bg = bm[:, g * SSD_D_STATE:(g + 1) * SSD_D_STATE]
        cg = cm[:, g * SSD_D_STATE:(g + 1) * SSD_D_STATE]
        cb = _dot_nt(cg, bg)
        for hp in range(SSD_HPG // 2):
            lo = (g * SSD_HPG + 2 * hp) * SSD_HEAD_DIM
            pair = []
            for k in range(2):
                h = g * SSD_HPG + 2 * hp + k
                diff = acum[:, h:h + 1] - acum_t[h:h + 1, :]
                seg = jnp.exp(jnp.where(tril, diff, -jnp.inf))
                pair.append(_dot((cb * seg).astype(BF16), xdt_bf[:, lo:lo + LANES]))
            yacc_ref[:, lo:lo + LANES] = jnp.where(lane_lo, pair[0], pair[1])

        gs = slice(g * SSD_GROUP_W, (g + 1) * SSD_GROUP_W)
        st_g = st_ref[:, gs]
        y_off = _dot(cg, st_g.astype(BF16)) * jnp.exp(acum_exp[:, gs])
        yacc_ref[:, gs] = yacc_ref[:, gs] + y_off
        st_ref[:, gs] = st_g * jnp.exp(acum_last[:, gs]) + _dot(bg.T, x_end[:, gs])

    y = yacc_ref[...] + d_exp_ref[...] * xs
    y_ref[0] = _ssd_gated_norm(y, z_ref[...], norm_w_ref).astype(BF16)

    @pl.when(c == nc - 1)
    def _():
        state_ref[0] = st_ref[...].T


def ssd_prompt(proj, batch, n_chunks, zero_rows, state0, tail0, w):
    q = SSD_CHUNK

    def rows(b, c):
        return b * n_chunks + c

    return pl.pallas_call(
        functools.partial(_ssd_prompt_kernel, zero_rows=zero_rows),
        grid=(batch, n_chunks),
        in_specs=[
            pl.BlockSpec((q, SSD_D_INNER), lambda b, c: (rows(b, c), COL_Z // SSD_D_INNER)),
            pl.BlockSpec((q, SSD_D_INNER), lambda b, c: (rows(b, c), COL_XS // SSD_D_INNER)),
            pl.BlockSpec((q, SSD_BC), lambda b, c: (rows(b, c), COL_BC // SSD_BC)),
            pl.BlockSpec((q, LANES), lambda b, c: (rows(b, c), COL_SMALL // LANES)),
            _const_spec((SSD_CONV, SSD_CONV_DIM)),
            _const_spec((1, SSD_CONV_DIM)),
            _const_spec((1, LANES)),
            _const_spec((1, LANES)),
            _const_spec((1, SSD_D_INNER)),
            _const_spec((1, SSD_D_INNER)),
            _const_spec((LANES, SSD_D_INNER)),
            _const_spec((1, SSD_D_INNER, SSD_D_STATE)),
            _const_spec((1, SUBLANES, SSD_CONV_DIM)),
        ],
        out_specs=[
            pl.BlockSpec((1, q, SSD_D_INNER), lambda b, c: (b, c, 0)),
            pl.BlockSpec((1, SSD_D_INNER, SSD_D_STATE), lambda b, c: (b, 0, 0)),
            pl.BlockSpec((1, SUBLANES, SSD_CONV_DIM), lambda b, c: (b, 0, 0)),
        ],
        out_shape=[
            jax.ShapeDtypeStruct((batch, n_chunks * q, SSD_D_INNER), BF16),
            jax.ShapeDtypeStruct((batch, SSD_D_INNER, SSD_D_STATE), F32),
            jax.ShapeDtypeStruct((batch, SUBLANES, SSD_CONV_DIM), F32),
        ],
        scratch_shapes=[
            pltpu.VMEM((SUBLANES + q, SSD_CONV_DIM), F32),
            pltpu.VMEM((SSD_D_STATE, SSD_D_INNER), F32),
            pltpu.VMEM((q, SSD_D_INNER), F32),
        ],
        compiler_params=_cparams("parallel", "arbitrary"),
        name="ssd_prompt",
    )(proj, proj, proj, proj, w["conv_w"], w["conv_b"], w["dt_bias"], w["a_log"], w["d_exp"],
      w["ssd_norm_w"], w["expand"], state0, tail0)


def _ssd_step_pre_kernel(xs_ref, bc_ref, small_ref, c0_ref, c1_ref, c2_ref, conv_w_ref, conv_b_ref,
                         dt_bias_ref, a_log_ref, expand_ref,
                         xact_ref, xdt_ref, decay_ref, bm_ref, cm_ref, convnew_ref):
    x = jnp.concatenate([xs_ref[...], bc_ref[...]], axis=1)
    hist = (c0_ref[...], c1_ref[...], c2_ref[...], x)
    conv = conv_b_ref[...]
    for j in range(SSD_CONV):
        conv = conv + conv_w_ref[j:j + 1, :] * hist[j]
    xbc = jax.nn.silu(conv)
    xs = xbc[:, 0:SSD_D_INNER]
    dt = _ssd_dt(small_ref[...], dt_bias_ref, None)
    da = dt * (-jnp.exp(a_log_ref[...]))
    expand = expand_ref[...]
    xact_ref[...] = xs
    xdt_ref[...] = xs * _sel_right(expand, dt)
    decay_ref[...] = jnp.exp(da)
    bm_ref[...] = xbc[:, SSD_D_INNER:SSD_D_INNER + SSD_GROUPS * SSD_D_STATE]
    cm_ref[...] = xbc[:, SSD_D_INNER + SSD_GROUPS * SSD_D_STATE:]
    convnew_ref[:, 0:SSD_CONV_DIM] = hist[1]
    convnew_ref[:, SSD_CONV_DIM:2 * SSD_CONV_DIM] = hist[2]
    convnew_ref[:, 2 * SSD_CONV_DIM:] = x


def ssd_step_pre(proj, conv_state, w):
    t = proj.shape[0]
    tm = _row_tile(t, 128)
    gw = SSD_GROUPS * SSD_D_STATE
    row = lambda width, col: pl.BlockSpec((tm, width), lambda i: (i, col))
    return pl.pallas_call(
        _ssd_step_pre_kernel,
        grid=(t // tm,),
        in_specs=[
            row(SSD_D_INNER, COL_XS // SSD_D_INNER),
            row(SSD_BC, COL_BC // SSD_BC),
            row(LANES, COL_SMALL // LANES),
            row(SSD_CONV_DIM, 0), row(SSD_CONV_DIM, 1), row(SSD_CONV_DIM, 2),
            _const_spec((SSD_CONV, SSD_CONV_DIM)),
            _const_spec((1, SSD_CONV_DIM)),
            _const_spec((1, LANES)),
            _const_spec((1, LANES)),
            _const_spec((LANES, SSD_D_INNER)),
        ],
        out_specs=[row(SSD_D_INNER, 0), row(SSD_D_INNER, 0), row(LANES, 0), row(gw, 0),
                   row(gw, 0), row((SSD_CONV - 1) * SSD_CONV_DIM, 0)],
        out_shape=[
            jax.ShapeDtypeStruct((t, SSD_D_INNER), F32),
            jax.ShapeDtypeStruct((t, SSD_D_INNER), F32),
            jax.ShapeDtypeStruct((t, LANES), F32),
            jax.ShapeDtypeStruct((t, gw), F32),
            jax.ShapeDtypeStruct((t, gw), F32),
            jax.ShapeDtypeStruct((t, (SSD_CONV - 1) * SSD_CONV_DIM), F32),
        ],
        compiler_params=_cparams("parallel"),
        name="ssd_step_pre",
    )(proj, proj, proj, conv_state, conv_state, conv_state, w["conv_w"], w["conv_b"], w["dt_bias"],
      w["a_log"], w["expand"])


def _dot_tn(a, b):
    return lax.dot_general(a, b, (((0,), (0,)), ((), ())), preferred_element_type=F32)


def _own_row(j):
    return lax.broadcasted_iota(jnp.int32, (SUBLANES, 1), 0) == j


def _ssd_step_state_kernel(s_ref, xdt_ref, dec_ref, bm_ref, cm_ref, snew_ref, y_ref):
    j = pl.program_id(1)

    @pl.when(j == 0)
    def _():
        y_ref[...] = jnp.zeros(y_ref.shape, F32)

    def group_rows(x):
        return jnp.concatenate(
            [x[:, g * SSD_D_STATE:(g + 1) * SSD_D_STATE] for g in range(SSD_GROUPS)], axis=0)

    group_of_lane = lax.broadcasted_iota(jnp.int32, (SUBLANES, SSD_D_INNER), 1) // SSD_GROUP_W
    b_rows = group_rows(bm_ref[...]).astype(BF16)
    y_new = jnp.zeros(y_ref.shape, F32)
    for u in range(DECODE_STEP_SEQS):
        row = j * DECODE_STEP_SEQS + u
        own = _own_row(row)
        dec = dec_ref[pl.ds(row, 1), :]
        xdt = jnp.where(own, xdt_ref[...], 0.0)
        x_rows = jnp.concatenate(
            [jnp.where(group_of_lane == g, xdt, 0.0) for g in range(SSD_GROUPS)], axis=0).astype(BF16)
        c_rows = group_rows(jnp.where(own, cm_ref[...], 0.0)).astype(BF16)
        decayed = [s_ref[u, h * SSD_HEAD_DIM:(h + 1) * SSD_HEAD_DIM, :] * dec[:, h:h + 1]
                   for h in range(SSD_HEADS)]
        s_new = jnp.concatenate(decayed, axis=0) + _dot_tn(x_rows, b_rows)
        snew_ref[u] = s_new
        y_rows = _dot_nt(c_rows, s_new.astype(BF16))
        y_new = y_new + jnp.concatenate(
            [y_rows[g * SUBLANES:(g + 1) * SUBLANES, g * SSD_GROUP_W:(g + 1) * SSD_GROUP_W]
             for g in range(SSD_GROUPS)], axis=1)
    y_ref[...] += y_new


def _decode_state_specs(rows, cols):
    steps = SUBLANES // DECODE_STEP_SEQS
    shape = (DECODE_STEP_SEQS, rows, cols)
    index = lambda i, j: (i * steps + j, 0, 0)
    return pl.BlockSpec(shape, index), pl.BlockSpec(shape, index)


def ssd_step_state(state, xdt, dec, bm, cm):
    t = state.shape[0]
    assert t % SUBLANES == 0
    gw = SSD_GROUPS * SSD_D_STATE
    rows = lambda width: pl.BlockSpec((SUBLANES, width), lambda i, j: (i, 0))
    st_in, st_out = _decode_state_specs(SSD_D_INNER, SSD_D_STATE)
    return pl.pallas_call(
        _ssd_step_state_kernel,
        grid=(t // SUBLANES, SUBLANES // DECODE_STEP_SEQS),
        in_specs=[st_in, rows(SSD_D_INNER), rows(LANES), rows(gw), rows(gw)],
        out_specs=[st_out, rows(SSD_D_INNER)],
        out_shape=[
            jax.ShapeDtypeStruct((t, SSD_D_INNER, SSD_D_STATE), F32),
            jax.ShapeDtypeStruct((t, SSD_D_INNER), F32),
        ],
        compiler_params=_cparams("parallel", "arbitrary"),
        name="ssd_step_state",
    )(state, xdt, dec, bm, cm)


def _ssd_step_post_kernel(y_ref, xact_ref, z_ref, d_exp_ref, norm_w_ref, o_ref):
    y = y_ref[...] + d_exp_ref[...] * xact_ref[...]
    o_ref[...] = _ssd_gated_norm(y, z_ref[...], norm_w_ref).astype(BF16)


def ssd_step_post(y, xact, proj, w):
    t = y.shape[0]
    tm = _row_tile(t, 128)
    row = lambda col: pl.BlockSpec((tm, SSD_D_INNER), lambda i: (i, col))
    return pl.pallas_call(
        _ssd_step_post_kernel,
        grid=(t // tm,),
        in_specs=[row(0), row(0), row(COL_Z // SSD_D_INNER), _const_spec((1, SSD_D_INNER)),
                  _const_spec((1, SSD_D_INNER))],
        out_specs=row(0),
        out_shape=jax.ShapeDtypeStruct((t, SSD_D_INNER), BF16),
        compiler_params=_cparams("parallel"),
        name="ssd_step_post",
    )(y, xact, proj, w["d_exp"], w["ssd_norm_w"])


def _gla_gate(small, gate_up_ref, gate_b_ref):
    pre = _dot(small.astype(BF16), gate_up_ref[...]) + gate_b_ref[...]
    return jax.nn.log_sigmoid(pre) / GLA_GATE_NORMALIZER


def _gla_out_norm(o, g, norm_w_ref):
    parts = []
    for h in range(GLA_HEADS):
        parts.append(_rms_scale(o[:, h * GLA_DV:(h + 1) * GLA_DV]))
    return jnp.concatenate(parts, axis=1) * norm_w_ref[...] * jax.nn.silu(g)


def _gla_prompt_kernel(q_ref, k_ref, v_ref, g_ref, small_ref, gate_up_ref, gate_b_ref, norm_w_ref,
                       state0_ref, o_ref, state_ref, s_ref, *, zero_rows):
    c = pl.program_id(1)
    nc = pl.num_programs(1)
    q = GLA_CHUNK
    rows = q * GLA_STEP_CHUNKS

    @pl.when(c == 0)
    def _():
        s_ref[...] = state0_ref[0]

    valid = (lax.broadcasted_iota(jnp.int32, (rows, GLA_KEY_DIM), 0) + c * rows) >= zero_rows
    gk = jnp.where(valid, _gla_gate(small_ref[...], gate_up_ref, gate_b_ref), 0.0)
    kk = jnp.where(valid, k_ref[...], 0.0)
    r = lax.broadcasted_iota(jnp.int32, (rows, rows), 0)
    cc = lax.broadcasted_iota(jnp.int32, (rows, rows), 1)
    chunk_tril = jnp.logical_and(r >= cc, r // q == cc // q)
    bcum = _sel_left(chunk_tril.astype(BF16), gk)
    lasts = [bcum[(i + 1) * q - 1:(i + 1) * q, :] for i in range(GLA_STEP_CHUNKS)]
    last_rows = jnp.concatenate([jnp.broadcast_to(x, (q, GLA_KEY_DIM)) for x in lasts], axis=0)
    qe = q_ref[...] * jnp.exp(bcum) * (GLA_DK ** -0.5)
    ke = kk * jnp.exp(-bcum)
    kend = kk * jnp.exp(last_rows - bcum)
    v_all = v_ref[...]
    vv = v_all.astype(BF16)
    chunk_of_row = lax.broadcasted_iota(jnp.int32, (rows, GLA_VAL_DIM), 0) // q
    v_chunk = [jnp.where(chunk_of_row == i, v_all, 0.0).astype(BF16) for i in range(GLA_STEP_CHUNKS)]
    outs = [[] for _ in range(GLA_STEP_CHUNKS)]
    for h in range(GLA_HEADS):
        ks = slice(h * GLA_DK, (h + 1) * GLA_DK)
        vs = slice(h * GLA_DV, (h + 1) * GLA_DV)
        qe_h = qe[:, ks].astype(BF16)
        att = jnp.where(chunk_tril, _dot_nt(qe_h, ke[:, ks].astype(BF16)), 0.0)
        o_intra = _dot(att.astype(BF16), vv[:, vs])
        kend_t = kend[:, ks].T.astype(BF16)
        s_h = s_ref[ks, :]
        for i in range(GLA_STEP_CHUNKS):
            rs = slice(i * q, (i + 1) * q)
            outs[i].append(o_intra[rs, :] + _dot(qe_h[rs, :], s_h.astype(BF16)))
            decay_col = jnp.broadcast_to(jnp.exp(lasts[i][:, ks]), (GLA_DK, GLA_DK)).T
            s_h = (jnp.concatenate([decay_col] * (GLA_DV // GLA_DK), axis=1) * s_h
                   + _dot(kend_t, v_chunk[i][:, vs]))
        s_ref[ks, :] = s_h
    o = jnp.concatenate([jnp.concatenate(o_i, axis=1) for o_i in outs], axis=0)
    o_ref[0] = _gla_out_norm(o, g_ref[...], norm_w_ref).astype(BF16)

    @pl.when(c == nc - 1)
    def _():
        state_ref[0] = s_ref[...]


def gla_prompt(proj, batch, n_rows, zero_rows, state0, w):
    rows = GLA_CHUNK * GLA_STEP_CHUNKS
    assert n_rows % rows == 0
    n_steps = n_rows // rows

    def blk(b, c):
        return b * n_steps + c

    return pl.pallas_call(
        functools.partial(_gla_prompt_kernel, zero_rows=zero_rows),
        grid=(batch, n_steps),
        in_specs=[
            pl.BlockSpec((rows, GLA_KEY_DIM), lambda b, c: (blk(b, c), COL_Q // GLA_KEY_DIM)),
            pl.BlockSpec((rows, GLA_KEY_DIM), lambda b, c: (blk(b, c), COL_K // GLA_KEY_DIM)),
            pl.BlockSpec((rows, GLA_VAL_DIM), lambda b, c: (blk(b, c), COL_V // GLA_VAL_DIM)),
            pl.BlockSpec((rows, GLA_VAL_DIM), lambda b, c: (blk(b, c), COL_G // GLA_VAL_DIM)),
            pl.BlockSpec((rows, LANES), lambda b, c: (blk(b, c), COL_SMALL // LANES)),
            _const_spec((LANES, GLA_KEY_DIM)),
            _const_spec((1, GLA_KEY_DIM)),
            _const_spec((1, GLA_VAL_DIM)),
            _const_spec((1, GLA_KEY_DIM, GLA_DV)),
        ],
        out_specs=[
            pl.BlockSpec((1, rows, GLA_VAL_DIM), lambda b, c: (b, c, 0)),
            pl.BlockSpec((1, GLA_KEY_DIM, GLA_DV), lambda b, c: (b, 0, 0)),
        ],
        out_shape=[
            jax.ShapeDtypeStruct((batch, n_steps * rows, GLA_VAL_DIM), BF16),
            jax.ShapeDtypeStruct((batch, GLA_KEY_DIM, GLA_DV), F32),
        ],
        scratch_shapes=[pltpu.VMEM((GLA_KEY_DIM, GLA_DV), F32)],
        compiler_params=_cparams("parallel", "arbitrary"),
        name="gla_prompt",
    )(proj, proj, proj, proj, proj, w["gate_up"], w["gate_b"], w["gla_norm_w"], state0)


def _gla_step_kernel(s_ref, q_ref, k_ref, v_ref, g_ref, small_ref, gate_up_ref, gate_b_ref, norm_w_ref,
                     snew_ref, o_ref, dec_ref, acc_ref):
    j = pl.program_id(1)

    @pl.when(j == 0)
    def _():
        dec_ref[...] = jnp.exp(_gla_gate(small_ref[...], gate_up_ref, gate_b_ref))
        acc_ref[...] = jnp.zeros(acc_ref.shape, F32)

    head_of_lane = lax.broadcasted_iota(jnp.int32, (SUBLANES, GLA_KEY_DIM), 1) // GLA_DK

    def per_head_rows(x):
        return jnp.concatenate([jnp.where(head_of_lane == h, x, 0.0) for h in range(GLA_HEADS)], axis=0)

    vv = v_ref[...]
    v_rows = jnp.concatenate(
        [vv[:, h * GLA_DV:(h + 1) * GLA_DV] for h in range(GLA_HEADS)], axis=0).astype(BF16)
    o_new = jnp.zeros(acc_ref.shape, F32)
    for u in range(DECODE_STEP_SEQS):
        own = _own_row(j * DECODE_STEP_SEQS + u)
        k_rows = per_head_rows(jnp.where(own, k_ref[...], 0.0)).astype(BF16)
        q_rows = per_head_rows(jnp.where(own, q_ref[...] * (GLA_DK ** -0.5), 0.0)).astype(BF16)
        pieces = [p.astype(F32) for p in _split3(jnp.where(own, dec_ref[...], 0.0))]
        d_rows = jnp.concatenate(pieces + [jnp.zeros_like(pieces[0])], axis=0).astype(BF16)
        decay = _dot_tn(d_rows, jnp.ones((d_rows.shape[0], GLA_DV), BF16))
        s_new = s_ref[u] * decay + _dot_tn(k_rows, v_rows)
        snew_ref[u] = s_new
        o_rows = _dot(q_rows, s_new.astype(BF16))
        o_new = o_new + jnp.concatenate(
            [o_rows[h * SUBLANES:(h + 1) * SUBLANES, :] for h in range(GLA_HEADS)], axis=1)
    acc_ref[...] += o_new

    @pl.when(j == pl.num_programs(1) - 1)
    def _():
        o_ref[...] = _gla_out_norm(acc_ref[...], g_ref[...], norm_w_ref)


def gla_step(state, proj, w):
    t = state.shape[0]
    assert t % SUBLANES == 0
    rows = lambda width, col: pl.BlockSpec((SUBLANES, width), lambda i, j: (i, col))
    st_in, st_out = _decode_state_specs(GLA_KEY_DIM, GLA_DV)
    return pl.pallas_call(
        _gla_step_kernel,
        grid=(t // SUBLANES, SUBLANES // DECODE_STEP_SEQS),
        in_specs=[st_in, rows(GLA_KEY_DIM, COL_Q // GLA_KEY_DIM), rows(GLA_KEY_DIM, COL_K // GLA_KEY_DIM),
                  rows(GLA_VAL_DIM, COL_V // GLA_VAL_DIM), rows(GLA_VAL_DIM, COL_G // GLA_VAL_DIM),
                  rows(LANES, COL_SMALL // LANES), _const_spec((LANES, GLA_KEY_DIM)),
                  _const_spec((1, GLA_KEY_DIM)), _const_spec((1, GLA_VAL_DIM))],
        out_specs=[st_out, rows(GLA_VAL_DIM, 0)],
        out_shape=[
            jax.ShapeDtypeStruct((t, GLA_KEY_DIM, GLA_DV), F32),
            jax.ShapeDtypeStruct((t, GLA_VAL_DIM), F32),
        ],
        scratch_shapes=[pltpu.VMEM((SUBLANES, GLA_KEY_DIM), F32), pltpu.VMEM((SUBLANES, GLA_VAL_DIM), F32)],
        compiler_params=_cparams("parallel", "arbitrary"),
        name="gla_step",
    )(state, proj, proj, proj, proj, proj, w["gate_up"], w["gate_b"], w["gla_norm_w"])


def _merge_kernel(x_ref, ys_ref, og_ref, nw_ref, wm_ref, wso_ref, wgo_ref, wout_ref, o_ref):
    x = x_ref[...]
    xn = (_rms_scale(x) * nw_ref[...]).astype(BF16)
    gates = jax.nn.sigmoid(_dot(xn, wm_ref[...]))
    y_ssd = _dot(ys_ref[...].astype(BF16), wso_ref[...])
    y_gla = _dot(og_ref[...].astype(BF16), wgo_ref[...])
    mix = gates[:, 0:D_MODEL] * y_ssd + gates[:, D_MODEL:] * y_gla
    o_ref[...] = x + _dot(mix.astype(BF16), wout_ref[...])


def merge(x, y_ssd, o_gla, w):
    t = x.shape[0]
    tm = _row_tile(t, 512)
    row = lambda width: pl.BlockSpec((tm, width), lambda i: (i, 0))
    return pl.pallas_call(
        _merge_kernel,
        grid=(t // tm,),
        in_specs=[row(D_MODEL), row(SSD_D_INNER), row(GLA_VAL_DIM), _const_spec((1, D_MODEL)),
                  _const_spec((D_MODEL, 2 * D_MODEL)), _const_spec((SSD_D_INNER, D_MODEL)),
                  _const_spec((GLA_VAL_DIM, D_MODEL)), _const_spec((D_MODEL, D_MODEL))],
        out_specs=row(D_MODEL),
        out_shape=jax.ShapeDtypeStruct((t, D_MODEL), F32),
        compiler_params=_cparams("parallel"),
        name="merge",
    )(x, y_ssd, o_gla, w["norm1_w"], w["w_merge"], w["w_ssd_out"], w["w_gla_out"], w["w_out"])


def _merge_exchange_pairs(n):
    pairs = []
    p = n // 2
    while p > 0:
        q, r, d = n // 2, 0, p
        while d > 0:
            pairs += [(i, i + d) for i in range(n - d) if (i & p) == r]
            d, q, r = q - p, q // 2, p
        p //= 2
    return pairs


_SORT16 = _merge_exchange_pairs(PEER_TOPK)
_BITONIC16 = [(i, i + d) for d in (8, 4, 2, 1) for i in range(PEER_TOPK) if (i & d) == 0]


def _exchange(v, ids, i, j):
    if ids is None:
        v[i], v[j] = jnp.maximum(v[i], v[j]), jnp.minimum(v[i], v[j])
    else:
        ge = v[i] >= v[j]
        v[i], v[j] = jnp.where(ge, v[i], v[j]), jnp.where(ge, v[j], v[i])
        ids[i], ids[j] = jnp.where(ge, ids[i], ids[j]), jnp.where(ge, ids[j], ids[i])


def _merge_top16(v, ids, w, wids):
    n, m = len(v), len(w)
    for i in range(n - m, n):
        o = n - 1 - i
        if ids is None:
            v[i] = jnp.maximum(v[i], w[o])
        else:
            ge = v[i] >= w[o]
            v[i] = jnp.where(ge, v[i], w[o])
            ids[i] = jnp.where(ge, ids[i], wids[o])
    for i, j in _BITONIC16:
        _exchange(v, ids, i, j)


def _top16_rows(x, with_ids):
    n = x.shape[0] // SUBLANES
    assert n == PEER_TOPK
    v = [x[SUBLANES * r:SUBLANES * (r + 1), :] for r in range(n)]
    ids = None
    if with_ids:
        row = lax.broadcasted_iota(jnp.int32, v[0].shape, 0).astype(F32)
        ids = [row + float(SUBLANES * r) for r in range(n)]
    for i, j in _SORT16:
        _exchange(v, ids, i, j)
    shift = SUBLANES // 2
    while shift:
        w = [pltpu.roll(a, shift, 0) for a in v]
        wids = [pltpu.roll(a, shift, 0) for a in ids] if with_ids else None
        _merge_top16(v, ids, w, wids)
        shift //= 2
    return v, ids


def _peer_kernel(x_ref, nw_ref, fnw_ref, wqt_ref, keys_ref, u_ref, v_ref, o_ref,
                 hn_ref, qt_ref, rw_ref, s2t_ref, svk_ref, sik_ref, stat_ref, g1k_ref, ids_ref, g1tm_ref,
                 kept8_ref, e8_ref, hid_ref, acc_ref):
    s = pl.program_id(1)
    ns = pl.num_programs(1)
    tb = x_ref.shape[0]
    nk = PEER_N_KEYS
    nj = PEER_HEADS * PEER_TOPK

    @pl.when(s == 0)
    def _select():
        acc_ref[...] = jnp.zeros(acc_ref.shape, F32)
        hn_ref[...] = (_rms_scale(x_ref[...]) * nw_ref[...]).astype(BF16)
        stat_ref[...] = jnp.zeros(stat_ref.shape, F32)
        qt_ref[...] = _dot_nt(wqt_ref[...], hn_ref[...]).astype(BF16)
        for h in range(PEER_HEADS):
            for c in range(2):
                r0 = (h * 2 + c) * (PEER_DQ // 2)
                sc_t = _dot(keys_ref[h * 2 + c], qt_ref[r0:r0 + PEER_DQ // 2, :])
                for g in range(tb // LANES):
                    ls = slice(g * LANES, (g + 1) * LANES)
                    vals, ids = _top16_rows(sc_t[:, ls], c == 0)
                    for k in range(PEER_TOPK):
                        svk_ref[c, k, h:h + 1, ls] = vals[k][0:1, :]
                        if c == 0:
                            sik_ref[k, h:h + 1, ls] = ids[k][0:1, :]
                if c == 1:
                    s2t_ref[h] = sc_t

        pairs = ([(0, k2) for k2 in range(PEER_TOPK)]
                 + [(k1, k2) for k1 in range(1, PEER_TOPK // 2) for k2 in range(PEER_TOPK // (k1 + 1))]
                 + [(k1, 0) for k1 in range(PEER_TOPK // 2, PEER_TOPK)])
        for g in range(tb // LANES):
            ls = slice(g * LANES, (g + 1) * LANES)
            a = [svk_ref[0, k, :, ls] for k in range(PEER_TOPK)]
            b = [svk_ref[1, k, :, ls] for k in range(PEER_TOPK)]
            best = [a[0] + b[k2] for k2 in range(PEER_TOPK)]
            for k1 in range(1, PEER_TOPK // 2):
                _merge_top16(best, None, [a[k1] + b[k2] for k2 in range(PEER_TOPK // (k1 + 1))], None)
            _merge_top16(best, None, [a[k1] + b[0] for k1 in range(PEER_TOPK // 2, PEER_TOPK)], None)
            tau, top = best[PEER_TOPK - 1], best[0]
            z = jnp.zeros_like(tau)
            for k1, k2 in pairs:
                cv = a[k1] + b[k2]
                z = z + jnp.where(cv >= tau, jnp.exp(cv - top), 0.0)
            stat_ref[0:PEER_HEADS, ls] = tau
            stat_ref[PEER_HEADS:2 * PEER_HEADS, ls] = b[0]
            inv_z = 1.0 / z
            for k in range(PEER_TOPK):
                g1k_ref[k, :, ls] = jnp.exp(a[k] - a[0]) * inv_z

        ids_ref[...] = sik_ref[...].reshape(nj, tb).T
        g1tm_ref[...] = g1k_ref[...].reshape(nj, tb).T
        def count_kept(h, carry):
            sc2 = s2t_ref[h]
            tau_t = stat_ref[pl.ds(h, 1), :]
            kept = jnp.zeros(sc2.shape, F32)
            for k1 in range(PEER_TOPK):
                kept = jnp.where((sc2 + svk_ref[0, k1, pl.ds(h, 1), :]) >= tau_t, float(k1 + 1), kept)
            e2 = jnp.exp(sc2 - stat_ref[pl.ds(PEER_HEADS + h, 1), :])
            kept8_ref[pl.ds(h, tb, stride=PEER_HEADS), :] = kept.T
            e8_ref[pl.ds(h, tb, stride=PEER_HEADS), :] = e2.T
            return carry

        lax.fori_loop(0, PEER_HEADS, count_kept, 0)

        key1 = lax.broadcasted_iota(jnp.int32, (nk, nk), 0).astype(F32)

        def scatter(t, carry):
            hb = pl.multiple_of(t * PEER_HEADS, PEER_HEADS)
            kept8 = kept8_ref[pl.ds(hb, PEER_HEADS), :]
            e8 = e8_ref[pl.ds(hb, PEER_HEADS), :]
            rows = jnp.concatenate(
                [jnp.where(kept8 > float(k1), e8, 0.0) for k1 in range(PEER_TOPK)], axis=0).astype(BF16)
            place = jnp.where(key1 == ids_ref[pl.ds(t, 1), :], g1tm_ref[pl.ds(t, 1), :], 0.0)
            rw_ref[pl.ds(t * PEER_PITCH, nk), :] = _dot(place.astype(BF16), rows)
            return carry

        lax.fori_loop(0, tb, scatter, 0, unroll=PEER_SCATTER_UNROLL)

    def produce():
        hn = hn_ref[...]
        half = PEER_STEP_EXPERTS // 2
        for c in range(2):
            hid_ref[:, c * half:(c + 1) * half] = _dot_nt(hn, u_ref[c * half:(c + 1) * half, :])

    def gated_prev():
        prev = hid_ref[...]
        act = 0.5 * prev * (1.0 + lax.erf(prev * (2.0 ** -0.5)))
        slab0 = (s - 1) * PEER_SLABS_PER_STEP
        wts = jnp.concatenate(
            [rw_ref[pl.ds(slab0 + i, tb, stride=PEER_PITCH), :] for i in range(PEER_SLABS_PER_STEP)],
            axis=1)
        return (act * wts).astype(BF16)

    def consume(gated):
        half = D_MODEL // 2
        for c in range(2):
            acc_ref[:, c * half:(c + 1) * half] += _dot(gated, v_ref[:, c * half:(c + 1) * half])

    @pl.when(s == 0)
    def _():
        produce()

    @pl.when(jnp.logical_and(s > 0, s < ns - 1))
    def _():
        gated = gated_prev()
        produce()
        consume(gated)

    @pl.when(s == ns - 1)
    def _():
        consume(gated_prev())
        o_ref[...] = _rms_scale(x_ref[...] + acc_ref[...]) * fnw_ref[...]


def peer_final(x, w):
    t = x.shape[0]
    tb = _row_tile(t, PEER_TOKEN_BLOCK)
    assert tb % LANES == 0
    nblk = PEER_N_EXPERTS // PEER_STEP_EXPERTS
    return pl.pallas_call(
        _peer_kernel,
        grid=(t // tb, nblk + 1),
        in_specs=[
            pl.BlockSpec((tb, D_MODEL), lambda i, s: (i, 0)),
            _const_spec((1, D_MODEL)),
            _const_spec((1, D_MODEL)),
            _const_spec((PEER_HEADS * PEER_DQ, D_MODEL)),
            _const_spec((2 * PEER_HEADS, PEER_N_KEYS, PEER_DQ // 2)),
            pl.BlockSpec((PEER_STEP_EXPERTS, D_MODEL), lambda i, s: (jnp.minimum(s, nblk - 1), 0)),
            pl.BlockSpec((PEER_STEP_EXPERTS, D_MODEL), lambda i, s: (jnp.maximum(s - 1, 0), 0)),
        ],
        out_specs=pl.BlockSpec((tb, D_MODEL), lambda i, s: (i, 0)),
        out_shape=jax.ShapeDtypeStruct((t, D_MODEL), F32),
        scratch_shapes=[
            pltpu.VMEM((tb, D_MODEL), BF16),
            pltpu.VMEM((PEER_HEADS * PEER_DQ, tb), BF16),
            pltpu.VMEM((tb * PEER_PITCH, PEER_N_KEYS), F32),
            pltpu.VMEM((PEER_HEADS, PEER_N_KEYS, tb), F32),
            pltpu.VMEM((2, PEER_TOPK, PEER_HEADS, tb), F32),
            pltpu.VMEM((PEER_TOPK, PEER_HEADS, tb), F32),
            pltpu.VMEM((LANES, tb), F32),
            pltpu.VMEM((PEER_TOPK, PEER_HEADS, tb), F32),
            pltpu.VMEM((tb, PEER_HEADS * PEER_TOPK), F32),
            pltpu.VMEM((tb, PEER_HEADS * PEER_TOPK), F32),
            pltpu.VMEM((tb * PEER_HEADS, PEER_N_KEYS), F32),
            pltpu.VMEM((tb * PEER_HEADS, PEER_N_KEYS), F32),
            pltpu.VMEM((tb, PEER_STEP_EXPERTS), F32),
            pltpu.VMEM((tb, D_MODEL), F32),
        ],
        compiler_params=_cparams("parallel", "arbitrary"),
        name="peer_final",
    )(x, w["norm2_w"], w["final_norm_w"], w["peer_wq_t"], w["peer_keys"], w["peer_u"], w["peer_v"])


def _prepare_weights(norm1_w, w_in, conv_w, conv_b, dt_bias, a_log, d_skip, ssd_norm_w, w_ssd_out,
                     gla_gate_up, gla_gate_b, gla_norm_w, w_gla_out, w_out, norm2_w, peer_w_q,
                     peer_sub_keys, peer_u, peer_v, final_norm_w):
    offs = [0]
    for sz in _IN_SIZES:
        offs.append(offs[-1] + sz)
    z0, xbc0, dt0, q0, _, _, _, glr0, mg0, end = offs
    small = jnp.concatenate(
        [w_in[:, dt0:q0], w_in[:, glr0:mg0],
         jnp.zeros((D_MODEL, SMALL_COLS - SSD_HEADS - GLA_GATE_RANK), w_in.dtype)], axis=1)
    w_proj = jnp.concatenate([w_in[:, z0:dt0], w_in[:, q0:glr0], small], axis=1).astype(BF16)
    assert w_proj.shape[1] == IN_COLS

    def lane_pad(v):
        return jnp.pad(v, (0, LANES - v.shape[0])).reshape(1, LANES)

    head_of_col = jnp.arange(SSD_D_INNER) // SSD_HEAD_DIM
    expand = (jnp.arange(LANES)[:, None] == head_of_col[None, :]).astype(BF16)
    gate_up = jnp.zeros((LANES, GLA_KEY_DIM), F32).at[SMALL_GLR:SMALL_GLR + GLA_GATE_RANK].set(gla_gate_up)
    keys = jnp.transpose(peer_sub_keys, (1, 0, 2, 3)).reshape(2 * PEER_HEADS, PEER_N_KEYS, PEER_DQ // 2)
    return {
        "norm1_w": norm1_w.reshape(1, D_MODEL),
        "w_proj": w_proj,
        "w_merge": w_in[:, mg0:end].astype(BF16),
        "conv_w": conv_w,
        "conv_b": conv_b.reshape(1, SSD_CONV_DIM),
        "dt_bias": lane_pad(dt_bias),
        "a_log": lane_pad(a_log),
        "d_exp": jnp.repeat(d_skip, SSD_HEAD_DIM).reshape(1, SSD_D_INNER),
        "ssd_norm_w": ssd_norm_w.reshape(1, SSD_D_INNER),
        "expand": expand,
        "w_ssd_out": w_ssd_out.astype(BF16),
        "gate_up": gate_up.astype(BF16),
        "gate_b": gla_gate_b.reshape(1, GLA_KEY_DIM),
        "gla_norm_w": jnp.tile(gla_norm_w, GLA_HEADS).reshape(1, GLA_VAL_DIM),
        "w_gla_out": w_gla_out.astype(BF16),
        "w_out": w_out.astype(BF16),
        "norm2_w": norm2_w.reshape(1, D_MODEL),
        "peer_wq_t": peer_w_q.T.astype(BF16),
        "peer_keys": keys.astype(BF16),
        "peer_u": peer_u.astype(BF16),
        "peer_v": peer_v.astype(BF16),
        "final_norm_w": final_norm_w.reshape(1, D_MODEL),
    }


def _prompt_path(x_prompt, meta_tokens, w):
    b, seq, _ = x_prompt.shape
    assert seq % SSD_CHUNK == 0
    meta_rows = jnp.concatenate(
        [jnp.zeros((PROMPT_ZERO_ROWS, D_MODEL), x_prompt.dtype), meta_tokens.astype(x_prompt.dtype)], axis=0)
    meta_proj = norm_matmul(meta_rows, w["norm1_w"], w["w_proj"], IN_COL_TILE)
    zero_ssd = jnp.zeros((1, SSD_D_INNER, SSD_D_STATE), F32)
    zero_tail = jnp.zeros((1, SUBLANES, SSD_CONV_DIM), F32)
    zero_gla = jnp.zeros((1, GLA_KEY_DIM, GLA_DV), F32)
    _, ssd0, tail0 = ssd_prompt(meta_proj, 1, 1, PROMPT_ZERO_ROWS, zero_ssd, zero_tail, w)
    _, gla0 = gla_prompt(meta_proj, 1, SSD_CHUNK, PROMPT_ZERO_ROWS, zero_gla, w)
    proj = norm_matmul(x_prompt.reshape(b * seq, D_MODEL), w["norm1_w"], w["w_proj"], IN_COL_TILE)
    y_ssd, st_ssd, conv_tail = ssd_prompt(proj, b, seq // SSD_CHUNK, 0, ssd0, tail0, w)
    o_gla, st_gla = gla_prompt(proj, b, seq, 0, gla0, w)
    x1 = merge(x_prompt.reshape(b * seq, D_MODEL), y_ssd.reshape(b * seq, SSD_D_INNER),
               o_gla.reshape(b * seq, GLA_VAL_DIM), w)
    y = peer_final(x1, w).reshape(b, seq, D_MODEL)
    return (y,
            st_ssd.reshape(1, b, SSD_HEADS, SSD_HEAD_DIM, SSD_D_STATE),
            conv_tail[:, SUBLANES - (SSD_CONV - 1):, :][None],
            st_gla.reshape(1, b, GLA_HEADS, GLA_DK, GLA_DV))


def _sample_path(x_sample, state_ssd, state_conv, state_gla, w):
    b = x_sample.shape[0]
    x = x_sample.reshape(b, D_MODEL)
    proj = norm_matmul(x, w["norm1_w"], w["w_proj"], IN_COL_TILE)
    xact, xdt, dec, bm, cm, conv_new = ssd_step_pre(
        proj, state_conv.reshape(b, (SSD_CONV - 1) * SSD_CONV_DIM), w)
    st_ssd, y = ssd_step_state(state_ssd.reshape(b, SSD_D_INNER, SSD_D_STATE), xdt, dec, bm, cm)
    y_ssd = ssd_step_post(y, xact, proj, w)
    st_gla, o_gla = gla_step(state_gla.reshape(b, GLA_KEY_DIM, GLA_DV), proj, w)
    x1 = merge(x, y_ssd, o_gla, w)
    y = peer_final(x1, w).reshape(b, 1, D_MODEL)
    return (y,
            st_ssd.reshape(1, b, SSD_HEADS, SSD_HEAD_DIM, SSD_D_STATE),
            conv_new.reshape(1, b, SSD_CONV - 1, SSD_CONV_DIM),
            st_gla.reshape(1, b, GLA_HEADS, GLA_DK, GLA_DV))


def kernel(x_prompt, x_sample, state_ssd, state_conv, state_gla, meta_tokens, norm1_w, w_in, conv_w, conv_b,
           dt_bias, a_log, d_skip, ssd_norm_w, w_ssd_out, gla_gate_up, gla_gate_b, gla_norm_w, w_gla_out,
           w_out, norm2_w, peer_w_q, peer_sub_keys, peer_u, peer_v, final_norm_w):
    layer = (norm1_w, w_in, conv_w, conv_b, dt_bias, a_log, d_skip, ssd_norm_w, w_ssd_out, gla_gate_up,
             gla_gate_b, gla_norm_w, w_gla_out, w_out, norm2_w, peer_w_q, peer_sub_keys, peer_u, peer_v)
    assert all(p.shape[0] == 1 for p in layer), "single-layer step"
    w = _prepare_weights(*[p[0] for p in layer], final_norm_w)
    yp, p_ssd, p_conv, p_gla = _prompt_path(x_prompt, meta_tokens, w)
    ys, s_ssd, s_conv, s_gla = _sample_path(x_sample, state_ssd[0], state_conv[0], state_gla[0], w)
    return (yp, ys, p_ssd, p_conv, p_gla, s_ssd, s_conv, s_gla)
```

```python
import functools

import jax
import jax.numpy as jnp
from jax import lax
from jax.experimental import pallas as pl
from jax.experimental.pallas import tpu as pltpu

F32 = jnp.float32
BF16 = jnp.bfloat16

LANES = 128
SUBLANES = 8
VMEM_LIMIT_BYTES = 60 * 1024 * 1024

D_MODEL = 1024
N_META = 16
EPS = 1e-6
SSD_D_INNER = 2 * D_MODEL
SSD_HEAD_DIM = 64
SSD_HEADS = SSD_D_INNER // SSD_HEAD_DIM
SSD_GROUPS = 4
SSD_HPG = SSD_HEADS // SSD_GROUPS
SSD_D_STATE = 128
SSD_CONV = 4
SSD_CHUNK = 128
SSD_BC = 2 * SSD_GROUPS * SSD_D_STATE
SSD_CONV_DIM = SSD_D_INNER + SSD_BC
SSD_GROUP_W = SSD_HPG * SSD_HEAD_DIM
GLA_HEADS = 4
GLA_KEY_DIM = D_MODEL // 2
GLA_VAL_DIM = D_MODEL
GLA_DK = GLA_KEY_DIM // GLA_HEADS
GLA_DV = GLA_VAL_DIM // GLA_HEADS
GLA_GATE_RANK = 16
GLA_GATE_NORMALIZER = 16.0
GLA_CHUNK = 64
GLA_STEP_CHUNKS = 4
DECODE_STEP_SEQS = 8
PEER_HEADS = 8
PEER_N_KEYS = 128
PEER_N_EXPERTS = PEER_N_KEYS * PEER_N_KEYS
PEER_DQ = 256
PEER_TOPK = 16
PEER_SLABS_PER_STEP = 16
PEER_STEP_EXPERTS = PEER_SLABS_PER_STEP * PEER_N_KEYS
PEER_SCATTER_UNROLL = 16
PEER_PITCH = PEER_N_KEYS + 4
PEER_TOKEN_BLOCK = 256

_IN_SIZES = (SSD_D_INNER, SSD_CONV_DIM, SSD_HEADS, GLA_KEY_DIM, GLA_KEY_DIM, GLA_VAL_DIM,
             GLA_VAL_DIM, GLA_GATE_RANK, 2 * D_MODEL)
COL_Z = 0
COL_XS = COL_Z + SSD_D_INNER
COL_BC = COL_XS + SSD_D_INNER
COL_Q = COL_BC + SSD_BC
COL_K = COL_Q + GLA_KEY_DIM
COL_V = COL_K + GLA_KEY_DIM
COL_G = COL_V + GLA_VAL_DIM
COL_SMALL = COL_G + GLA_VAL_DIM
MXU_WIDTH = 256
SMALL_COLS = MXU_WIDTH
IN_COLS = COL_SMALL + SMALL_COLS
IN_COL_TILE = IN_COLS // 3
SMALL_GLR = SSD_HEADS
PROMPT_ZERO_ROWS = SSD_CHUNK - N_META


def _cparams(*sem):
    return pltpu.CompilerParams(dimension_semantics=sem, vmem_limit_bytes=VMEM_LIMIT_BYTES)


def _const_spec(shape):
    nd = len(shape)
    return pl.BlockSpec(shape, lambda *_: (0,) * nd, pipeline_mode=pl.Buffered(1))


def _row_tile(t, cap):
    tm = cap
    while t % tm:
        tm //= 2
    return tm


def _split3(x):
    hi = x.astype(BF16)
    r1 = x - hi.astype(F32)
    mid = r1.astype(BF16)
    lo = (r1 - mid.astype(F32)).astype(BF16)
    return hi, mid, lo


def _dot(a, b):
    return jnp.dot(a, b, preferred_element_type=F32)


def _dot_nt(a, b):
    return lax.dot_general(a, b, (((1,), (1,)), ((), ())), preferred_element_type=F32)


def _sel_right(sel01, x):
    hi, mid, lo = _split3(x)
    return _dot(hi, sel01) + _dot(mid, sel01) + _dot(lo, sel01)


def _sel_left(sel01, x):
    hi, mid, lo = _split3(x)
    return _dot(sel01, hi) + _dot(sel01, mid) + _dot(sel01, lo)


def _rms_scale(x):
    return x * lax.rsqrt(jnp.mean(x * x, axis=-1, keepdims=True) + EPS)


def _tril(n):
    r = lax.broadcasted_iota(jnp.int32, (n, n), 0)
    c = lax.broadcasted_iota(jnp.int32, (n, n), 1)
    return r >= c


def _norm_matmul_kernel(x_ref, nw_ref, w_ref, o_ref, xn_ref):
    @pl.when(pl.program_id(1) == 0)
    def _():
        xn_ref[...] = (_rms_scale(x_ref[...]) * nw_ref[...]).astype(BF16)

    o_ref[...] = _dot(xn_ref[...], w_ref[...])


def norm_matmul(x, norm_w, w, tn):
    t, d = x.shape
    n = w.shape[1]
    tm = _row_tile(t, 1024)
    assert n % tn == 0
    return pl.pallas_call(
        _norm_matmul_kernel,
        grid=(t // tm, n // tn),
        in_specs=[
            pl.BlockSpec((tm, d), lambda i, j: (i, 0)),
            pl.BlockSpec((1, d), lambda i, j: (0, 0)),
            pl.BlockSpec((d, tn), lambda i, j: (0, j)),
        ],
        out_specs=pl.BlockSpec((tm, tn), lambda i, j: (i, j)),
        out_shape=jax.ShapeDtypeStruct((t, n), F32),
        scratch_shapes=[pltpu.VMEM((tm, d), BF16)],
        compiler_params=_cparams("parallel", "arbitrary"),
        name="norm_in_proj",
    )(x, norm_w, w)


def _ssd_dt(small, dt_bias_ref, valid):
    lane = lax.broadcasted_iota(jnp.int32, small.shape, 1)
    dt = jax.nn.softplus(small + dt_bias_ref[...])
    keep = lane < SSD_HEADS
    if valid is not None:
        keep = jnp.logical_and(keep, valid)
    return jnp.where(keep, dt, 0.0)


def _ssd_gated_norm(y, z, norm_w_ref):
    y = y * jax.nn.silu(z)
    parts = []
    for g in range(SSD_GROUPS):
        parts.append(_rms_scale(y[:, g * SSD_GROUP_W:(g + 1) * SSD_GROUP_W]))
    return jnp.concatenate(parts, axis=1) * norm_w_ref[...]


def _ssd_prompt_kernel(z_ref, xs_ref, bc_ref, small_ref, conv_w_ref, conv_b_ref, dt_bias_ref,
                       a_log_ref, d_exp_ref, norm_w_ref, expand_ref, state0_ref, tail0_ref,
                       y_ref, state_ref, convtail_ref,
                       hist_ref, st_ref, yacc_ref, *, zero_rows):
    c = pl.program_id(1)
    nc = pl.num_programs(1)
    q = SSD_CHUNK

    @pl.when(c == 0)
    def _():
        hist_ref[0:SUBLANES, :] = tail0_ref[0]
        st_ref[...] = state0_ref[0].T

    hist_ref[SUBLANES:SUBLANES + q, 0:SSD_D_INNER] = xs_ref[...]
    hist_ref[SUBLANES:SUBLANES + q, SSD_D_INNER:SSD_CONV_DIM] = bc_ref[...]
    conv = conv_b_ref[...]
    for j in range(SSD_CONV):
        start = SUBLANES - (SSD_CONV - 1) + j
        conv = conv + conv_w_ref[j:j + 1, :] * hist_ref[start:start + q, :]
    xbc = jax.nn.silu(conv)
    tail = hist_ref[q:q + SUBLANES, :]
    hist_ref[0:SUBLANES, :] = tail
    convtail_ref[0] = tail

    xs = xbc[:, 0:SSD_D_INNER]
    bm = xbc[:, SSD_D_INNER:SSD_D_INNER + SSD_GROUPS * SSD_D_STATE].astype(BF16)
    cm = xbc[:, SSD_D_INNER + SSD_GROUPS * SSD_D_STATE:].astype(BF16)

    row = lax.broadcasted_iota(jnp.int32, (q, LANES), 0) + c * q
    dt = _ssd_dt(small_ref[...], dt_bias_ref, row >= zero_rows)
    da = dt * (-jnp.exp(a_log_ref[...]))
    tril = _tril(q)
    acum = _sel_left(tril.astype(BF16), da)
    acum_t = acum.T
    expand = expand_ref[...]
    dt_exp = _sel_right(expand, dt)
    acum_exp = _sel_right(expand, acum)
    acum_last = acum_exp[q - 1:q, :]
    xdt = xs * dt_exp
    xdt_bf = xdt.astype(BF16)
    x_end = (xdt * jnp.exp(acum_last - acum_exp)).astype(BF16)

    lane_lo = lax.broadcasted_iota(jnp.int32, (q, LANES), 1) < SSD_HEAD_DIM
    for g in range(SSD_GROUPS):
        bg = bm[:, g * SSD_D_STATE:(g + 1) * SSD_D_STATE]
        cg = cm[:, g * SSD_D_STATE:(g + 1) * SSD_D_STATE]
        cb = _dot_nt(cg, bg)
        for hp in range(SSD_HPG // 2):
            lo = (g * SSD_HPG + 2 * hp) * SSD_HEAD_DIM
            pair = []
            for k in range(2):
                h = g * SSD_HPG + 2 * hp + k
                diff = acum[:, h:h + 1] - acum_t[h:h + 1, :]
                seg = jnp.exp(jnp.where(tril, diff, -jnp.inf))
                pair.append(_dot((cb * seg).astype(BF16), xdt_bf[:, lo:lo + LANES]))
            yacc_ref[:, lo:lo + LANES] = jnp.where(lane_lo, pair[0], pair[1])

        gs = slice(g * SSD_GROUP_W, (g + 1) * SSD_GROUP_W)
        st_g = st_ref[:, gs]
        y_off = _dot(cg, st_g.astype(BF16)) * jnp.exp(acum_exp[:, gs])
        yacc_ref[:, gs] = yacc_ref[:, gs] + y_off
        st_ref[:, gs] = st_g * jnp.exp(acum_last[:, gs]) + _dot(bg.T, x_end[:, gs])

    y = yacc_ref[...] + d_exp_ref[...] * xs
    y_ref[0] = _ssd_gated_norm(y, z_ref[...], norm_w_ref).astype(BF16)

    @pl.when(c == nc - 1)
    def _():
        state_ref[0] = st_ref[...].T


def ssd_prompt(proj, batch, n_chunks, zero_rows, state0, tail0, w):
    q = SSD_CHUNK

    def rows(b, c):
        return b * n_chunks + c

    return pl.pallas_call(
        functools.partial(_ssd_prompt_kernel, zero_rows=zero_rows),
        grid=(batch, n_chunks),
        in_specs=[
            pl.BlockSpec((q, SSD_D_INNER), lambda b, c: (rows(b, c), COL_Z // SSD_D_INNER)),
            pl.BlockSpec((q, SSD_D_INNER), lambda b, c: (rows(b, c), COL_XS // SSD_D_INNER)),
            pl.BlockSpec((q, SSD_BC), lambda b, c: (rows(b, c), COL_BC // SSD_BC)),
            pl.BlockSpec((q, LANES), lambda b, c: (rows(b, c), COL_SMALL // LANES)),
            _const_spec((SSD_CONV, SSD_CONV_DIM)),
            _const_spec((1, SSD_CONV_DIM)),
            _const_spec((1, LANES)),
            _const_spec((1, LANES)),
            _const_spec((1, SSD_D_INNER)),
            _const_spec((1, SSD_D_INNER)),
            _const_spec((LANES, SSD_D_INNER)),
            _const_spec((1, SSD_D_INNER, SSD_D_STATE)),
            _const_spec((1, SUBLANES, SSD_CONV_DIM)),
        ],
        out_specs=[
            pl.BlockSpec((1, q, SSD_D_INNER), lambda b, c: (b, c, 0)),
            pl.BlockSpec((1, SSD_D_INNER, SSD_D_STATE), lambda b, c: (b, 0, 0)),
            pl.BlockSpec((1, SUBLANES, SSD_CONV_DIM), lambda b, c: (b, 0, 0)),
        ],
        out_shape=[
            jax.ShapeDtypeStruct((batch, n_chunks * q, SSD_D_INNER), BF16),
            jax.ShapeDtypeStruct((batch, SSD_D_INNER, SSD_D_STATE), F32),
            jax.ShapeDtypeStruct((batch, SUBLANES, SSD_CONV_DIM), F32),
        ],
        scratch_shapes=[
            pltpu.VMEM((SUBLANES + q, SSD_CONV_DIM), F32),
            pltpu.VMEM((SSD_D_STATE, SSD_D_INNER), F32),
            pltpu.VMEM((q, SSD_D_INNER), F32),
        ],
        compiler_params=_cparams("parallel", "arbitrary"),
        name="ssd_prompt",
    )(proj, proj, proj, proj, w["conv_w"], w["conv_b"], w["dt_bias"], w["a_log"], w["d_exp"],
      w["ssd_norm_w"], w["expand"], state0, tail0)


def _ssd_step_pre_kernel(xs_ref, bc_ref, small_ref, c0_ref, c1_ref, c2_ref, conv_w_ref, conv_b_ref,
                         dt_bias_ref, a_log_ref, expand_ref,
                         xact_ref, xdt_ref, decay_ref, bm_ref, cm_ref, convnew_ref):
    x = jnp.concatenate([xs_ref[...], bc_ref[...]], axis=1)
    hist = (c0_ref[...], c1_ref[...], c2_ref[...], x)
    conv = conv_b_ref[...]
    for j in range(SSD_CONV):
        conv = conv + conv_w_ref[j:j + 1, :] * hist[j]
    xbc = jax.nn.silu(conv)
    xs = xbc[:, 0:SSD_D_INNER]
    dt = _ssd_dt(small_ref[...], dt_bias_ref, None)
    da = dt * (-jnp.exp(a_log_ref[...]))
    expand = expand_ref[...]
    xact_ref[...] = xs
    xdt_ref[...] = xs * _sel_right(expand, dt)
    decay_ref[...] = jnp.exp(da)
    bm_ref[...] = xbc[:, SSD_D_INNER:SSD_D_INNER + SSD_GROUPS * SSD_D_STATE]
    cm_ref[...] = xbc[:, SSD_D_INNER + SSD_GROUPS * SSD_D_STATE:]
    convnew_ref[:, 0:SSD_CONV_DIM] = hist[1]
    convnew_ref[:, SSD_CONV_DIM:2 * SSD_CONV_DIM] = hist[2]
    convnew_ref[:, 2 * SSD_CONV_DIM:] = x


def ssd_step_pre(proj, conv_state, w):
    t = proj.shape[0]
    tm = _row_tile(t, 128)
    gw = SSD_GROUPS * SSD_D_STATE
    row = lambda width, col: pl.BlockSpec((tm, width), lambda i: (i, col))
    return pl.pallas_call(
        _ssd_step_pre_kernel,
        grid=(t // tm,),
        in_specs=[
            row(SSD_D_INNER, COL_XS // SSD_D_INNER),
            row(SSD_BC, COL_BC // SSD_BC),
            row(LANES, COL_SMALL // LANES),
            row(SSD_CONV_DIM, 0), row(SSD_CONV_DIM, 1), row(SSD_CONV_DIM, 2),
            _const_spec((SSD_CONV, SSD_CONV_DIM)),
            _const_spec((1, SSD_CONV_DIM)),
            _const_spec((1, LANES)),
            _const_spec((1, LANES)),
            _const_spec((LANES, SSD_D_INNER)),
        ],
        out_specs=[row(SSD_D_INNER, 0), row(SSD_D_INNER, 0), row(LANES, 0), row(gw, 0),
                   row(gw, 0), row((SSD_CONV - 1) * SSD_CONV_DIM, 0)],
        out_shape=[
            jax.ShapeDtypeStruct((t, SSD_D_INNER), F32),
            jax.ShapeDtypeStruct((t, SSD_D_INNER), F32),
            jax.ShapeDtypeStruct((t, LANES), F32),
            jax.ShapeDtypeStruct((t, gw), F32),
            jax.ShapeDtypeStruct((t, gw), F32),
            jax.ShapeDtypeStruct((t, (SSD_CONV - 1) * SSD_CONV_DIM), F32),
        ],
        compiler_params=_cparams("parallel"),
        name="ssd_step_pre",
    )(proj, proj, proj, conv_state, conv_state, conv_state, w["conv_w"], w["conv_b"], w["dt_bias"],
      w["a_log"], w["expand"])


def _dot_tn(a, b):
    return lax.dot_general(a, b, (((0,), (0,)), ((), ())), preferred_element_type=F32)


def _own_row(j):
    return lax.broadcasted_iota(jnp.int32, (SUBLANES, 1), 0) == j


def _ssd_step_state_kernel(s_ref, xdt_ref, dec_ref, bm_ref, cm_ref, snew_ref, y_ref):
    j = pl.program_id(1)

    @pl.when(j == 0)
    def _():
        y_ref[...] = jnp.zeros(y_ref.shape, F32)

    def group_rows(x):
        return jnp.concatenate(
            [x[:, g * SSD_D_STATE:(g + 1) * SSD_D_STATE] for g in range(SSD_GROUPS)], axis=0)

    group_of_lane = lax.broadcasted_iota(jnp.int32, (SUBLANES, SSD_D_INNER), 1) // SSD_GROUP_W
    b_rows = group_rows(bm_ref[...]).astype(BF16)
    y_new = jnp.zeros(y_ref.shape, F32)
    for u in range(DECODE_STEP_SEQS):
        row = j * DECODE_STEP_SEQS + u
        own = _own_row(row)
        dec = dec_ref[pl.ds(row, 1), :]
        xdt = jnp.where(own, xdt_ref[...], 0.0)
        x_rows = jnp.concatenate(
            [jnp.where(group_of_lane == g, xdt, 0.0) for g in range(SSD_GROUPS)], axis=0).astype(BF16)
        c_rows = group_rows(jnp.where(own, cm_ref[...], 0.0)).astype(BF16)
        decayed = [s_ref[u, h * SSD_HEAD_DIM:(h + 1) * SSD_HEAD_DIM, :] * dec[:, h:h + 1]
                   for h in range(SSD_HEADS)]
        s_new = jnp.concatenate(decayed, axis=0) + _dot_tn(x_rows, b_rows)
        snew_ref[u] = s_new
        y_rows = _dot_nt(c_rows, s_new.astype(BF16))
        y_new = y_new + jnp.concatenate(
            [y_rows[g * SUBLANES:(g + 1) * SUBLANES, g * SSD_GROUP_W:(g + 1) * SSD_GROUP_W]
             for g in range(SSD_GROUPS)], axis=1)
    y_ref[...] += y_new


def _decode_state_specs(rows, cols):
    steps = SUBLANES // DECODE_STEP_SEQS
    shape = (DECODE_STEP_SEQS, rows, cols)
    index = lambda i, j: (i * steps + j, 0, 0)
    return pl.BlockSpec(shape, index), pl.BlockSpec(shape, index)


def ssd_step_state(state, xdt, dec, bm, cm):
    t = state.shape[0]
    assert t % SUBLANES == 0
    gw = SSD_GROUPS * SSD_D_STATE
    rows = lambda width: pl.BlockSpec((SUBLANES, width), lambda i, j: (i, 0))
    st_in, st_out = _decode_state_specs(SSD_D_INNER, SSD_D_STATE)
    return pl.pallas_call(
        _ssd_step_state_kernel,
        grid=(t // SUBLANES, SUBLANES // DECODE_STEP_SEQS),
        in_specs=[st_in, rows(SSD_D_INNER), rows(LANES), rows(gw), rows(gw)],
        out_specs=[st_out, rows(SSD_D_INNER)],
        out_shape=[
            jax.ShapeDtypeStruct((t, SSD_D_INNER, SSD_D_STATE), F32),
            jax.ShapeDtypeStruct((t, SSD_D_INNER), F32),
        ],
        compiler_params=_cparams("parallel", "arbitrary"),
        name="ssd_step_state",
    )(state, xdt, dec, bm, cm)


def _ssd_step_post_kernel(y_ref, xact_ref, z_ref, d_exp_ref, norm_w_ref, o_ref):
    y = y_ref[...] + d_exp_ref[...] * xact_ref[...]
    o_ref[...] = _ssd_gated_norm(y, z_ref[...], norm_w_ref).astype(BF16)


def ssd_step_post(y, xact, proj, w):
    t = y.shape[0]
    tm = _row_tile(t, 128)
    row = lambda col: pl.BlockSpec((tm, SSD_D_INNER), lambda i: (i, col))
    return pl.pallas_call(
        _ssd_step_post_kernel,
        grid=(t // tm,),
        in_specs=[row(0), row(0), row(COL_Z // SSD_D_INNER), _const_spec((1, SSD_D_INNER)),
                  _const_spec((1, SSD_D_INNER))],
        out_specs=row(0),
        out_shape=jax.ShapeDtypeStruct((t, SSD_D_INNER), BF16),
        compiler_params=_cparams("parallel"),
        name="ssd_step_post",
    )(y, xact, proj, w["d_exp"], w["ssd_norm_w"])


def _gla_gate(small, gate_up_ref, gate_b_ref):
    pre = _dot(small.astype(BF16), gate_up_ref[...]) + gate_b_ref[...]
    return jax.nn.log_sigmoid(pre) / GLA_GATE_NORMALIZER


def _gla_out_norm(o, g, norm_w_ref):
    parts = []
    for h in range(GLA_HEADS):
        parts.append(_rms_scale(o[:, h * GLA_DV:(h + 1) * GLA_DV]))
    return jnp.concatenate(parts, axis=1) * norm_w_ref[...] * jax.nn.silu(g)


def _gla_prompt_kernel(q_ref, k_ref, v_ref, g_ref, small_ref, gate_up_ref, gate_b_ref, norm_w_ref,
                       state0_ref, o_ref, state_ref, s_ref, *, zero_rows):
    c = pl.program_id(1)
    nc = pl.num_programs(1)
    q = GLA_CHUNK
    rows = q_ref.shape[0]
    n_sub = rows // q

    @pl.when(c == 0)
    def _():
        s_ref[...] = state0_ref[0]

    valid = (lax.broadcasted_iota(jnp.int32, (rows, GLA_KEY_DIM), 0) + c * rows) >= zero_rows
    gk = jnp.where(valid, _gla_gate(small_ref[...], gate_up_ref, gate_b_ref), 0.0)
    kk = jnp.where(valid, k_ref[...], 0.0)
    r = lax.broadcasted_iota(jnp.int32, (rows, rows), 0)
    cc = lax.broadcasted_iota(jnp.int32, (rows, rows), 1)
    chunk_tril = jnp.logical_and(r >= cc, r // q == cc // q)
    bcum = _sel_left(chunk_tril.astype(BF16), gk)
    lasts = [bcum[(i + 1) * q - 1:(i + 1) * q, :] for i in range(n_sub)]
    last_rows = jnp.concatenate([jnp.broadcast_to(x, (q, GLA_KEY_DIM)) for x in lasts], axis=0)
    qe = q_ref[...] * jnp.exp(bcum) * (GLA_DK ** -0.5)
    ke = kk * jnp.exp(-bcum)
    kend = kk * jnp.exp(last_rows - bcum)
    v_all = v_ref[...]
    vv = v_all.astype(BF16)
    chunk_of_row = lax.broadcasted_iota(jnp.int32, (rows, GLA_VAL_DIM), 0) // q
    v_chunk = [jnp.where(chunk_of_row == i, v_all, 0.0).astype(BF16) for i in range(n_sub)]
    outs = [[] for _ in range(n_sub)]
    for h in range(GLA_HEADS):
        ks = slice(h * GLA_DK, (h + 1) * GLA_DK)
        vs = slice(h * GLA_DV, (h + 1) * GLA_DV)
        qe_h = qe[:, ks].astype(BF16)
        att = jnp.where(chunk_tril, _dot_nt(qe_h, ke[:, ks].astype(BF16)), 0.0)
        o_intra = _dot(att.astype(BF16), vv[:, vs])
        kend_t = kend[:, ks].T.astype(BF16)
        s_h = s_ref[ks, :]
        for i in range(n_sub):
            rs = slice(i * q, (i + 1) * q)
            outs[i].append(o_intra[rs, :] + _dot(qe_h[rs, :], s_h.astype(BF16)))
            decay_col = jnp.broadcast_to(jnp.exp(lasts[i][:, ks]), (GLA_DK, GLA_DK)).T
            s_h = (jnp.concatenate([decay_col] * (GLA_DV // GLA_DK), axis=1) * s_h
                   + _dot(kend_t, v_chunk[i][:, vs]))
        s_ref[ks, :] = s_h
    o = jnp.concatenate([jnp.concatenate(o_i, axis=1) for o_i in outs], axis=0)
    o_ref[0] = _gla_out_norm(o, g_ref[...], norm_w_ref).astype(BF16)

    @pl.when(c == nc - 1)
    def _():
        state_ref[0] = s_ref[...]


def gla_prompt(proj, batch, n_rows, zero_rows, state0, w):
    rows = min(GLA_CHUNK * GLA_STEP_CHUNKS, n_rows)
    assert n_rows % rows == 0
    n_steps = n_rows // rows

    def blk(b, c):
        return b * n_steps + c

    return pl.pallas_call(
        functools.partial(_gla_prompt_kernel, zero_rows=zero_rows),
        grid=(batch, n_steps),
        in_specs=[
            pl.BlockSpec((rows, GLA_KEY_DIM), lambda b, c: (blk(b, c), COL_Q // GLA_KEY_DIM)),
            pl.BlockSpec((rows, GLA_KEY_DIM), lambda b, c: (blk(b, c), COL_K // GLA_KEY_DIM)),
            pl.BlockSpec((rows, GLA_VAL_DIM), lambda b, c: (blk(b, c), COL_V // GLA_VAL_DIM)),
            pl.BlockSpec((rows, GLA_VAL_DIM), lambda b, c: (blk(b, c), COL_G // GLA_VAL_DIM)),
            pl.BlockSpec((rows, LANES), lambda b, c: (blk(b, c), COL_SMALL // LANES)),
            _const_spec((LANES, GLA_KEY_DIM)),
            _const_spec((1, GLA_KEY_DIM)),
            _const_spec((1, GLA_VAL_DIM)),
            _const_spec((1, GLA_KEY_DIM, GLA_DV)),
        ],
        out_specs=[
            pl.BlockSpec((1, rows, GLA_VAL_DIM), lambda b, c: (b, c, 0)),
            pl.BlockSpec((1, GLA_KEY_DIM, GLA_DV), lambda b, c: (b, 0, 0)),
        ],
        out_shape=[
            jax.ShapeDtypeStruct((batch, n_steps * rows, GLA_VAL_DIM), BF16),
            jax.ShapeDtypeStruct((batch, GLA_KEY_DIM, GLA_DV), F32),
        ],
        scratch_shapes=[pltpu.VMEM((GLA_KEY_DIM, GLA_DV), F32)],
        compiler_params=_cparams("parallel", "arbitrary"),
        name="gla_prompt",
    )(proj, proj, proj, proj, proj, w["gate_up"], w["gate_b"], w["gla_norm_w"], state0)


def _gla_step_kernel(s_ref, q_ref, k_ref, v_ref, g_ref, small_ref, gate_up_ref, gate_b_ref, norm_w_ref,
                     snew_ref, o_ref, dec_ref, acc_ref):
    j = pl.program_id(1)

    @pl.when(j == 0)
    def _():
        dec_ref[...] = jnp.exp(_gla_gate(small_ref[...], gate_up_ref, gate_b_ref))
        acc_ref[...] = jnp.zeros(acc_ref.shape, F32)

    head_of_lane = lax.broadcasted_iota(jnp.int32, (SUBLANES, GLA_KEY_DIM), 1) // GLA_DK

    def per_head_rows(x):
        return jnp.concatenate([jnp.where(head_of_lane == h, x, 0.0) for h in range(GLA_HEADS)], axis=0)

    vv = v_ref[...]
    v_rows = jnp.concatenate(
        [vv[:, h * GLA_DV:(h + 1) * GLA_DV] for h in range(GLA_HEADS)], axis=0).astype(BF16)
    o_new = jnp.zeros(acc_ref.shape, F32)
    for u in range(DECODE_STEP_SEQS):
        own = _own_row(j * DECODE_STEP_SEQS + u)
        k_rows = per_head_rows(jnp.where(own, k_ref[...], 0.0)).astype(BF16)
        q_rows = per_head_rows(jnp.where(own, q_ref[...] * (GLA_DK ** -0.5), 0.0)).astype(BF16)
        pieces = [p.astype(F32) for p in _split3(jnp.where(own, dec_ref[...], 0.0))]
        d_rows = jnp.concatenate(pieces + [jnp.zeros_like(pieces[0])], axis=0).astype(BF16)
        decay = _dot_tn(d_rows, jnp.ones((d_rows.shape[0], GLA_DV), BF16))
        s_new = s_ref[u] * decay + _dot_tn(k_rows, v_rows)
        snew_ref[u] = s_new
        o_rows = _dot(q_rows, s_new.astype(BF16))
        o_new = o_new + jnp.concatenate(
            [o_rows[h * SUBLANES:(h + 1) * SUBLANES, :] for h in range(GLA_HEADS)], axis=1)
    acc_ref[...] += o_new

    @pl.when(j == pl.num_programs(1) - 1)
    def _():
        o_ref[...] = _gla_out_norm(acc_ref[...], g_ref[...], norm_w_ref)


def gla_step(state, proj, w):
    t = state.shape[0]
    assert t % SUBLANES == 0
    rows = lambda width, col: pl.BlockSpec((SUBLANES, width), lambda i, j: (i, col))
    st_in, st_out = _decode_state_specs(GLA_KEY_DIM, GLA_DV)
    return pl.pallas_call(
        _gla_step_kernel,
        grid=(t // SUBLANES, SUBLANES // DECODE_STEP_SEQS),
        in_specs=[st_in, rows(GLA_KEY_DIM, COL_Q // GLA_KEY_DIM), rows(GLA_KEY_DIM, COL_K // GLA_KEY_DIM),
                  rows(GLA_VAL_DIM, COL_V // GLA_VAL_DIM), rows(GLA_VAL_DIM, COL_G // GLA_VAL_DIM),
                  rows(LANES, COL_SMALL // LANES), _const_spec((LANES, GLA_KEY_DIM)),
                  _const_spec((1, GLA_KEY_DIM)), _const_spec((1, GLA_VAL_DIM))],
        out_specs=[st_out, rows(GLA_VAL_DIM, 0)],
        out_shape=[
            jax.ShapeDtypeStruct((t, GLA_KEY_DIM, GLA_DV), F32),
            jax.ShapeDtypeStruct((t, GLA_VAL_DIM), F32),
        ],
        scratch_shapes=[pltpu.VMEM((SUBLANES, GLA_KEY_DIM), F32), pltpu.VMEM((SUBLANES, GLA_VAL_DIM), F32)],
        compiler_params=_cparams("parallel", "arbitrary"),
        name="gla_step",
    )(state, proj, proj, proj, proj, proj, w["gate_up"], w["gate_b"], w["gla_norm_w"])


def _merge_kernel(x_ref, ys_ref, og_ref, nw_ref, wm_ref, wso_ref, wgo_ref, wout_ref, o_ref):
    x = x_ref[...]
    xn = (_rms_scale(x) * nw_ref[...]).astype(BF16)
    gates = jax.nn.sigmoid(_dot(xn, wm_ref[...]))
    y_ssd = _dot(ys_ref[...].astype(BF16), wso_ref[...])
    y_gla = _dot(og_ref[...].astype(BF16), wgo_ref[...])
    mix = gates[:, 0:D_MODEL] * y_ssd + gates[:, D_MODEL:] * y_gla
    o_ref[...] = x + _dot(mix.astype(BF16), wout_ref[...])


def merge(x, y_ssd, o_gla, w):
    t = x.shape[0]
    tm = _row_tile(t, 512)
    row = lambda width: pl.BlockSpec((tm, width), lambda i: (i, 0))
    return pl.pallas_call(
        _merge_kernel,
        grid=(t // tm,),
        in_specs=[row(D_MODEL), row(SSD_D_INNER), row(GLA_VAL_DIM), _const_spec((1, D_MODEL)),
                  _const_spec((D_MODEL, 2 * D_MODEL)), _const_spec((SSD_D_INNER, D_MODEL)),
                  _const_spec((GLA_VAL_DIM, D_MODEL)), _const_spec((D_MODEL, D_MODEL))],
        out_specs=row(D_MODEL),
        out_shape=jax.ShapeDtypeStruct((t, D_MODEL), F32),
        compiler_params=_cparams("parallel"),
        name="merge",
    )(x, y_ssd, o_gla, w["norm1_w"], w["w_merge"], w["w_ssd_out"], w["w_gla_out"], w["w_out"])


def _merge_exchange_pairs(n):
    pairs = []
    p = n // 2
    while p > 0:
        q, r, d = n // 2, 0, p
        while d > 0:
            pairs += [(i, i + d) for i in range(n - d) if (i & p) == r]
            d, q, r = q - p, q // 2, p
        p //= 2
    return pairs


_SORT16 = _merge_exchange_pairs(PEER_TOPK)
_BITONIC16 = [(i, i + d) for d in (8, 4, 2, 1) for i in range(PEER_TOPK) if (i & d) == 0]


def _exchange(v, ids, i, j):
    if ids is None:
        v[i], v[j] = jnp.maximum(v[i], v[j]), jnp.minimum(v[i], v[j])
    else:
        ge = v[i] >= v[j]
        v[i], v[j] = jnp.where(ge, v[i], v[j]), jnp.where(ge, v[j], v[i])
        ids[i], ids[j] = jnp.where(ge, ids[i], ids[j]), jnp.where(ge, ids[j], ids[i])


def _merge_top16(v, ids, w, wids):
    n, m = len(v), len(w)
    for i in range(n - m, n):
        o = n - 1 - i
        if ids is None:
            v[i] = jnp.maximum(v[i], w[o])
        else:
            ge = v[i] >= w[o]
            v[i] = jnp.where(ge, v[i], w[o])
            ids[i] = jnp.where(ge, ids[i], wids[o])
    for i, j in _BITONIC16:
        _exchange(v, ids, i, j)


def _top16_rows(x, with_ids):
    n = x.shape[0] // SUBLANES
    assert n == PEER_TOPK
    v = [x[SUBLANES * r:SUBLANES * (r + 1), :] for r in range(n)]
    ids = None
    if with_ids:
        row = lax.broadcasted_iota(jnp.int32, v[0].shape, 0).astype(F32)
        ids = [row + float(SUBLANES * r) for r in range(n)]
    for i, j in _SORT16:
        _exchange(v, ids, i, j)
    shift = SUBLANES // 2
    while shift:
        w = [pltpu.roll(a, shift, 0) for a in v]
        wids = [pltpu.roll(a, shift, 0) for a in ids] if with_ids else None
        _merge_top16(v, ids, w, wids)
        shift //= 2
    return v, ids


def _peer_kernel(x_ref, nw_ref, fnw_ref, wqt_ref, keys_ref, u_ref, v_ref, o_ref,
                 hn_ref, qt_ref, rw_ref, s2t_ref, svk_ref, sik_ref, stat_ref, g1k_ref, ids_ref, g1tm_ref,
                 kept8_ref, e8_ref, hid_ref, acc_ref):
    s = pl.program_id(1)
    ns = pl.num_programs(1)
    tb = x_ref.shape[0]
    nk = PEER_N_KEYS
    nj = PEER_HEADS * PEER_TOPK

    @pl.when(s == 0)
    def _select():
        acc_ref[...] = jnp.zeros(acc_ref.shape, F32)
        hn_ref[...] = (_rms_scale(x_ref[...]) * nw_ref[...]).astype(BF16)
        stat_ref[...] = jnp.zeros(stat_ref.shape, F32)
        qt_ref[...] = _dot_nt(wqt_ref[...], hn_ref[...]).astype(BF16)
        for h in range(PEER_HEADS):
            for c in range(2):
                r0 = (h * 2 + c) * (PEER_DQ // 2)
                sc_t = _dot(keys_ref[h * 2 + c], qt_ref[r0:r0 + PEER_DQ // 2, :])
                for g in range(tb // LANES):
                    ls = slice(g * LANES, (g + 1) * LANES)
                    vals, ids = _top16_rows(sc_t[:, ls], c == 0)
                    for k in range(PEER_TOPK):
                        svk_ref[c, k, h:h + 1, ls] = vals[k][0:1, :]
                        if c == 0:
                            sik_ref[k, h:h + 1, ls] = ids[k][0:1, :]
                if c == 1:
                    s2t_ref[h] = sc_t

        pairs = ([(0, k2) for k2 in range(PEER_TOPK)]
                 + [(k1, k2) for k1 in range(1, PEER_TOPK // 2) for k2 in range(PEER_TOPK // (k1 + 1))]
                 + [(k1, 0) for k1 in range(PEER_TOPK // 2, PEER_TOPK)])
        for g in range(tb // LANES):
            ls = slice(g * LANES, (g + 1) * LANES)
            a = [svk_ref[0, k, :, ls] for k in range(PEER_TOPK)]
            b = [svk_ref[1, k, :, ls] for k in range(PEER_TOPK)]
            best = [a[0] + b[k2] for k2 in range(PEER_TOPK)]
            for k1 in range(1, PEER_TOPK // 2):
                _merge_top16(best, None, [a[k1] + b[k2] for k2 in range(PEER_TOPK // (k1 + 1))], None)
            _merge_top16(best, None, [a[k1] + b[0] for k1 in range(PEER_TOPK // 2, PEER_TOPK)], None)
            tau, top = best[PEER_TOPK - 1], best[0]
            z = jnp.zeros_like(tau)
            for k1, k2 in pairs:
                cv = a[k1] + b[k2]
                z = z + jnp.where(cv >= tau, jnp.exp(cv - top), 0.0)
            stat_ref[0:PEER_HEADS, ls] = tau
            stat_ref[PEER_HEADS:2 * PEER_HEADS, ls] = b[0]
            inv_z = 1.0 / z
            for k in range(PEER_TOPK):
                g1k_ref[k, :, ls] = jnp.exp(a[k] - a[0]) * inv_z

        ids_ref[...] = sik_ref[...].reshape(nj, tb).T
        g1tm_ref[...] = g1k_ref[...].reshape(nj, tb).T
        def count_kept(h, carry):
            sc2 = s2t_ref[h]
            tau_t = stat_ref[pl.ds(h, 1), :]
            kept = jnp.zeros(sc2.shape, F32)
            for k1 in range(PEER_TOPK):
                kept = jnp.where((sc2 + svk_ref[0, k1, pl.ds(h, 1), :]) >= tau_t, float(k1 + 1), kept)
            e2 = jnp.exp(sc2 - stat_ref[pl.ds(PEER_HEADS + h, 1), :])
            kept8_ref[pl.ds(h, tb, stride=PEER_HEADS), :] = kept.T
            e8_ref[pl.ds(h, tb, stride=PEER_HEADS), :] = e2.T
            return carry

        lax.fori_loop(0, PEER_HEADS, count_kept, 0)

        key1 = lax.broadcasted_iota(jnp.int32, (nk, nk), 0).astype(F32)

        def scatter(t, carry):
            hb = pl.multiple_of(t * PEER_HEADS, PEER_HEADS)
            kept8 = kept8_ref[pl.ds(hb, PEER_HEADS), :]
            e8 = e8_ref[pl.ds(hb, PEER_HEADS), :]
            rows = jnp.concatenate(
                [jnp.where(kept8 > float(k1), e8, 0.0) for k1 in range(PEER_TOPK)], axis=0).astype(BF16)
            place = jnp.where(key1 == ids_ref[pl.ds(t, 1), :], g1tm_ref[pl.ds(t, 1), :], 0.0)
            rw_ref[pl.ds(t * PEER_PITCH, nk), :] = _dot(place.astype(BF16), rows)
            return carry

        lax.fori_loop(0, tb, scatter, 0, unroll=PEER_SCATTER_UNROLL)

    def produce():
        hn = hn_ref[...]
        half = PEER_STEP_EXPERTS // 2
        for c in range(2):
            hid_ref[:, c * half:(c + 1) * half] = _dot_nt(hn, u_ref[c * half:(c + 1) * half, :])

    def gated_prev():
        prev = hid_ref[...]
        act = 0.5 * prev * (1.0 + lax.erf(prev * (2.0 ** -0.5)))
        slab0 = (s - 1) * PEER_SLABS_PER_STEP
        wts = jnp.concatenate(
            [rw_ref[pl.ds(slab0 + i, tb, stride=PEER_PITCH), :] for i in range(PEER_SLABS_PER_STEP)],
            axis=1)
        return (act * wts).astype(BF16)

    def consume(gated):
        half = D_MODEL // 2
        for c in range(2):
            acc_ref[:, c * half:(c + 1) * half] += _dot(gated, v_ref[:, c * half:(c + 1) * half])

    @pl.when(s == 0)
    def _():
        produce()

    @pl.when(jnp.logical_and(s > 0, s < ns - 1))
    def _():
        gated = gated_prev()
        produce()
        consume(gated)

    @pl.when(s == ns - 1)
    def _():
        consume(gated_prev())
        o_ref[...] = _rms_scale(x_ref[...] + acc_ref[...]) * fnw_ref[...]


def peer_final(x, w):
    t = x.shape[0]
    tb = _row_tile(t, PEER_TOKEN_BLOCK)
    assert tb % LANES == 0
    nblk = PEER_N_EXPERTS // PEER_STEP_EXPERTS
    return pl.pallas_call(
        _peer_kernel,
        grid=(t // tb, nblk + 1),
        in_specs=[
            pl.BlockSpec((tb, D_MODEL), lambda i, s: (i, 0)),
            _const_spec((1, D_MODEL)),
            _const_spec((1, D_MODEL)),
            _const_spec((PEER_HEADS * PEER_DQ, D_MODEL)),
            _const_spec((2 * PEER_HEADS, PEER_N_KEYS, PEER_DQ // 2)),
            pl.BlockSpec((PEER_STEP_EXPERTS, D_MODEL), lambda i, s: (jnp.minimum(s, nblk - 1), 0)),
            pl.BlockSpec((PEER_STEP_EXPERTS, D_MODEL), lambda i, s: (jnp.maximum(s - 1, 0), 0)),
        ],
        out_specs=pl.BlockSpec((tb, D_MODEL), lambda i, s: (i, 0)),
        out_shape=jax.ShapeDtypeStruct((t, D_MODEL), F32),
        scratch_shapes=[
            pltpu.VMEM((tb, D_MODEL), BF16),
            pltpu.VMEM((PEER_HEADS * PEER_DQ, tb), BF16),
            pltpu.VMEM((tb * PEER_PITCH, PEER_N_KEYS), F32),
            pltpu.VMEM((PEER_HEADS, PEER_N_KEYS, tb), F32),
            pltpu.VMEM((2, PEER_TOPK, PEER_HEADS, tb), F32),
            pltpu.VMEM((PEER_TOPK, PEER_HEADS, tb), F32),
            pltpu.VMEM((LANES, tb), F32),
            pltpu.VMEM((PEER_TOPK, PEER_HEADS, tb), F32),
            pltpu.VMEM((tb, PEER_HEADS * PEER_TOPK), F32),
            pltpu.VMEM((tb, PEER_HEADS * PEER_TOPK), F32),
            pltpu.VMEM((tb * PEER_HEADS, PEER_N_KEYS), F32),
            pltpu.VMEM((tb * PEER_HEADS, PEER_N_KEYS), F32),
            pltpu.VMEM((tb, PEER_STEP_EXPERTS), F32),
            pltpu.VMEM((tb, D_MODEL), F32),
        ],
        compiler_params=_cparams("parallel", "arbitrary"),
        name="peer_final",
    )(x, w["norm2_w"], w["final_norm_w"], w["peer_wq_t"], w["peer_keys"], w["peer_u"], w["peer_v"])


def _prepare_weights(norm1_w, w_in, conv_w, conv_b, dt_bias, a_log, d_skip, ssd_norm_w, w_ssd_out,
                     gla_gate_up, gla_gate_b, gla_norm_w, w_gla_out, w_out, norm2_w, peer_w_q,
                     peer_sub_keys, peer_u, peer_v, final_norm_w):
    offs = [0]
    for sz in _IN_SIZES:
        offs.append(offs[-1] + sz)
    z0, xbc0, dt0, q0, _, _, _, glr0, mg0, end = offs
    small = jnp.concatenate(
        [w_in[:, dt0:q0], w_in[:, glr0:mg0],
         jnp.zeros((D_MODEL, SMALL_COLS - SSD_HEADS - GLA_GATE_RANK), w_in.dtype)], axis=1)
    w_proj = jnp.concatenate([w_in[:, z0:dt0], w_in[:, q0:glr0], small], axis=1).astype(BF16)
    assert w_proj.shape[1] == IN_COLS

    def lane_pad(v):
        return jnp.pad(v, (0, LANES - v.shape[0])).reshape(1, LANES)

    head_of_col = jnp.arange(SSD_D_INNER) // SSD_HEAD_DIM
    expand = (jnp.arange(LANES)[:, None] == head_of_col[None, :]).astype(BF16)
    gate_up = jnp.zeros((LANES, GLA_KEY_DIM), F32).at[SMALL_GLR:SMALL_GLR + GLA_GATE_RANK].set(gla_gate_up)
    keys = jnp.transpose(peer_sub_keys, (1, 0, 2, 3)).reshape(2 * PEER_HEADS, PEER_N_KEYS, PEER_DQ // 2)
    return {
        "norm1_w": norm1_w.reshape(1, D_MODEL),
        "w_proj": w_proj,
        "w_merge": w_in[:, mg0:end].astype(BF16),
        "conv_w": conv_w,
        "conv_b": conv_b.reshape(1, SSD_CONV_DIM),
        "dt_bias": lane_pad(dt_bias),
        "a_log": lane_pad(a_log),
        "d_exp": jnp.repeat(d_skip, SSD_HEAD_DIM).reshape(1, SSD_D_INNER),
        "ssd_norm_w": ssd_norm_w.reshape(1, SSD_D_INNER),
        "expand": expand,
        "w_ssd_out": w_ssd_out.astype(BF16),
        "gate_up": gate_up.astype(BF16),
        "gate_b": gla_gate_b.reshape(1, GLA_KEY_DIM),
        "gla_norm_w": jnp.tile(gla_norm_w, GLA_HEADS).reshape(1, GLA_VAL_DIM),
        "w_gla_out": w_gla_out.astype(BF16),
        "w_out": w_out.astype(BF16),
        "norm2_w": norm2_w.reshape(1, D_MODEL),
        "peer_wq_t": peer_w_q.T.astype(BF16),
        "peer_keys": keys.astype(BF16),
        "peer_u": peer_u.astype(BF16),
        "peer_v": peer_v.astype(BF16),
        "final_norm_w": final_norm_w.reshape(1, D_MODEL),
    }


def _prompt_path(x_prompt, meta_tokens, w):
    b, seq, _ = x_prompt.shape
    assert seq % SSD_CHUNK == 0
    meta_rows = jnp.concatenate(
        [jnp.zeros((PROMPT_ZERO_ROWS, D_MODEL), x_prompt.dtype), meta_tokens.astype(x_prompt.dtype)], axis=0)
    meta_proj = norm_matmul(meta_rows, w["norm1_w"], w["w_proj"], IN_COL_TILE)
    zero_ssd = jnp.zeros((1, SSD_D_INNER, SSD_D_STATE), F32)
    zero_tail = jnp.zeros((1, SUBLANES, SSD_CONV_DIM), F32)
    zero_gla = jnp.zeros((1, GLA_KEY_DIM, GLA_DV), F32)
    _, ssd0, tail0 = ssd_prompt(meta_proj, 1, 1, PROMPT_ZERO_ROWS, zero_ssd, zero_tail, w)
    _, gla0 = gla_prompt(meta_proj, 1, SSD_CHUNK, PROMPT_ZERO_ROWS, zero_gla, w)
    proj = norm_matmul(x_prompt.reshape(b * seq, D_MODEL), w["norm1_w"], w["w_proj"], IN_COL_TILE)
    y_ssd, st_ssd, conv_tail = ssd_prompt(proj, b, seq // SSD_CHUNK, 0, ssd0, tail0, w)
    o_gla, st_gla = gla_prompt(proj, b, seq, 0, gla0, w)
    x1 = merge(x_prompt.reshape(b * seq, D_MODEL), y_ssd.reshape(b * seq, SSD_D_INNER),
               o_gla.reshape(b * seq, GLA_VAL_DIM), w)
    y = peer_final(x1, w).reshape(b, seq, D_MODEL)
    return (y,
            st_ssd.reshape(1, b, SSD_HEADS, SSD_HEAD_DIM, SSD_D_STATE),
            conv_tail[:, SUBLANES - (SSD_CONV - 1):, :][None],
            st_gla.reshape(1, b, GLA_HEADS, GLA_DK, GLA_DV))


def _sample_path(x_sample, state_ssd, state_conv, state_gla, w):
    b = x_sample.shape[0]
    x = x_sample.reshape(b, D_MODEL)
    proj = norm_matmul(x, w["norm1_w"], w["w_proj"], IN_COL_TILE)
    xact, xdt, dec, bm, cm, conv_new = ssd_step_pre(
        proj, state_conv.reshape(b, (SSD_CONV - 1) * SSD_CONV_DIM), w)
    st_ssd, y = ssd_step_state(state_ssd.reshape(b, SSD_D_INNER, SSD_D_STATE), xdt, dec, bm, cm)
    y_ssd = ssd_step_post(y, xact, proj, w)
    st_gla, o_gla = gla_step(state_gla.reshape(b, GLA_KEY_DIM, GLA_DV), proj, w)
    x1 = merge(x, y_ssd, o_gla, w)
    y = peer_final(x1, w).reshape(b, 1, D_MODEL)
    return (y,
            st_ssd.reshape(1, b, SSD_HEADS, SSD_HEAD_DIM, SSD_D_STATE),
            conv_new.reshape(1, b, SSD_CONV - 1, SSD_CONV_DIM),
            st_gla.reshape(1, b, GLA_HEADS, GLA_DK, GLA_DV))


def kernel(x_prompt, x_sample, state_ssd, state_conv, state_gla, meta_tokens, norm1_w, w_in, conv_w, conv_b,
           dt_bias, a_log, d_skip, ssd_norm_w, w_ssd_out, gla_gate_up, gla_gate_b, gla_norm_w, w_gla_out,
           w_out, norm2_w, peer_w_q, peer_sub_keys, peer_u, peer_v, final_norm_w):
    layer = (norm1_w, w_in, conv_w, conv_b, dt_bias, a_log, d_skip, ssd_norm_w, w_ssd_out, gla_gate_up,
             gla_gate_b, gla_norm_w, w_gla_out, w_out, norm2_w, peer_w_q, peer_sub_keys, peer_u, peer_v)
    assert all(p.shape[0] == 1 for p in layer), "single-layer step"
    w = _prepare_weights(*[p[0] for p in layer], final_norm_w)
    yp, p_ssd, p_conv, p_gla = _prompt_path(x_prompt, meta_tokens, w)
    ys, s_ssd, s_conv, s_gla = _sample_path(x_sample, state_ssd[0], state_conv[0], state_gla[0], w)
    return (yp, ys, p_ssd, p_conv, p_gla, s_ssd, s_conv, s_gla)
```

```python
import functools

import jax
import jax.numpy as jnp
from jax import lax
from jax.experimental import pallas as pl
from jax.experimental.pallas import tpu as pltpu

F32 = jnp.float32
BF16 = jnp.bfloat16

LANES = 128
SUBLANES = 8
VMEM_LIMIT_BYTES = 60 * 1024 * 1024

D_MODEL = 1024
N_META = 16
EPS = 1e-6
SSD_D_INNER = 2 * D_MODEL
SSD_HEAD_DIM = 64
SSD_HEADS = SSD_D_INNER // SSD_HEAD_DIM
SSD_GROUPS = 4
SSD_HPG = SSD_HEADS // SSD_GROUPS
SSD_D_STATE = 128
SSD_CONV = 4
SSD_CHUNK = 128
SSD_BC = 2 * SSD_GROUPS * SSD_D_STATE
SSD_CONV_DIM = SSD_D_INNER + SSD_BC
SSD_GROUP_W = SSD_HPG * SSD_HEAD_DIM
GLA_HEADS = 4
GLA_KEY_DIM = D_MODEL // 2
GLA_VAL_DIM = D_MODEL
GLA_DK = GLA_KEY_DIM // GLA_HEADS
GLA_DV = GLA_VAL_DIM // GLA_HEADS
GLA_GATE_RANK = 16
GLA_GATE_NORMALIZER = 16.0
GLA_CHUNK = 64
GLA_STEP_CHUNKS = 4
DECODE_STEP_SEQS = 8
PEER_HEADS = 8
PEER_N_KEYS = 128
PEER_N_EXPERTS = PEER_N_KEYS * PEER_N_KEYS
PEER_DQ = 256
PEER_TOPK = 16
PEER_SLABS_PER_STEP = 16
PEER_STEP_EXPERTS = PEER_SLABS_PER_STEP * PEER_N_KEYS
PEER_SCATTER_UNROLL = 16
PEER_PITCH = PEER_N_KEYS + 4
PEER_TOKEN_BLOCK = 256

_IN_SIZES = (SSD_D_INNER, SSD_CONV_DIM, SSD_HEADS, GLA_KEY_DIM, GLA_KEY_DIM, GLA_VAL_DIM,
             GLA_VAL_DIM, GLA_GATE_RANK, 2 * D_MODEL)
COL_Z = 0
COL_XS = COL_Z + SSD_D_INNER
COL_BC = COL_XS + SSD_D_INNER
COL_Q = COL_BC + SSD_BC
COL_K = COL_Q + GLA_KEY_DIM
COL_V = COL_K + GLA_KEY_DIM
COL_G = COL_V + GLA_VAL_DIM
COL_SMALL = COL_G + GLA_VAL_DIM
MXU_WIDTH = 256
SMALL_COLS = MXU_WIDTH
IN_COLS = COL_SMALL + SMALL_COLS
IN_COL_TILE = IN_COLS // 3
SMALL_GLR = SSD_HEADS
PROMPT_ZERO_ROWS = SSD_CHUNK - N_META


def _cparams(*sem):
    return pltpu.CompilerParams(dimension_semantics=sem, vmem_limit_bytes=VMEM_LIMIT_BYTES)


def _const_spec(shape):
    nd = len(shape)
    return pl.BlockSpec(shape, lambda *_: (0,) * nd, pipeline_mode=pl.Buffered(1))


def _row_tile(t, cap):
    tm = cap
    while t % tm:
        tm //= 2
    return tm


def _split3(x):
    hi = x.astype(BF16)
    r1 = x - hi.astype(F32)
    mid = r1.astype(BF16)
    lo = (r1 - mid.astype(F32)).astype(BF16)
    return hi, mid, lo


def _dot(a, b):
    return jnp.dot(a, b, preferred_element_type=F32)


def _dot_nt(a, b):
    return lax.dot_general(a, b, (((1,), (1,)), ((), ())), preferred_element_type=F32)


def _sel_right(sel01, x):
    hi, mid, lo = _split3(x)
    return _dot(hi, sel01) + _dot(mid, sel01) + _dot(lo, sel01)


def _sel_left(sel01, x):
    hi, mid, lo = _split3(x)
    return _dot(sel01, hi) + _dot(sel01, mid) + _dot(sel01, lo)


def _rms_scale(x):
    return x * lax.rsqrt(jnp.mean(x * x, axis=-1, keepdims=True) + EPS)


def _tril(n):
    r = lax.broadcasted_iota(jnp.int32, (n, n), 0)
    c = lax.broadcasted_iota(jnp.int32, (n, n), 1)
    return r >= c


def _norm_matmul_kernel(x_ref, nw_ref, w_ref, o_ref, xn_ref):
    @pl.when(pl.program_id(1) == 0)
    def _():
        xn_ref[...] = (_rms_scale(x_ref[...]) * nw_ref[...]).astype(BF16)

    o_ref[...] = _dot(xn_ref[...], w_ref[...])


def norm_matmul(x, norm_w, w, tn):
    t, d = x.shape
    n = w.shape[1]
    tm = _row_tile(t, 1024)
    assert n % tn == 0
    return pl.pallas_call(
        _norm_matmul_kernel,
        grid=(t // tm, n // tn),
        in_specs=[
            pl.BlockSpec((tm, d), lambda i, j: (i, 0)),
            pl.BlockSpec((1, d), lambda i, j: (0, 0)),
            pl.BlockSpec((d, tn), lambda i, j: (0, j)),
        ],
        out_specs=pl.BlockSpec((tm, tn), lambda i, j: (i, j)),
        out_shape=jax.ShapeDtypeStruct((t, n), F32),
        scratch_shapes=[pltpu.VMEM((tm, d), BF16)],
        compiler_params=_cparams("parallel", "arbitrary"),
        name="norm_in_proj",
    )(x, norm_w, w)


def _ssd_dt(small, dt_bias_ref, valid):
    lane = lax.broadcasted_iota(jnp.int32, small.shape, 1)
    dt = jax.nn.softplus(small + dt_bias_ref[...])
    keep = lane < SSD_HEADS
    if valid is not None:
        keep = jnp.logical_and(keep, valid)
    return jnp.where(keep, dt, 0.0)


def _ssd_gated_norm(y, z, norm_w_ref):
    y = y * jax.nn.silu(z)
    parts = []
    for g in range(SSD_GROUPS):
        parts.append(_rms_scale(y[:, g * SSD_GROUP_W:(g + 1) * SSD_GROUP_W]))
    return jnp.concatenate(parts, axis=1) * norm_w_ref[...]


def _ssd_prompt_kernel(z_ref, xs_ref, bc_ref, small_ref, conv_w_ref, conv_b_ref, dt_bias_ref,
                       a_log_ref, d_exp_ref, norm_w_ref, expand_ref, state0_ref, tail0_ref,
                       y_ref, state_ref, convtail_ref,
                       hist_ref, st_ref, yacc_ref, *, zero_rows):
    c = pl.program_id(1)
    nc = pl.num_programs(1)
    q = SSD_CHUNK

    @pl.when(c == 0)
    def _():
        hist_ref[0:SUBLANES, :] = tail0_ref[0]
        st_ref[...] = state0_ref[0].T

    hist_ref[SUBLANES:SUBLANES + q, 0:SSD_D_INNER] = xs_ref[...]
    hist_ref[SUBLANES:SUBLANES + q, SSD_D_INNER:SSD_CONV_DIM] = bc_ref[...]
    conv = conv_b_ref[...]
    for j in range(SSD_CONV):
        start = SUBLANES - (SSD_CONV - 1) + j
        conv = conv + conv_w_ref[j:j + 1, :] * hist_ref[start:start + q, :]
    xbc = jax.nn.silu(conv)
    tail = hist_ref[q:q + SUBLANES, :]
    hist_ref[0:SUBLANES, :] = tail
    convtail_ref[0] = tail

    xs = xbc[:, 0:SSD_D_INNER]
    bm = xbc[:, SSD_D_INNER:SSD_D_INNER + SSD_GROUPS * SSD_D_STATE].astype(BF16)
    cm = xbc[:, SSD_D_INNER + SSD_GROUPS * SSD_D_STATE:].astype(BF16)

    row = lax.broadcasted_iota(jnp.int32, (q, LANES), 0) + c * q
    dt = _ssd_dt(small_ref[...], dt_bias_ref, row >= zero_rows)
    da = dt * (-jnp.exp(a_log_ref[...]))
    tril = _tril(q)
    acum = _sel_left(tril.astype(BF16), da)
    acum_t = acum.T
    expand = expand_ref[...]
    dt_exp = _sel_right(expand, dt)
    acum_exp = _sel_right(expand, acum)
    acum_last = acum_exp[q - 1:q, :]
    xdt = xs * dt_exp
    xdt_bf = xdt.astype(BF16)
    x_end = (xdt * jnp.exp(acum_last - acum_exp)).astype(BF16)

    lane_lo = lax.broadcasted_iota(jnp.int32, (q, LANES), 1) < SSD_HEAD_DIM
    for g in range(SSD_GROUPS):
        bg = bm[:, g * SSD_D_STATE:(g + 1) * SSD_D_STATE]
        cg = cm[:, g * SSD_D_STATE:(g + 1) * SSD_D_STATE]
        cb = _dot_nt(cg, bg)
        for hp in range(SSD_HPG // 2):
            lo = (g * SSD_HPG + 2 * hp) * SSD_HEAD_DIM
            pair = []
            for k in range(2):
                h = g * SSD_HPG + 2 * hp + k
                diff = acum[:, h:h + 1] - acum_t[h:h + 1, :]
                seg = jnp.exp(jnp.where(tril, diff, -jnp.inf))
                pair.append(_dot((cb * seg).astype(BF16), xdt_bf[:, lo:lo + LANES]))
            yacc_ref[:, lo:lo + LANES] = jnp.where(lane_lo, pair[0], pair[1])

        gs = slice(g * SSD_GROUP_W, (g + 1) * SSD_GROUP_W)
        st_g = st_ref[:, gs]
        y_off = _dot(cg, st_g.astype(BF16)) * jnp.exp(acum_exp[:, gs])
        yacc_ref[:, gs] = yacc_ref[:, gs] + y_off
        st_ref[:, gs] = st_g * jnp.exp(acum_last[:, gs]) + _dot(bg.T, x_end[:, gs])

    y = yacc_ref[...] + d_exp_ref[...] * xs
    y_ref[0] = _ssd_gated_norm(y, z_ref[...], norm_w_ref).astype(BF16)

    @pl.when(c == nc - 1)
    def _():
        state_ref[0] = st_ref[...].T


def ssd_prompt(proj, batch, n_chunks, zero_rows, state0, tail0, w):
    q = SSD_CHUNK

    def rows(b, c):
        return b * n_chunks + c

    return pl.pallas_call(
        functools.partial(_ssd_prompt_kernel, zero_rows=zero_rows),
        grid=(batch, n_chunks),
        in_specs=[
            pl.BlockSpec((q, SSD_D_INNER), lambda b, c: (rows(b, c), COL_Z // SSD_D_INNER)),
            pl.BlockSpec((q, SSD_D_INNER), lambda b, c: (rows(b, c), COL_XS // SSD_D_INNER)),
            pl.BlockSpec((q, SSD_BC), lambda b, c: (rows(b, c), COL_BC // SSD_BC)),
            pl.BlockSpec((q, LANES), lambda b, c: (rows(b, c), COL_SMALL // LANES)),
            _const_spec((SSD_CONV, SSD_CONV_DIM)),
            _const_spec((1, SSD_CONV_DIM)),
            _const_spec((1, LANES)),
            _const_spec((1, LANES)),
            _const_spec((1, SSD_D_INNER)),
            _const_spec((1, SSD_D_INNER)),
            _const_spec((LANES, SSD_D_INNER)),
            _const_spec((1, SSD_D_INNER, SSD_D_STATE)),
            _const_spec((1, SUBLANES, SSD_CONV_DIM)),
        ],
        out_specs=[
            pl.BlockSpec((1, q, SSD_D_INNER), lambda b, c: (b, c, 0)),
            pl.BlockSpec((1, SSD_D_INNER, SSD_D_STATE), lambda b, c: (b, 0, 0)),
            pl.BlockSpec((1, SUBLANES, SSD_CONV_DIM), lambda b, c: (b, 0, 0)),
        ],
        out_shape=[
            jax.ShapeDtypeStruct((batch, n_chunks * q, SSD_D_INNER), BF16),
            jax.ShapeDtypeStruct((batch, SSD_D_INNER, SSD_D_STATE), F32),
            jax.ShapeDtypeStruct((batch, SUBLANES, SSD_CONV_DIM), F32),
        ],
        scratch_shapes=[
            pltpu.VMEM((SUBLANES + q, SSD_CONV_DIM), F32),
            pltpu.VMEM((SSD_D_STATE, SSD_D_INNER), F32),
            pltpu.VMEM((q, SSD_D_INNER), F32),
        ],
        compiler_params=_cparams("parallel", "arbitrary"),
        name="ssd_prompt",
    )(proj, proj, proj, proj, w["conv_w"], w["conv_b"], w["dt_bias"], w["a_log"], w["d_exp"],
      w["ssd_norm_w"], w["expand"], state0, tail0)


def _ssd_step_pre_kernel(xs_ref, bc_ref, small_ref, c0_ref, c1_ref, c2_ref, conv_w_ref, conv_b_ref,
                         dt_bias_ref, a_log_ref, expand_ref,
                         xact_ref, xdt_ref, decay_ref, bm_ref, cm_ref, convnew_ref):
    x = jnp.concatenate([xs_ref[...], bc_ref[...]], axis=1)
    hist = (c0_ref[...], c1_ref[...], c2_ref[...], x)
    conv = conv_b_ref[...]
    for j in range(SSD_CONV):
        conv = conv + conv_w_ref[j:j + 1, :] * hist[j]
    xbc = jax.nn.silu(conv)
    xs = xbc[:, 0:SSD_D_INNER]
    dt = _ssd_dt(small_ref[...], dt_bias_ref, None)
    da = dt * (-jnp.exp(a_log_ref[...]))
    expand = expand_ref[...]
    xact_ref[...] = xs
    xdt_ref[...] = xs * _sel_right(expand, dt)
    decay_ref[...] = jnp.exp(da)
    bm_ref[...] = xbc[:, SSD_D_INNER:SSD_D_INNER + SSD_GROUPS * SSD_D_STATE]
    cm_ref[...] = xbc[:, SSD_D_INNER + SSD_GROUPS * SSD_D_STATE:]
    convnew_ref[:, 0:SSD_CONV_DIM] = hist[1]
    convnew_ref[:, SSD_CONV_DIM:2 * SSD_CONV_DIM] = hist[2]
    convnew_ref[:, 2 * SSD_CONV_DIM:] = x


def ssd_step_pre(proj, conv_state, w):
    t = proj.shape[0]
    tm = _row_tile(t, 128)
    gw = SSD_GROUPS * SSD_D_STATE
    row = lambda width, col: pl.BlockSpec((tm, width), lambda i: (i, col))
    return pl.pallas_call(
        _ssd_step_pre_kernel,
        grid=(t // tm,),
        in_specs=[
            row(SSD_D_INNER, COL_XS // SSD_D_INNER),
            row(SSD_BC, COL_BC // SSD_BC),
            row(LANES, COL_SMALL // LANES),
            row(SSD_CONV_DIM, 0), row(SSD_CONV_DIM, 1), row(SSD_CONV_DIM, 2),
            _const_spec((SSD_CONV, SSD_CONV_DIM)),
            _const_spec((1, SSD_CONV_DIM)),
            _const_spec((1, LANES)),
            _const_spec((1, LANES)),
            _const_spec((LANES, SSD_D_INNER)),
        ],
        out_specs=[row(SSD_D_INNER, 0), row(SSD_D_INNER, 0), row(LANES, 0), row(gw, 0),
                   row(gw, 0), row((SSD_CONV - 1) * SSD_CONV_DIM, 0)],
        out_shape=[
            jax.ShapeDtypeStruct((t, SSD_D_INNER), F32),
            jax.ShapeDtypeStruct((t, SSD_D_INNER), F32),
            jax.ShapeDtypeStruct((t, LANES), F32),
            jax.ShapeDtypeStruct((t, gw), F32),
            jax.ShapeDtypeStruct((t, gw), F32),
            jax.ShapeDtypeStruct((t, (SSD_CONV - 1) * SSD_CONV_DIM), F32),
        ],
        compiler_params=_cparams("parallel"),
        name="ssd_step_pre",
    )(proj, proj, proj, conv_state, conv_state, conv_state, w["conv_w"], w["conv_b"], w["dt_bias"],
      w["a_log"], w["expand"])


def _dot_tn(a, b):
    return lax.dot_general(a, b, (((0,), (0,)), ((), ())), preferred_element_type=F32)


def _own_row(j):
    return lax.broadcasted_iota(jnp.int32, (SUBLANES, 1), 0) == j


def _ssd_step_state_kernel(s_ref, xdt_ref, dec_ref, bm_ref, cm_ref, xact_ref, z_ref, d_exp_ref, norm_w_ref,
                           snew_ref, y_ref):
    j = pl.program_id(1)

    @pl.when(j == 0)
    def _():
        y_ref[...] = jnp.zeros(y_ref.shape, F32)

    def group_rows(x):
        return jnp.concatenate(
            [x[:, g * SSD_D_STATE:(g + 1) * SSD_D_STATE] for g in range(SSD_GROUPS)], axis=0)

    group_of_lane = lax.broadcasted_iota(jnp.int32, (SUBLANES, SSD_D_INNER), 1) // SSD_GROUP_W
    b_rows = group_rows(bm_ref[...]).astype(BF16)
    y_new = jnp.zeros(y_ref.shape, F32)
    for u in range(DECODE_STEP_SEQS):
        row = j * DECODE_STEP_SEQS + u
        own = _own_row(row)
        dec = dec_ref[pl.ds(row, 1), :]
        xdt = jnp.where(own, xdt_ref[...], 0.0)
        x_rows = jnp.concatenate(
            [jnp.where(group_of_lane == g, xdt, 0.0) for g in range(SSD_GROUPS)], axis=0).astype(BF16)
        c_rows = group_rows(jnp.where(own, cm_ref[...], 0.0)).astype(BF16)
        decayed = [s_ref[u, h * SSD_HEAD_DIM:(h + 1) * SSD_HEAD_DIM, :] * dec[:, h:h + 1]
                   for h in range(SSD_HEADS)]
        s_new = jnp.concatenate(decayed, axis=0) + _dot_tn(x_rows, b_rows)
        snew_ref[u] = s_new
        y_rows = _dot_nt(c_rows, s_new.astype(BF16))
        y_new = y_new + jnp.concatenate(
            [y_rows[g * SUBLANES:(g + 1) * SUBLANES, g * SSD_GROUP_W:(g + 1) * SSD_GROUP_W]
             for g in range(SSD_GROUPS)], axis=1)
    y_ref[...] += y_new

    @pl.when(j == pl.num_programs(1) - 1)
    def _():
        y = y_ref[...] + d_exp_ref[...] * xact_ref[...]
        y_ref[...] = _ssd_gated_norm(y, z_ref[...], norm_w_ref)


def _decode_state_specs(rows, cols):
    steps = SUBLANES // DECODE_STEP_SEQS
    shape = (DECODE_STEP_SEQS, rows, cols)
    index = lambda i, j: (i * steps + j, 0, 0)
    return pl.BlockSpec(shape, index), pl.BlockSpec(shape, index)


def ssd_step_state(state, xdt, dec, bm, cm, xact, proj, w):
    t = state.shape[0]
    assert t % SUBLANES == 0
    gw = SSD_GROUPS * SSD_D_STATE
    rows = lambda width, col=0: pl.BlockSpec((SUBLANES, width), lambda i, j: (i, col))
    st_in, st_out = _decode_state_specs(SSD_D_INNER, SSD_D_STATE)
    return pl.pallas_call(
        _ssd_step_state_kernel,
        grid=(t // SUBLANES, SUBLANES // DECODE_STEP_SEQS),
        in_specs=[st_in, rows(SSD_D_INNER), rows(LANES), rows(gw), rows(gw), rows(SSD_D_INNER),
                  rows(SSD_D_INNER, COL_Z // SSD_D_INNER), _const_spec((1, SSD_D_INNER)),
                  _const_spec((1, SSD_D_INNER))],
        out_specs=[st_out, rows(SSD_D_INNER)],
        out_shape=[
            jax.ShapeDtypeStruct((t, SSD_D_INNER, SSD_D_STATE), F32),
            jax.ShapeDtypeStruct((t, SSD_D_INNER), F32),
        ],
        compiler_params=_cparams("parallel", "arbitrary"),
        name="ssd_step_state",
    )(state, xdt, dec, bm, cm, xact, proj, w["d_exp"], w["ssd_norm_w"])


def _gla_gate(small, gate_up_ref, gate_b_ref):
    pre = _dot(small.astype(BF16), gate_up_ref[...]) + gate_b_ref[...]
    return jax.nn.log_sigmoid(pre) / GLA_GATE_NORMALIZER


def _gla_out_norm(o, g, norm_w_ref):
    parts = []
    for h in range(GLA_HEADS):
        parts.append(_rms_scale(o[:, h * GLA_DV:(h + 1) * GLA_DV]))
    return jnp.concatenate(parts, axis=1) * norm_w_ref[...] * jax.nn.silu(g)


def _gla_prompt_kernel(q_ref, k_ref, v_ref, g_ref, small_ref, gate_up_ref, gate_b_ref, norm_w_ref,
                       state0_ref, o_ref, state_ref, s_ref, *, zero_rows):
    c = pl.program_id(1)
    nc = pl.num_programs(1)
    q = GLA_CHUNK
    rows = q_ref.shape[0]
    n_sub = rows // q

    @pl.when(c == 0)
    def _():
        s_ref[...] = state0_ref[0]

    valid = (lax.broadcasted_iota(jnp.int32, (rows, GLA_KEY_DIM), 0) + c * rows) >= zero_rows
    gk = jnp.where(valid, _gla_gate(small_ref[...], gate_up_ref, gate_b_ref), 0.0)
    kk = jnp.where(valid, k_ref[...], 0.0)
    r = lax.broadcasted_iota(jnp.int32, (rows, rows), 0)
    cc = lax.broadcasted_iota(jnp.int32, (rows, rows), 1)
    chunk_tril = jnp.logical_and(r >= cc, r // q == cc // q)
    bcum = _sel_left(chunk_tril.astype(BF16), gk)
    lasts = [bcum[(i + 1) * q - 1:(i + 1) * q, :] for i in range(n_sub)]
    last_rows = jnp.concatenate([jnp.broadcast_to(x, (q, GLA_KEY_DIM)) for x in lasts], axis=0)
    qe = q_ref[...] * jnp.exp(bcum) * (GLA_DK ** -0.5)
    ke = kk * jnp.exp(-bcum)
    kend = kk * jnp.exp(last_rows - bcum)
    v_all = v_ref[...]
    vv = v_all.astype(BF16)
    chunk_of_row = lax.broadcasted_iota(jnp.int32, (rows, GLA_VAL_DIM), 0) // q
    v_chunk = [jnp.where(chunk_of_row == i, v_all, 0.0).astype(BF16) for i in range(n_sub)]
    outs = [[] for _ in range(n_sub)]
    for h in range(GLA_HEADS):
        ks = slice(h * GLA_DK, (h + 1) * GLA_DK)
        vs = slice(h * GLA_DV, (h + 1) * GLA_DV)
        qe_h = qe[:, ks].astype(BF16)
        att = jnp.where(chunk_tril, _dot_nt(qe_h, ke[:, ks].astype(BF16)), 0.0)
        o_intra = _dot(att.astype(BF16), vv[:, vs])
        kend_t = kend[:, ks].T.astype(BF16)
        s_h = s_ref[ks, :]
        for i in range(n_sub):
            rs = slice(i * q, (i + 1) * q)
            outs[i].append(o_intra[rs, :] + _dot(qe_h[rs, :], s_h.astype(BF16)))
            decay_col = jnp.broadcast_to(jnp.exp(lasts[i][:, ks]), (GLA_DK, GLA_DK)).T
            s_h = (jnp.concatenate([decay_col] * (GLA_DV // GLA_DK), axis=1) * s_h
                   + _dot(kend_t, v_chunk[i][:, vs]))
        s_ref[ks, :] = s_h
    o = jnp.concatenate([jnp.concatenate(o_i, axis=1) for o_i in outs], axis=0)
    o_ref[0] = _gla_out_norm(o, g_ref[...], norm_w_ref).astype(BF16)

    @pl.when(c == nc - 1)
    def _():
        state_ref[0] = s_ref[...]


def gla_prompt(proj, batch, n_rows, zero_rows, state0, w):
    rows = min(GLA_CHUNK * GLA_STEP_CHUNKS, n_rows)
    assert n_rows % rows == 0
    n_steps = n_rows // rows

    def blk(b, c):
        return b * n_steps + c

    return pl.pallas_call(
        functools.partial(_gla_prompt_kernel, zero_rows=zero_rows),
        grid=(batch, n_steps),
        in_specs=[
            pl.BlockSpec((rows, GLA_KEY_DIM), lambda b, c: (blk(b, c), COL_Q // GLA_KEY_DIM)),
            pl.BlockSpec((rows, GLA_KEY_DIM), lambda b, c: (blk(b, c), COL_K // GLA_KEY_DIM)),
            pl.BlockSpec((rows, GLA_VAL_DIM), lambda b, c: (blk(b, c), COL_V // GLA_VAL_DIM)),
            pl.BlockSpec((rows, GLA_VAL_DIM), lambda b, c: (blk(b, c), COL_G // GLA_VAL_DIM)),
            pl.BlockSpec((rows, LANES), lambda b, c: (blk(b, c), COL_SMALL // LANES)),
            _const_spec((LANES, GLA_KEY_DIM)),
            _const_spec((1, GLA_KEY_DIM)),
            _const_spec((1, GLA_VAL_DIM)),
            _const_spec((1, GLA_KEY_DIM, GLA_DV)),
        ],
        out_specs=[
            pl.BlockSpec((1, rows, GLA_VAL_DIM), lambda b, c: (b, c, 0)),
            pl.BlockSpec((1, GLA_KEY_DIM, GLA_DV), lambda b, c: (b, 0, 0)),
        ],
        out_shape=[
            jax.ShapeDtypeStruct((batch, n_steps * rows, GLA_VAL_DIM), BF16),
            jax.ShapeDtypeStruct((batch, GLA_KEY_DIM, GLA_DV), F32),
        ],
        scratch_shapes=[pltpu.VMEM((GLA_KEY_DIM, GLA_DV), F32)],
        compiler_params=_cparams("parallel", "arbitrary"),
        name="gla_prompt",
    )(proj, proj, proj, proj, proj, w["gate_up"], w["gate_b"], w["gla_norm_w"], state0)


def _gla_step_kernel(s_ref, q_ref, k_ref, v_ref, g_ref, small_ref, gate_up_ref, gate_b_ref, norm_w_ref,
                     snew_ref, o_ref, dec_ref, acc_ref):
    j = pl.program_id(1)

    @pl.when(j == 0)
    def _():
        dec_ref[...] = jnp.exp(_gla_gate(small_ref[...], gate_up_ref, gate_b_ref))
        acc_ref[...] = jnp.zeros(acc_ref.shape, F32)

    head_of_lane = lax.broadcasted_iota(jnp.int32, (SUBLANES, GLA_KEY_DIM), 1) // GLA_DK

    def per_head_rows(x):
        return jnp.concatenate([jnp.where(head_of_lane == h, x, 0.0) for h in range(GLA_HEADS)], axis=0)

    vv = v_ref[...]
    v_rows = jnp.concatenate(
        [vv[:, h * GLA_DV:(h + 1) * GLA_DV] for h in range(GLA_HEADS)], axis=0).astype(BF16)
    o_new = jnp.zeros(acc_ref.shape, F32)
    for u in range(DECODE_STEP_SEQS):
        own = _own_row(j * DECODE_STEP_SEQS + u)
        k_rows = per_head_rows(jnp.where(own, k_ref[...], 0.0)).astype(BF16)
        q_rows = per_head_rows(jnp.where(own, q_ref[...] * (GLA_DK ** -0.5), 0.0)).astype(BF16)
        pieces = [p.astype(F32) for p in _split3(jnp.where(own, dec_ref[...], 0.0))]
        d_rows = jnp.concatenate(pieces + [jnp.zeros_like(pieces[0])], axis=0).astype(BF16)
        decay = _dot_tn(d_rows, jnp.ones((d_rows.shape[0], GLA_DV), BF16))
        s_new = s_ref[u] * decay + _dot_tn(k_rows, v_rows)
        snew_ref[u] = s_new
        o_rows = _dot(q_rows, s_new.astype(BF16))
        o_new = o_new + jnp.concatenate(
            [o_rows[h * SUBLANES:(h + 1) * SUBLANES, :] for h in range(GLA_HEADS)], axis=1)
    acc_ref[...] += o_new

    @pl.when(j == pl.num_programs(1) - 1)
    def _():
        o_ref[...] = _gla_out_norm(acc_ref[...], g_ref[...], norm_w_ref)


def gla_step(state, proj, w):
    t = state.shape[0]
    assert t % SUBLANES == 0
    rows = lambda width, col: pl.BlockSpec((SUBLANES, width), lambda i, j: (i, col))
    st_in, st_out = _decode_state_specs(GLA_KEY_DIM, GLA_DV)
    return pl.pallas_call(
        _gla_step_kernel,
        grid=(t // SUBLANES, SUBLANES // DECODE_STEP_SEQS),
        in_specs=[st_in, rows(GLA_KEY_DIM, COL_Q // GLA_KEY_DIM), rows(GLA_KEY_DIM, COL_K // GLA_KEY_DIM),
                  rows(GLA_VAL_DIM, COL_V // GLA_VAL_DIM), rows(GLA_VAL_DIM, COL_G // GLA_VAL_DIM),
                  rows(LANES, COL_SMALL // LANES), _const_spec((LANES, GLA_KEY_DIM)),
                  _const_spec((1, GLA_KEY_DIM)), _const_spec((1, GLA_VAL_DIM))],
        out_specs=[st_out, rows(GLA_VAL_DIM, 0)],
        out_shape=[
            jax.ShapeDtypeStruct((t, GLA_KEY_DIM, GLA_DV), F32),
            jax.ShapeDtypeStruct((t, GLA_VAL_DIM), F32),
        ],
        scratch_shapes=[pltpu.VMEM((SUBLANES, GLA_KEY_DIM), F32), pltpu.VMEM((SUBLANES, GLA_VAL_DIM), F32)],
        compiler_params=_cparams("parallel", "arbitrary"),
        name="gla_step",
    )(state, proj, proj, proj, proj, proj, w["gate_up"], w["gate_b"], w["gla_norm_w"])


def _merge_kernel(x_ref, ys_ref, og_ref, nw_ref, wm_ref, wso_ref, wgo_ref, wout_ref, o_ref):
    x = x_ref[...]
    xn = (_rms_scale(x) * nw_ref[...]).astype(BF16)
    gates = jax.nn.sigmoid(_dot(xn, wm_ref[...]))
    y_ssd = _dot(ys_ref[...].astype(BF16), wso_ref[...])
    y_gla = _dot(og_ref[...].astype(BF16), wgo_ref[...])
    mix = gates[:, 0:D_MODEL] * y_ssd + gates[:, D_MODEL:] * y_gla
    o_ref[...] = x + _dot(mix.astype(BF16), wout_ref[...])


def merge(x, y_ssd, o_gla, w):
    t = x.shape[0]
    tm = _row_tile(t, 512)
    row = lambda width: pl.BlockSpec((tm, width), lambda i: (i, 0))
    return pl.pallas_call(
        _merge_kernel,
        grid=(t // tm,),
        in_specs=[row(D_MODEL), row(SSD_D_INNER), row(GLA_VAL_DIM), _const_spec((1, D_MODEL)),
                  _const_spec((D_MODEL, 2 * D_MODEL)), _const_spec((SSD_D_INNER, D_MODEL)),
                  _const_spec((GLA_VAL_DIM, D_MODEL)), _const_spec((D_MODEL, D_MODEL))],
        out_specs=row(D_MODEL),
        out_shape=jax.ShapeDtypeStruct((t, D_MODEL), F32),
        compiler_params=_cparams("parallel"),
        name="merge",
    )(x, y_ssd, o_gla, w["norm1_w"], w["w_merge"], w["w_ssd_out"], w["w_gla_out"], w["w_out"])


def _merge_exchange_pairs(n):
    pairs = []
    p = n // 2
    while p > 0:
        q, r, d = n // 2, 0, p
        while d > 0:
            pairs += [(i, i + d) for i in range(n - d) if (i & p) == r]
            d, q, r = q - p, q // 2, p
        p //= 2
    return pairs


_SORT16 = _merge_exchange_pairs(PEER_TOPK)
_BITONIC16 = [(i, i + d) for d in (8, 4, 2, 1) for i in range(PEER_TOPK) if (i & d) == 0]


def _exchange(v, ids, i, j):
    if ids is None:
        v[i], v[j] = jnp.maximum(v[i], v[j]), jnp.minimum(v[i], v[j])
    else:
        ge = v[i] >= v[j]
        v[i], v[j] = jnp.where(ge, v[i], v[j]), jnp.where(ge, v[j], v[i])
        ids[i], ids[j] = jnp.where(ge, ids[i], ids[j]), jnp.where(ge, ids[j], ids[i])


def _merge_top16(v, ids, w, wids):
    n, m = len(v), len(w)
    for i in range(n - m, n):
        o = n - 1 - i
        if ids is None:
            v[i] = jnp.maximum(v[i], w[o])
        else:
            ge = v[i] >= w[o]
            v[i] = jnp.where(ge, v[i], w[o])
            ids[i] = jnp.where(ge, ids[i], wids[o])
    for i, j in _BITONIC16:
        _exchange(v, ids, i, j)


def _top16_rows(x, with_ids):
    n = x.shape[0] // SUBLANES
    assert n == PEER_TOPK
    v = [x[SUBLANES * r:SUBLANES * (r + 1), :] for r in range(n)]
    ids = None
    if with_ids:
        row = lax.broadcasted_iota(jnp.int32, v[0].shape, 0).astype(F32)
        ids = [row + float(SUBLANES * r) for r in range(n)]
    for i, j in _SORT16:
        _exchange(v, ids, i, j)
    shift = SUBLANES // 2
    while shift:
        w = [pltpu.roll(a, shift, 0) for a in v]
        wids = [pltpu.roll(a, shift, 0) for a in ids] if with_ids else None
        _merge_top16(v, ids, w, wids)
        shift //= 2
    return v, ids


def _peer_kernel(x_ref, nw_ref, fnw_ref, wqt_ref, keys_ref, u_ref, v_ref, o_ref,
                 hn_ref, qt_ref, rw_ref, s2t_ref, svk_ref, sik_ref, stat_ref, g1k_ref, ids_ref, g1tm_ref,
                 kept8_ref, e8_ref, hid_ref, acc_ref):
    s = pl.program_id(1)
    ns = pl.num_programs(1)
    tb = x_ref.shape[0]
    nk = PEER_N_KEYS
    nj = PEER_HEADS * PEER_TOPK

    @pl.when(s == 0)
    def _select():
        acc_ref[...] = jnp.zeros(acc_ref.shape, F32)
        hn_ref[...] = (_rms_scale(x_ref[...]) * nw_ref[...]).astype(BF16)
        stat_ref[...] = jnp.zeros(stat_ref.shape, F32)
        qt_ref[...] = _dot_nt(wqt_ref[...], hn_ref[...]).astype(BF16)
        for h in range(PEER_HEADS):
            for c in range(2):
                r0 = (h * 2 + c) * (PEER_DQ // 2)
                sc_t = _dot(keys_ref[h * 2 + c], qt_ref[r0:r0 + PEER_DQ // 2, :])
                for g in range(tb // LANES):
                    ls = slice(g * LANES, (g + 1) * LANES)
                    vals, ids = _top16_rows(sc_t[:, ls], c == 0)
                    for k in range(PEER_TOPK):
                        svk_ref[c, k, h:h + 1, ls] = vals[k][0:1, :]
                        if c == 0:
                            sik_ref[k, h:h + 1, ls] = ids[k][0:1, :]
                if c == 1:
                    s2t_ref[h] = sc_t

        pairs = ([(0, k2) for k2 in range(PEER_TOPK)]
                 + [(k1, k2) for k1 in range(1, PEER_TOPK // 2) for k2 in range(PEER_TOPK // (k1 + 1))]
                 + [(k1, 0) for k1 in range(PEER_TOPK // 2, PEER_TOPK)])
        for g in range(tb // LANES):
            ls = slice(g * LANES, (g + 1) * LANES)
            a = [svk_ref[0, k, :, ls] for k in range(PEER_TOPK)]
            b = [svk_ref[1, k, :, ls] for k in range(PEER_TOPK)]
            best = [a[0] + b[k2] for k2 in range(PEER_TOPK)]
            for k1 in range(1, PEER_TOPK // 2):
                _merge_top16(best, None, [a[k1] + b[k2] for k2 in range(PEER_TOPK // (k1 + 1))], None)
            _merge_top16(best, None, [a[k1] + b[0] for k1 in range(PEER_TOPK // 2, PEER_TOPK)], None)
            tau, top = best[PEER_TOPK - 1], best[0]
            z = jnp.zeros_like(tau)
            for k1, k2 in pairs:
                cv = a[k1] + b[k2]
                z = z + jnp.where(cv >= tau, jnp.exp(cv - top), 0.0)
            stat_ref[0:PEER_HEADS, ls] = tau
            stat_ref[PEER_HEADS:2 * PEER_HEADS, ls] = b[0]
            inv_z = 1.0 / z
            for k in range(PEER_TOPK):
                g1k_ref[k, :, ls] = jnp.exp(a[k] - a[0]) * inv_z

        ids_ref[...] = sik_ref[...].reshape(nj, tb).T
        g1tm_ref[...] = g1k_ref[...].reshape(nj, tb).T
        def count_kept(h, carry):
            sc2 = s2t_ref[h]
            tau_t = stat_ref[pl.ds(h, 1), :]
            kept = jnp.zeros(sc2.shape, F32)
            for k1 in range(PEER_TOPK):
                kept = jnp.where((sc2 + svk_ref[0, k1, pl.ds(h, 1), :]) >= tau_t, float(k1 + 1), kept)
            e2 = jnp.exp(sc2 - stat_ref[pl.ds(PEER_HEADS + h, 1), :])
            kept8_ref[pl.ds(h, tb, stride=PEER_HEADS), :] = kept.T
            e8_ref[pl.ds(h, tb, stride=PEER_HEADS), :] = e2.T
            return carry

        lax.fori_loop(0, PEER_HEADS, count_kept, 0)

        key1 = lax.broadcasted_iota(jnp.int32, (nk, nk), 0).astype(F32)

        def scatter(t, carry):
            hb = pl.multiple_of(t * PEER_HEADS, PEER_HEADS)
            kept8 = kept8_ref[pl.ds(hb, PEER_HEADS), :]
            e8 = e8_ref[pl.ds(hb, PEER_HEADS), :]
            rows = jnp.concatenate(
                [jnp.where(kept8 > float(k1), e8, 0.0) for k1 in range(PEER_TOPK)], axis=0).astype(BF16)
            place = jnp.where(key1 == ids_ref[pl.ds(t, 1), :], g1tm_ref[pl.ds(t, 1), :], 0.0)
            rw_ref[pl.ds(t * PEER_PITCH, nk), :] = _dot(place.astype(BF16), rows)
            return carry

        lax.fori_loop(0, tb, scatter, 0, unroll=PEER_SCATTER_UNROLL)

    def produce():
        hn = hn_ref[...]
        half = PEER_STEP_EXPERTS // 2
        for c in range(2):
            hid_ref[:, c * half:(c + 1) * half] = _dot_nt(hn, u_ref[c * half:(c + 1) * half, :])

    def gated_prev():
        prev = hid_ref[...]
        act = 0.5 * prev * (1.0 + lax.erf(prev * (2.0 ** -0.5)))
        slab0 = (s - 1) * PEER_SLABS_PER_STEP
        wts = jnp.concatenate(
            [rw_ref[pl.ds(slab0 + i, tb, stride=PEER_PITCH), :] for i in range(PEER_SLABS_PER_STEP)],
            axis=1)
        return (act * wts).astype(BF16)

    def consume(gated):
        half = D_MODEL // 2
        for c in range(2):
            acc_ref[:, c * half:(c + 1) * half] += _dot(gated, v_ref[:, c * half:(c + 1) * half])

    @pl.when(s == 0)
    def _():
        produce()

    @pl.when(jnp.logical_and(s > 0, s < ns - 1))
    def _():
        gated = gated_prev()
        produce()
        consume(gated)

    @pl.when(s == ns - 1)
    def _():
        consume(gated_prev())
        o_ref[...] = _rms_scale(x_ref[...] + acc_ref[...]) * fnw_ref[...]


def peer_final(x, w):
    t = x.shape[0]
    tb = _row_tile(t, PEER_TOKEN_BLOCK)
    assert tb % LANES == 0
    nblk = PEER_N_EXPERTS // PEER_STEP_EXPERTS
    return pl.pallas_call(
        _peer_kernel,
        grid=(t // tb, nblk + 1),
        in_specs=[
            pl.BlockSpec((tb, D_MODEL), lambda i, s: (i, 0)),
            _const_spec((1, D_MODEL)),
            _const_spec((1, D_MODEL)),
            _const_spec((PEER_HEADS * PEER_DQ, D_MODEL)),
            _const_spec((2 * PEER_HEADS, PEER_N_KEYS, PEER_DQ // 2)),
            pl.BlockSpec((PEER_STEP_EXPERTS, D_MODEL), lambda i, s: (jnp.minimum(s, nblk - 1), 0)),
            pl.BlockSpec((PEER_STEP_EXPERTS, D_MODEL), lambda i, s: (jnp.maximum(s - 1, 0), 0)),
        ],
        out_specs=pl.BlockSpec((tb, D_MODEL), lambda i, s: (i, 0)),
        out_shape=jax.ShapeDtypeStruct((t, D_MODEL), F32),
        scratch_shapes=[
            pltpu.VMEM((tb, D_MODEL), BF16),
            pltpu.VMEM((PEER_HEADS * PEER_DQ, tb), BF16),
            pltpu.VMEM((tb * PEER_PITCH, PEER_N_KEYS), F32),
            pltpu.VMEM((PEER_HEADS, PEER_N_KEYS, tb), F32),
            pltpu.VMEM((2, PEER_TOPK, PEER_HEADS, tb), F32),
            pltpu.VMEM((PEER_TOPK, PEER_HEADS, tb), F32),
            pltpu.VMEM((LANES, tb), F32),
            pltpu.VMEM((PEER_TOPK, PEER_HEADS, tb), F32),
            pltpu.VMEM((tb, PEER_HEADS * PEER_TOPK), F32),
            pltpu.VMEM((tb, PEER_HEADS * PEER_TOPK), F32),
            pltpu.VMEM((tb * PEER_HEADS, PEER_N_KEYS), F32),
            pltpu.VMEM((tb * PEER_HEADS, PEER_N_KEYS), F32),
            pltpu.VMEM((tb, PEER_STEP_EXPERTS), F32),
            pltpu.VMEM((tb, D_MODEL), F32),
        ],
        compiler_params=_cparams("parallel", "arbitrary"),
        name="peer_final",
    )(x, w["norm2_w"], w["final_norm_w"], w["peer_wq_t"], w["peer_keys"], w["peer_u"], w["peer_v"])


def _prepare_weights(norm1_w, w_in, conv_w, conv_b, dt_bias, a_log, d_skip, ssd_norm_w, w_ssd_out,
                     gla_gate_up, gla_gate_b, gla_norm_w, w_gla_out, w_out, norm2_w, peer_w_q,
                     peer_sub_keys, peer_u, peer_v, final_norm_w):
    offs = [0]
    for sz in _IN_SIZES:
        offs.append(offs[-1] + sz)
    z0, xbc0, dt0, q0, _, _, _, glr0, mg0, end = offs
    small = jnp.concatenate(
        [w_in[:, dt0:q0], w_in[:, glr0:mg0],
         jnp.zeros((D_MODEL, SMALL_COLS - SSD_HEADS - GLA_GATE_RANK), w_in.dtype)], axis=1)
    w_proj = jnp.concatenate([w_in[:, z0:dt0], w_in[:, q0:glr0], small], axis=1).astype(BF16)
    assert w_proj.shape[1] == IN_COLS

    def lane_pad(v):
        return jnp.pad(v, (0, LANES - v.shape[0])).reshape(1, LANES)

    head_of_col = jnp.arange(SSD_D_INNER) // SSD_HEAD_DIM
    expand = (jnp.arange(LANES)[:, None] == head_of_col[None, :]).astype(BF16)
    gate_up = jnp.zeros((LANES, GLA_KEY_DIM), F32).at[SMALL_GLR:SMALL_GLR + GLA_GATE_RANK].set(gla_gate_up)
    keys = jnp.transpose(peer_sub_keys, (1, 0, 2, 3)).reshape(2 * PEER_HEADS, PEER_N_KEYS, PEER_DQ // 2)
    return {
        "norm1_w": norm1_w.reshape(1, D_MODEL),
        "w_proj": w_proj,
        "w_merge": w_in[:, mg0:end].astype(BF16),
        "conv_w": conv_w,
        "conv_b": conv_b.reshape(1, SSD_CONV_DIM),
        "dt_bias": lane_pad(dt_bias),
        "a_log": lane_pad(a_log),
        "d_exp": jnp.repeat(d_skip, SSD_HEAD_DIM).reshape(1, SSD_D_INNER),
        "ssd_norm_w": ssd_norm_w.reshape(1, SSD_D_INNER),
        "expand": expand,
        "w_ssd_out": w_ssd_out.astype(BF16),
        "gate_up": gate_up.astype(BF16),
        "gate_b": gla_gate_b.reshape(1, GLA_KEY_DIM),
        "gla_norm_w": jnp.tile(gla_norm_w, GLA_HEADS).reshape(1, GLA_VAL_DIM),
        "w_gla_out": w_gla_out.astype(BF16),
        "w_out": w_out.astype(BF16),
        "norm2_w": norm2_w.reshape(1, D_MODEL),
        "peer_wq_t": peer_w_q.T.astype(BF16),
        "peer_keys": keys.astype(BF16),
        "peer_u": peer_u.astype(BF16),
        "peer_v": peer_v.astype(BF16),
        "final_norm_w": final_norm_w.reshape(1, D_MODEL),
    }


def _prompt_path(x_prompt, meta_tokens, w):
    b, seq, _ = x_prompt.shape
    assert seq % SSD_CHUNK == 0
    meta_rows = jnp.concatenate(
        [jnp.zeros((PROMPT_ZERO_ROWS, D_MODEL), x_prompt.dtype), meta_tokens.astype(x_prompt.dtype)], axis=0)
    meta_proj = norm_matmul(meta_rows, w["norm1_w"], w["w_proj"], IN_COL_TILE)
    zero_ssd = jnp.zeros((1, SSD_D_INNER, SSD_D_STATE), F32)
    zero_tail = jnp.zeros((1, SUBLANES, SSD_CONV_DIM), F32)
    zero_gla = jnp.zeros((1, GLA_KEY_DIM, GLA_DV), F32)
    _, ssd0, tail0 = ssd_prompt(meta_proj, 1, 1, PROMPT_ZERO_ROWS, zero_ssd, zero_tail, w)
    _, gla0 = gla_prompt(meta_proj, 1, SSD_CHUNK, PROMPT_ZERO_ROWS, zero_gla, w)
    proj = norm_matmul(x_prompt.reshape(b * seq, D_MODEL), w["norm1_w"], w["w_proj"], IN_COL_TILE)
    y_ssd, st_ssd, conv_tail = ssd_prompt(proj, b, seq // SSD_CHUNK, 0, ssd0, tail0, w)
    o_gla, st_gla = gla_prompt(proj, b, seq, 0, gla0, w)
    x1 = merge(x_prompt.reshape(b * seq, D_MODEL), y_ssd.reshape(b * seq, SSD_D_INNER),
               o_gla.reshape(b * seq, GLA_VAL_DIM), w)
    y = peer_final(x1, w).reshape(b, seq, D_MODEL)
    return (y,
            st_ssd.reshape(1, b, SSD_HEADS, SSD_HEAD_DIM, SSD_D_STATE),
            conv_tail[:, SUBLANES - (SSD_CONV - 1):, :][None],
            st_gla.reshape(1, b, GLA_HEADS, GLA_DK, GLA_DV))


def _sample_path(x_sample, state_ssd, state_conv, state_gla, w):
    b = x_sample.shape[0]
    x = x_sample.reshape(b, D_MODEL)
    proj = norm_matmul(x, w["norm1_w"], w["w_proj"], IN_COL_TILE)
    xact, xdt, dec, bm, cm, conv_new = ssd_step_pre(
        proj, state_conv.reshape(b, (SSD_CONV - 1) * SSD_CONV_DIM), w)
    st_ssd, y_ssd = ssd_step_state(state_ssd.reshape(b, SSD_D_INNER, SSD_D_STATE), xdt, dec, bm, cm,
                                   xact, proj, w)
    st_gla, o_gla = gla_step(state_gla.reshape(b, GLA_KEY_DIM, GLA_DV), proj, w)
    x1 = merge(x, y_ssd, o_gla, w)
    y = peer_final(x1, w).reshape(b, 1, D_MODEL)
    return (y,
            st_ssd.reshape(1, b, SSD_HEADS, SSD_HEAD_DIM, SSD_D_STATE),
            conv_new.reshape(1, b, SSD_CONV - 1, SSD_CONV_DIM),
            st_gla.reshape(1, b, GLA_HEADS, GLA_DK, GLA_DV))


def kernel(x_prompt, x_sample, state_ssd, state_conv, state_gla, meta_tokens, norm1_w, w_in, conv_w, conv_b,
           dt_bias, a_log, d_skip, ssd_norm_w, w_ssd_out, gla_gate_up, gla_gate_b, gla_norm_w, w_gla_out,
           w_out, norm2_w, peer_w_q, peer_sub_keys, peer_u, peer_v, final_norm_w):
    layer = (norm1_w, w_in, conv_w, conv_b, dt_bias, a_log, d_skip, ssd_norm_w, w_ssd_out, gla_gate_up,
             gla_gate_b, gla_norm_w, w_gla_out, w_out, norm2_w, peer_w_q, peer_sub_keys, peer_u, peer_v)
    assert all(p.shape[0] == 1 for p in layer), "single-layer step"
    w = _prepare_weights(*[p[0] for p in layer], final_norm_w)
    yp, p_ssd, p_conv, p_gla = _prompt_path(x_prompt, meta_tokens, w)
    ys, s_ssd, s_conv, s_gla = _sample_path(x_sample, state_ssd[0], state_conv[0], state_gla[0], w)
    return (yp, ys, p_ssd, p_conv, p_gla, s_ssd, s_conv, s_gla)
```
